```python
import jax, jax.numpy as jnp
from jax import lax
import numpy as np

D_MODEL = 1024
BATCH = 1
SEQ = 16384
DEPTH = 2

GRID_W = 64
CTX_LEN = 256
HEAD_DIM = 64
A_HEADS = 8
A_KV_HEADS = 2
C_HEADS = 8
C_KV_HEADS = 2
WINDOW = 128
BLOCK = 128
BRANCH_WIDTH = A_HEADS * HEAD_DIM
B_WIDTH = BRANCH_WIDTH
CONV_W = 3
N_BRANCH = 3
N_MOD = 6
ROPE_THETA = 10000.0
PEER_HEADS = 8
N_KEYS = 128
N_EXPERTS = N_KEYS * N_KEYS
PEER_TOPK = 16
PEER_QDIM = 256
PEER_BLOCK = 128
IN_COLS = (A_HEADS + 2 * A_KV_HEADS + C_HEADS + 2 * C_KV_HEADS) * HEAD_DIM + 3 * B_WIDTH + N_BRANCH * D_MODEL
DEEPNORM_ALPHA = (2 * DEPTH) ** 0.25
DEEPNORM_BETA = (8 * DEPTH) ** -0.25
LN_EPS = 1e-6
RMS_EPS = 1e-6
NEG_INF = -1e30
ATTN_SCALE = HEAD_DIM ** -0.5

kernel_name = 'hybrid_dit_gqa_conv_swa_peer'


def _in_splits():
    sizes = (A_HEADS * HEAD_DIM, A_KV_HEADS * HEAD_DIM, A_KV_HEADS * HEAD_DIM,
             C_HEADS * HEAD_DIM, C_KV_HEADS * HEAD_DIM, C_KV_HEADS * HEAD_DIM,
             B_WIDTH, B_WIDTH, B_WIDTH, N_BRANCH * D_MODEL)
    return [int(v) for v in np.cumsum(sizes)[:-1]]


def layer_norm(x, g, b):
    xf = x.astype(jnp.float32)
    mu = jnp.mean(xf, axis=-1, keepdims=True)
    xc = xf - mu
    var = jnp.mean(xc * xc, axis=-1, keepdims=True)
    return (xc * lax.rsqrt(var + LN_EPS) * g + b).astype(x.dtype)


def rms_norm(x, g):
    xf = x.astype(jnp.float32)
    return (xf * lax.rsqrt(jnp.mean(xf * xf, axis=-1, keepdims=True) + RMS_EPS) * g).astype(x.dtype)


def modulate(x, shift, scale):
    return x * (1 + scale) + shift


def axial_rope(n_tok):
    rows = n_tok // GRID_W
    row = jnp.repeat(jnp.arange(rows, dtype=jnp.int32), GRID_W)
    col = jnp.tile(jnp.arange(GRID_W, dtype=jnp.int32), rows)
    quarter = HEAD_DIM // 4
    inv_freq = ROPE_THETA ** (-jnp.arange(quarter, dtype=jnp.float32) / quarter)
    ang = jnp.stack([row, col], axis=-1).astype(jnp.float32)[..., None] * inv_freq
    return jnp.cos(ang), jnp.sin(ang)


def apply_rope(x, cos, sin):
    xr = x.astype(jnp.float32).reshape(*x.shape[:-1], 2, 2, HEAD_DIM // 4)
    x1, x2 = xr[..., 0, :], xr[..., 1, :]
    c = cos[None, :, None]
    s = sin[None, :, None]
    out = jnp.stack([x1 * c - x2 * s, x2 * c + x1 * s], axis=-2)
    return out.reshape(x.shape).astype(x.dtype)


def to_blocks(x):
    b, l = x.shape[:2]
    return jnp.moveaxis(x.reshape(b, l // BLOCK, BLOCK, *x.shape[2:]), 1, 0)


def from_blocks(x):
    nb, b = x.shape[:2]
    return jnp.moveaxis(x, 0, 1).reshape(b, nb * BLOCK, *x.shape[3:])


def dense_gqa(q, k, v, sink_g=None):
    s = jnp.einsum('bqkgd,bskd->bkgqs', q, k).astype(jnp.float32) * ATTN_SCALE
    if sink_g is not None:
        s_sink = jnp.broadcast_to(sink_g.astype(jnp.float32)[None, :, :, None, None], s.shape[:-1] + (1,))
        p = jax.nn.softmax(jnp.concatenate([s, s_sink], axis=-1), axis=-1)[..., :-1]
    else:
        p = jax.nn.softmax(s, axis=-1)
    return jnp.einsum('bkgqs,bskd->bqkgd', p.astype(v.dtype), v)


def banded_attention(q, k, v, k_ctx, v_ctx, sink_g):
    s_len = q.shape[1]
    pad = ((0, 0), (BLOCK, BLOCK), (0, 0), (0, 0))

    def band(t):
        tb = to_blocks(jnp.pad(t, pad))
        return jnp.concatenate([tb[:-2], tb[1:-1], tb[2:]], axis=2)

    valid = jnp.pad(jnp.ones((s_len,), jnp.bool_), (BLOCK, BLOCK)).reshape(-1, BLOCK)
    valid = jnp.concatenate([valid[:-2], valid[1:-1], valid[2:]], axis=1)
    rel = jnp.arange(3 * BLOCK)[None, :] - BLOCK - jnp.arange(BLOCK)[:, None]
    in_win = jnp.abs(rel) <= WINDOW
    n_loc = 3 * BLOCK
    n_ctx = k_ctx.shape[1]

    def blk(args):
        qb, kb, vb, vmask = args
        s_loc = jnp.einsum('bqkgd,bskd->bkgqs', qb, kb).astype(jnp.float32) * ATTN_SCALE
        s_loc = jnp.where(in_win & vmask[None, :], s_loc, NEG_INF)
        s_ctx = jnp.einsum('bqkgd,bskd->bkgqs', qb, k_ctx).astype(jnp.float32) * ATTN_SCALE
        s_sink = jnp.broadcast_to(sink_g.astype(jnp.float32)[None, :, :, None, None], s_ctx.shape[:-1] + (1,))
        p = jax.nn.softmax(jnp.concatenate([s_loc, s_ctx, s_sink], axis=-1), axis=-1).astype(vb.dtype)
        return (jnp.einsum('bkgqs,bskd->bqkgd', p[..., :n_loc], vb)
                + jnp.einsum('bkgqs,bskd->bqkgd', p[..., n_loc:n_loc + n_ctx], v_ctx))

    out = lax.map(blk, (to_blocks(q), band(k), band(v), valid))
    return from_blocks(out)


def short_conv(xb, gb, gc, w):
    z = gc * xb
    half = CONV_W // 2
    zp = jnp.pad(z, ((0, 0), (half, half), (0, 0)))
    n = z.shape[1]
    y = sum(w[i] * zp[:, i:i + n] for i in range(CONV_W))
    return gb * y


def merge_branches(ya, yb, yc, gate_logits, w_branch, w_out):
    gates = jax.nn.sigmoid(gate_logits).reshape(*gate_logits.shape[:-1], N_BRANCH, D_MODEL)
    m = sum(gates[..., i, :] * (y @ w_branch[i]) for i, y in enumerate((ya, yb, yc)))
    return m @ w_out


def token_mixer(h_lat, h_ctx, w_in, q_g, k_g, conv_w, sink, w_branch, w_out, cos, sin, need_ctx):
    bsz, s_len, _ = h_lat.shape
    c_len = h_ctx.shape[1]
    g_a = A_HEADS // A_KV_HEADS
    g_c = C_HEADS // C_KV_HEADS
    splits = _in_splits()

    def heads(t, n):
        return t.reshape(*t.shape[:-1], n, HEAD_DIM)

    qa, ka, va, qc, kc, vc, xb, gb, gc, glog = jnp.split(h_lat @ w_in, splits, axis=-1)
    qa_x, ka_x, va_x, qc_x, kc_x, vc_x, xb_x, gb_x, gc_x, glog_x = jnp.split(h_ctx @ w_in, splits, axis=-1)

    qa = apply_rope(rms_norm(heads(qa, A_HEADS), q_g), cos, sin).reshape(bsz, s_len, A_KV_HEADS, g_a, HEAD_DIM)
    ka = apply_rope(rms_norm(heads(ka, A_KV_HEADS), k_g), cos, sin)
    va = heads(va, A_KV_HEADS)
    ka_x = rms_norm(heads(ka_x, A_KV_HEADS), k_g)
    va_x = heads(va_x, A_KV_HEADS)
    k_all = jnp.concatenate([ka_x, ka], axis=1)
    v_all = jnp.concatenate([va_x, va], axis=1)
    ya = from_blocks(lax.map(lambda qb: dense_gqa(qb, k_all, v_all), to_blocks(qa)))
    ya = ya.reshape(bsz, s_len, A_HEADS * HEAD_DIM)

    sink_g = sink.reshape(C_KV_HEADS, g_c)
    qc = apply_rope(heads(qc, C_HEADS), cos, sin).reshape(bsz, s_len, C_KV_HEADS, g_c, HEAD_DIM)
    kc = apply_rope(heads(kc, C_KV_HEADS), cos, sin)
    vc = heads(vc, C_KV_HEADS)
    kc_x = heads(kc_x, C_KV_HEADS)
    vc_x = heads(vc_x, C_KV_HEADS)
    yc = banded_attention(qc, kc, vc, kc_x, vc_x, sink_g).reshape(bsz, s_len, C_HEADS * HEAD_DIM)

    yb = short_conv(xb, gb, gc, conv_w)

    y_lat = merge_branches(ya, yb, yc, glog, w_branch, w_out)

    if need_ctx:
        qa_x = rms_norm(heads(qa_x, A_HEADS), q_g).reshape(bsz, c_len, A_KV_HEADS, g_a, HEAD_DIM)
        ya_x = dense_gqa(qa_x, ka_x, va_x).reshape(bsz, c_len, A_HEADS * HEAD_DIM)
        qc_x = heads(qc_x, C_HEADS).reshape(bsz, c_len, C_KV_HEADS, g_c, HEAD_DIM)
        yc_x = dense_gqa(qc_x, kc_x, vc_x, sink_g).reshape(bsz, c_len, C_HEADS * HEAD_DIM)
        yb_x = short_conv(xb_x, gb_x, gc_x, conv_w)
        y_ctx = merge_branches(ya_x, yb_x, yc_x, glog_x, w_branch, w_out)
    else:
        y_ctx = None
    return y_lat, y_ctx


def peer(h, w_pq, sub_keys, u_tab, v_tab):
    shape = h.shape
    blocks = h.reshape(-1, PEER_BLOCK, D_MODEL)
    half = PEER_QDIM // 2

    def blk(xb):
        q = (xb @ w_pq).reshape(PEER_BLOCK, PEER_HEADS, 2, half)
        s = jnp.einsum('thpd,hpnd->thpn', q, sub_keys).astype(jnp.float32)
        s1, i1 = lax.top_k(s[:, :, 0], PEER_TOPK)
        s2, i2 = lax.top_k(s[:, :, 1], PEER_TOPK)
        cand = (s1[..., :, None] + s2[..., None, :]).reshape(PEER_BLOCK, PEER_HEADS, PEER_TOPK * PEER_TOPK)
        cand_idx = (i1[..., :, None] * N_KEYS + i2[..., None, :]).reshape(PEER_BLOCK, PEER_HEADS, PEER_TOPK * PEER_TOPK)
        top, pos = lax.top_k(cand, PEER_TOPK)
        expert = jnp.take_along_axis(cand_idx, pos, axis=-1)
        g = jax.nn.softmax(top, axis=-1).astype(xb.dtype)
        u = jnp.take(u_tab, expert, axis=0)
        v = jnp.take(v_tab, expert, axis=0)
        a = jax.nn.gelu(jnp.einsum('td,thkd->thk', xb, u), approximate=False) * g
        return jnp.einsum('thk,thkd->td', a, v)

    return lax.map(blk, blocks).reshape(shape)


def setup_inputs(seed: int = 0) -> dict:
    key = jax.random.key(seed)
    ks = jax.random.split(key, 21)
    L = DEPTH
    D = D_MODEL

    def nrm(k, shape, scale):
        return jax.random.normal(k, shape, jnp.float32) * scale

    return {
        'x': nrm(ks[0], (BATCH, SEQ, D), 1.0),
        'c': nrm(ks[1], (BATCH, D), 1.0),
        'ctx': nrm(ks[2], (BATCH, CTX_LEN, D), 1.0),
        'c_ctx': nrm(ks[3], (D,), 1.0),
        'w_mod': nrm(ks[4], (L, D, N_MOD * D), 0.5 * D ** -0.5),
        'b_mod': nrm(ks[5], (L, N_MOD * D), 0.02),
        'w_in': nrm(ks[6], (L, D, IN_COLS), D ** -0.5),
        'q_norm': 1.0 + nrm(ks[7], (L, HEAD_DIM), 0.02),
        'k_norm': 1.0 + nrm(ks[8], (L, HEAD_DIM), 0.02),
        'conv_w': nrm(ks[9], (L, CONV_W, B_WIDTH), CONV_W ** -0.5),
        'sink': nrm(ks[10], (L, C_HEADS), 0.5),
        'w_branch': nrm(ks[11], (L, N_BRANCH, BRANCH_WIDTH, D), DEEPNORM_BETA * BRANCH_WIDTH ** -0.5),
        'w_out': nrm(ks[12], (L, D, D), DEEPNORM_BETA * D ** -0.5),
        'ln1_g': 1.0 + nrm(ks[13], (L, D), 0.02),
        'ln1_b': nrm(ks[14], (L, D), 0.02),
        'w_pq': nrm(ks[15], (L, D, PEER_HEADS * PEER_QDIM), D ** -0.5),
        'sub_keys': nrm(ks[16], (L, PEER_HEADS, 2, N_KEYS, PEER_QDIM // 2), (PEER_QDIM // 2) ** -0.5),
        'u_tab': nrm(ks[17], (L, N_EXPERTS, D), D ** -0.5),
        'v_tab': nrm(ks[18], (L, N_EXPERTS, D), DEEPNORM_BETA * PEER_HEADS ** -0.5),
        'ln2_g': 1.0 + nrm(ks[19], (L, D), 0.02),
        'ln2_b': nrm(ks[20], (L, D), 0.02),
    }


def reference(x, c, ctx, c_ctx, w_mod, b_mod, w_in, q_norm, k_norm, conv_w, sink, w_branch, w_out,
              ln1_g, ln1_b, w_pq, sub_keys, u_tab, v_tab, ln2_g, ln2_b):
    s_len = x.shape[1]
    cos, sin = axial_rope(s_len)
    alpha = DEEPNORM_ALPHA
    for layer in range(DEPTH):
        need_ctx = layer < DEPTH - 1
        mod = (jax.nn.silu(c) @ w_mod[layer] + b_mod[layer])[:, None, :]
        mod_x = jax.nn.silu(c_ctx) @ w_mod[layer] + b_mod[layer]
        sh1, sc1, g1, sh2, sc2, g2 = jnp.split(mod, N_MOD, axis=-1)
        xsh1, xsc1, xg1, xsh2, xsc2, xg2 = jnp.split(mod_x, N_MOD, axis=-1)

        y, y_x = token_mixer(modulate(x, sh1, sc1), modulate(ctx, xsh1, xsc1), w_in[layer],
                             q_norm[layer], k_norm[layer], conv_w[layer], sink[layer],
                             w_branch[layer], w_out[layer], cos, sin, need_ctx)
        x = layer_norm(alpha * x + g1 * y, ln1_g[layer], ln1_b[layer])
        f = peer(modulate(x, sh2, sc2), w_pq[layer], sub_keys[layer], u_tab[layer], v_tab[layer])
        x = layer_norm(alpha * x + g2 * f, ln2_g[layer], ln2_b[layer])

        if need_ctx:
            ctx = layer_norm(alpha * ctx + xg1 * y_x, ln1_g[layer], ln1_b[layer])
            f_x = peer(modulate(ctx, xsh2, xsc2), w_pq[layer], sub_keys[layer], u_tab[layer], v_tab[layer])
            ctx = layer_norm(alpha * ctx + xg2 * f_x, ln2_g[layer], ln2_b[layer])
    return x
```

```python
import functools
import math

import jax
import jax.numpy as jnp
from jax import lax
from jax.experimental import pallas as pl
from jax.experimental.pallas import tpu as pltpu

F32 = jnp.float32
BF16 = jnp.bfloat16

D_MODEL = 1024
HEAD_DIM = 64
N_HEADS = 8
N_KV = 2
GROUP = N_HEADS // N_KV
GRID_W = 64
WINDOW = 128
CONV_W = 3
B_WIDTH = 512
N_MOD = 6
ROPE_THETA = 10000.0
PEER_HEADS = 8
N_KEYS = 128
PEER_TOPK = 16
PEER_QDIM = 256
DEPTH = 2
IN_COLS = 6144
DEEPNORM_ALPHA = (2 * DEPTH) ** 0.25
LN_EPS = 1e-6
RMS_EPS = 1e-6
NEG_INF = -1e30
ATTN_SCALE = HEAD_DIM ** -0.5

ROW_BLOCK = 256
KEY_CHUNK = 512
PEER_TOKENS = 640
PEER_EXPERTS = 1024
VMEM_LIMIT = 56 * 1024 * 1024

CAND_COUNTS = tuple(PEER_TOPK // (j + 1) for j in range(PEER_TOPK))
N_CAND = sum(CAND_COUNTS)
CAND_ROWS = 56


def _cparams(sem):
    return pltpu.CompilerParams(dimension_semantics=sem, vmem_limit_bytes=VMEM_LIMIT)


def _nt_dot(a, b):
    return lax.dot_general(a, b, (((1,), (1,)), ((), ())), preferred_element_type=F32)


def _layer_norm(r, g, b):
    mu = jnp.mean(r, axis=-1, keepdims=True)
    rc = r - mu
    var = jnp.mean(rc * rc, axis=-1, keepdims=True)
    return rc * lax.rsqrt(var + LN_EPS) * g + b


def _mod_kernel(c_ref, w_ref, b_ref, o_ref):
    c = c_ref[...]
    h = (c * jax.nn.sigmoid(c)).astype(BF16)
    o_ref[0] = jnp.dot(h, w_ref[0], preferred_element_type=F32) + b_ref[0]


def _mod_call(cc, w_mod, b_mod):
    depth = w_mod.shape[0]
    cols = w_mod.shape[2]
    tn = 1024
    return pl.pallas_call(
        _mod_kernel,
        grid=(depth, cols // tn),
        in_specs=[
            pl.BlockSpec((8, D_MODEL), lambda l, j: (0, 0)),
            pl.BlockSpec((1, D_MODEL, tn), lambda l, j: (l, 0, j)),
            pl.BlockSpec((1, 1, tn), lambda l, j: (l, 0, j)),
        ],
        out_specs=pl.BlockSpec((1, 8, tn), lambda l, j: (l, 0, j)),
        out_shape=jax.ShapeDtypeStruct((depth, 8, cols), F32),
        compiler_params=_cparams(("arbitrary", "arbitrary")),
        name="mod",
    )(cc, w_mod, b_mod)


def _inproj_kernel(x_ref, mod_ref, w_ref, o_ref):
    sh = mod_ref[0, 0:1, :]
    sc = mod_ref[0, 1:2, :]
    hm = (x_ref[...] * (1.0 + sc) + sh).astype(BF16)
    o_ref[...] = jnp.dot(hm, w_ref[...], preferred_element_type=F32)


def _inproj_call(x_all, mods, w_in, ctx_blocks):
    n = x_all.shape[0]
    tn = 1024
    tb = ROW_BLOCK
    return pl.pallas_call(
        _inproj_kernel,
        grid=(IN_COLS // tn, n // tb),
        in_specs=[
            pl.BlockSpec((tb, D_MODEL), lambda j, i: (i, 0)),
            pl.BlockSpec((1, N_MOD, D_MODEL), lambda j, i: (jnp.where(i < ctx_blocks, 0, 1), 0, 0)),
            pl.BlockSpec((D_MODEL, tn), lambda j, i: (0, j)),
        ],
        out_specs=pl.BlockSpec((tb, tn), lambda j, i: (i, j)),
        out_shape=jax.ShapeDtypeStruct((n, IN_COLS), F32),
        compiler_params=_cparams(("arbitrary", "arbitrary")),
        name="inproj",
    )(x_all, mods, w_in)


def _prep_kernel(p_ref, cos_ref, sa_ref, sb_ref, qg_ref, kg_ref,
                 qa_o, ka_o, va_o, qc_o, kc_o, vc_o):
    x = p_ref[...]
    cos = cos_ref[...]
    sa = sa_ref[...]
    sb = sb_ref[...]

    def tile(t, width):
        reps = width // 128
        return t if reps == 1 else jnp.concatenate([t] * reps, axis=1)

    def rope(y):
        w = y.shape[1]
        return (y * tile(cos, w)
                + pltpu.roll(y, w - 16, 1) * tile(sa, w)
                + pltpu.roll(y, 16, 1) * tile(sb, w))

    def inv_rms(xh):
        return lax.rsqrt(jnp.mean(xh * xh, axis=-1, keepdims=True) + RMS_EPS)

    qa = x[:, 0:512]
    ka = x[:, 512:640]
    va = x[:, 640:768]
    qc = x[:, 768:1280]
    kc = x[:, 1280:1408]
    vc = x[:, 1408:1536]

    qa_r = rope(qa * tile(qg_ref[...], 512))
    ka_r = rope(ka * kg_ref[...])
    qc_r = rope(qc)
    kc_r = rope(kc)

    for h in range(N_HEADS):
        sl = slice(h * HEAD_DIM, (h + 1) * HEAD_DIM)
        qa_o[h] = (qa_r[:, sl] * (inv_rms(qa[:, sl]) * ATTN_SCALE)).astype(BF16)
        qc_o[h] = (qc_r[:, sl] * ATTN_SCALE).astype(BF16)
    for h in range(N_KV):
        sl = slice(h * HEAD_DIM, (h + 1) * HEAD_DIM)
        ka_o[h] = (ka_r[:, sl] * inv_rms(ka[:, sl])).astype(BF16)
        va_o[h] = va[:, sl].astype(BF16)
        kc_o[h] = kc_r[:, sl].astype(BF16)
        vc_o[h] = vc[:, sl].astype(BF16)


def _prep_call(proj, cos_t, sa_t, sb_t, qg, kg):
    n = proj.shape[0]
    tb = ROW_BLOCK
    tab = pl.BlockSpec((tb, 128), lambda i: (i, 0))
    gain = pl.BlockSpec((1, 128), lambda i: (0, 0))

    def hm(nh):
        return pl.BlockSpec((nh, tb, HEAD_DIM), lambda i: (0, i, 0))

    def hshape(nh):
        return jax.ShapeDtypeStruct((nh, n, HEAD_DIM), BF16)

    return pl.pallas_call(
        _prep_kernel,
        grid=(n // tb,),
        in_specs=[pl.BlockSpec((tb, 1536), lambda i: (i, 0)), tab, tab, tab, gain, gain],
        out_specs=[hm(N_HEADS), hm(N_KV), hm(N_KV), hm(N_HEADS), hm(N_KV), hm(N_KV)],
        out_shape=[hshape(N_HEADS), hshape(N_KV), hshape(N_KV), hshape(N_HEADS), hshape(N_KV), hshape(N_KV)],
        compiler_params=_cparams(("arbitrary",)),
        name="prep",
    )(proj, cos_t, sa_t, sb_t, qg, kg)


def _attn_a_kernel(q_ref, k_ref, v_ref, o_ref, m_scr, l_scr, acc_scr, *, ctx_len, ctx_blocks, lat_chunks):
    i = pl.program_id(1)
    tq = q_ref.shape[1]
    q = q_ref[...].reshape(GROUP * tq, HEAD_DIM)

    m_scr[...] = jnp.full(m_scr.shape, -jnp.inf, F32)
    l_scr[...] = jnp.zeros(l_scr.shape, F32)
    acc_scr[...] = jnp.zeros(acc_scr.shape, F32)

    def step(k, v):
        s = _nt_dot(q, k)
        m_old = m_scr[...]
        m_new = jnp.maximum(m_old, jnp.max(s, axis=-1, keepdims=True))
        alpha = jnp.exp(m_old - m_new)
        p = jnp.exp(s - m_new)
        l_scr[...] = alpha * l_scr[...] + jnp.sum(p, axis=-1, keepdims=True)
        acc_scr[...] = alpha * acc_scr[...] + jnp.dot(p.astype(BF16), v, preferred_element_type=F32)
        m_scr[...] = m_new

    step(k_ref[0, 0:ctx_len, :], v_ref[0, 0:ctx_len, :])

    def body(c, carry):
        off = pl.multiple_of(ctx_len + c * KEY_CHUNK, 128)
        step(k_ref[0, pl.ds(off, KEY_CHUNK), :], v_ref[0, pl.ds(off, KEY_CHUNK), :])
        return carry

    lax.fori_loop(0, jnp.where(i < ctx_blocks, 0, lat_chunks), body, 0)

    o = acc_scr[...] / l_scr[...]
    o_ref[...] = o.reshape(GROUP, tq, HEAD_DIM).astype(BF16)


def _attn_a_call(q, k, v, ctx_len):
    n = q.shape[1]
    tq = ROW_BLOCK
    kern = functools.partial(_attn_a_kernel, ctx_len=ctx_len, ctx_blocks=ctx_len // tq,
                             lat_chunks=(n - ctx_len) // KEY_CHUNK)
    return pl.pallas_call(
        kern,
        grid=(N_KV, n // tq),
        in_specs=[
            pl.BlockSpec((GROUP, tq, HEAD_DIM), lambda j, i: (j, i, 0)),
            pl.BlockSpec((1, n, HEAD_DIM), lambda j, i: (j, 0, 0)),
            pl.BlockSpec((1, n, HEAD_DIM), lambda j, i: (j, 0, 0)),
        ],
        out_specs=pl.BlockSpec((GROUP, tq, HEAD_DIM), lambda j, i: (j, i, 0)),
        out_shape=jax.ShapeDtypeStruct((N_HEADS, n, HEAD_DIM), BF16),
        scratch_shapes=[
            pltpu.VMEM((GROUP * tq, 1), F32),
            pltpu.VMEM((GROUP * tq, 1), F32),
            pltpu.VMEM((GROUP * tq, HEAD_DIM), F32),
        ],
        compiler_params=_cparams(("arbitrary", "arbitrary")),
        name="attn_global",
    )(q, k, v)


def _attn_c_kernel(sink_ref, q_ref, k_ref, v_ref, o_ref, *, ctx_len, ctx_blocks, s_len):
    j = pl.program_id(0)
    i = pl.program_id(1)
    tq = q_ref.shape[1]
    rows = GROUP * tq
    n_loc = tq + 2 * WINDOW
    q = q_ref[...].reshape(rows, HEAD_DIM)

    p0 = i * tq - ctx_len
    ws = pl.multiple_of(jnp.clip(p0 - WINDOW, 0, s_len - n_loc), 128)
    k_loc = k_ref[0, pl.ds(ctx_len + ws, n_loc), :]
    v_loc = v_ref[0, pl.ds(ctx_len + ws, n_loc), :]
    k_ctx = k_ref[0, 0:ctx_len, :]
    v_ctx = v_ref[0, 0:ctx_len, :]

    s_loc = _nt_dot(q, k_loc)
    t_pos = p0 + (lax.broadcasted_iota(jnp.int32, (rows, n_loc), 0) & (tq - 1))
    s_pos = ws + lax.broadcasted_iota(jnp.int32, (rows, n_loc), 1)
    valid = (jnp.abs(t_pos - s_pos) <= WINDOW) & (i >= ctx_blocks)
    s_loc = jnp.where(valid, s_loc, NEG_INF)
    s_ctx = _nt_dot(q, k_ctx)

    grp = lax.broadcasted_iota(jnp.int32, (rows, 1), 0) // tq
    sink = jnp.full((rows, 1), sink_ref[j * GROUP + GROUP - 1], F32)
    for g in range(GROUP - 1):
        sink = jnp.where(grp == g, sink_ref[j * GROUP + g], sink)

    m = jnp.maximum(jnp.maximum(jnp.max(s_loc, axis=-1, keepdims=True),
                                jnp.max(s_ctx, axis=-1, keepdims=True)), sink)
    p_loc = jnp.exp(s_loc - m)
    p_ctx = jnp.exp(s_ctx - m)
    l = (jnp.sum(p_loc, axis=-1, keepdims=True) + jnp.sum(p_ctx, axis=-1, keepdims=True)
         + jnp.exp(sink - m))
    o = (jnp.dot(p_loc.astype(BF16), v_loc, preferred_element_type=F32)
         + jnp.dot(p_ctx.astype(BF16), v_ctx, preferred_element_type=F32)) / l
    o_ref[...] = o.reshape(GROUP, tq, HEAD_DIM).astype(BF16)


def _attn_c_call(sink, q, k, v, ctx_len):
    n = q.shape[1]
    tq = ROW_BLOCK
    kern = functools.partial(_attn_c_kernel, ctx_len=ctx_len, ctx_blocks=ctx_len // tq, s_len=n - ctx_len)
    return pl.pallas_call(
        kern,
        grid=(N_KV, n // tq),
        in_specs=[
            pl.BlockSpec(memory_space=pltpu.SMEM),
            pl.BlockSpec((GROUP, tq, HEAD_DIM), lambda j, i: (j, i, 0)),
            pl.BlockSpec((1, n, HEAD_DIM), lambda j, i: (j, 0, 0)),
            pl.BlockSpec((1, n, HEAD_DIM), lambda j, i: (j, 0, 0)),
        ],
        out_specs=pl.BlockSpec((GROUP, tq, HEAD_DIM), lambda j, i: (j, i, 0)),
        out_shape=jax.ShapeDtypeStruct((N_HEADS, n, HEAD_DIM), BF16),
        compiler_params=_cparams(("arbitrary", "arbitrary")),
        name="attn_window",
    )(sink, q, k, v)


def _merge_kernel(x_ref, mod_ref, ya_ref, yc_ref, xb_ref, gb_ref, gc_ref,
                  xbp_ref, gcp_ref, xbn_ref, gcn_ref, gl_ref, cw_ref, wb_ref, wo_ref,
                  lg_ref, lb_ref, wpq_ref, x1_ref, hq_ref, qp_ref, *, n_blocks, ctx_blocks):
    i = pl.program_id(0)
    tb = x_ref.shape[0]

    z = gc_ref[...] * xb_ref[...]
    z_before = gcp_ref[7:8, :] * xbp_ref[7:8, :]
    z_after = gcn_ref[0:1, :] * xbn_ref[0:1, :]
    seq_start = (i == 0) | (i == ctx_blocks)
    seq_end = (i == ctx_blocks - 1) | (i == n_blocks - 1)
    z_before = jnp.where(seq_start, 0.0, z_before)
    z_after = jnp.where(seq_end, 0.0, z_after)
    rid = lax.broadcasted_iota(jnp.int32, z.shape, 0)
    z_prev = jnp.where(rid == 0, z_before, pltpu.roll(z, 1, 0))
    z_next = jnp.where(rid == tb - 1, z_after, pltpu.roll(z, tb - 1, 0))
    cw = cw_ref[...]
    yb = gb_ref[...] * (cw[0:1, :] * z_prev + cw[1:2, :] * z + cw[2:3, :] * z_next)

    def head_proj(y_ref, b):
        acc = jnp.dot(y_ref[0], wb_ref[b, 0:HEAD_DIM, :], preferred_element_type=F32)
        for h in range(1, N_HEADS):
            acc = acc + jnp.dot(y_ref[h], wb_ref[b, h * HEAD_DIM:(h + 1) * HEAD_DIM, :],
                                preferred_element_type=F32)
        return acc

    pa = head_proj(ya_ref, 0)
    pb = jnp.dot(yb.astype(BF16), wb_ref[1], preferred_element_type=F32)
    pc = head_proj(yc_ref, 2)
    gates = jax.nn.sigmoid(gl_ref[...])
    m = (gates[:, 0:D_MODEL] * pa + gates[:, D_MODEL:2 * D_MODEL] * pb
         + gates[:, 2 * D_MODEL:3 * D_MODEL] * pc)
    y = jnp.dot(m.astype(BF16), wo_ref[...], preferred_element_type=F32)

    g1 = mod_ref[0, 2:3, :]
    sh2 = mod_ref[0, 3:4, :]
    sc2 = mod_ref[0, 4:5, :]
    x1 = _layer_norm(DEEPNORM_ALPHA * x_ref[...] + g1 * y, lg_ref[...], lb_ref[...])
    x1_ref[...] = x1
    hq = (x1 * (1.0 + sc2) + sh2).astype(BF16)
    hq_ref[...] = hq
    qp_ref[...] = jnp.dot(hq, wpq_ref[...], preferred_element_type=F32)


def _merge_call(x_all, mods, ya, yc, proj, conv_w, w_branch, w_out, ln_g, ln_b, w_pq, ctx_blocks):
    n = x_all.shape[0]
    tb = ROW_BLOCK
    nb = n // tb
    halo = tb // 8
    last8 = n // 8 - 1
    kern = functools.partial(_merge_kernel, n_blocks=nb, ctx_blocks=ctx_blocks)
    qcols = PEER_HEADS * PEER_QDIM

    def full(shape):
        return pl.BlockSpec(shape, lambda i: (0,) * len(shape))

    def cols512(c):
        return pl.BlockSpec((tb, B_WIDTH), lambda i: (i, c))

    def before(c):
        return pl.BlockSpec((8, B_WIDTH), lambda i: (jnp.maximum(i * halo - 1, 0), c))

    def after(c):
        return pl.BlockSpec((8, B_WIDTH), lambda i: (jnp.minimum((i + 1) * halo, last8), c))

    return pl.pallas_call(
        kern,
        grid=(nb,),
        in_specs=[
            pl.BlockSpec((tb, D_MODEL), lambda i: (i, 0)),
            pl.BlockSpec((1, N_MOD, D_MODEL), lambda i: (jnp.where(i < ctx_blocks, 0, 1), 0, 0)),
            pl.BlockSpec((N_HEADS, tb, HEAD_DIM), lambda i: (0, i, 0)),
            pl.BlockSpec((N_HEADS, tb, HEAD_DIM), lambda i: (0, i, 0)),
            cols512(3), cols512(4), cols512(5),
            before(3), before(5), after(3), after(5),
            pl.BlockSpec((tb, 3 * D_MODEL), lambda i: (i, 1)),
            full((CONV_W, B_WIDTH)),
            full((3, B_WIDTH, D_MODEL)),
            full((D_MODEL, D_MODEL)),
            full((1, D_MODEL)), full((1, D_MODEL)),
            full((D_MODEL, qcols)),
        ],
        out_specs=[
            pl.BlockSpec((tb, D_MODEL), lambda i: (i, 0)),
            pl.BlockSpec((tb, D_MODEL), lambda i: (i, 0)),
            pl.BlockSpec((tb, qcols), lambda i: (i, 0)),
        ],
        out_shape=[
            jax.ShapeDtypeStruct((n, D_MODEL), F32),
            jax.ShapeDtypeStruct((n, D_MODEL), BF16),
            jax.ShapeDtypeStruct((n, qcols), F32),
        ],
        compiler_params=_cparams(("arbitrary",)),
        name="merge",
    )(x_all, mods, ya, yc, proj, proj, proj, proj, proj, proj, proj, proj,
      conv_w, w_branch, w_out, ln_g, ln_b, w_pq)


def _extract_topk(s, rounds, on_round):
    rows = s.shape[0]
    ridx = lax.broadcasted_iota(jnp.int32, s.shape, 0)
    rank = jnp.full(s.shape, rounds, jnp.int32)
    for r in range(rounds):
        m = jnp.max(s, axis=0, keepdims=True)
        first = jnp.min(jnp.where(s == m, ridx, rows), axis=0, keepdims=True)
        hit = ridx == first
        rank = jnp.where(hit, r, rank)
        s = jnp.where(hit, -jnp.inf, s)
        on_round(r, m)
    return rank


def _topk_kernel(qp_ref, sk_ref, r2_o, w2_o, k1_o, w1_o, tv_scr, cand_scr, kj_scr):
    tt = qp_ref.shape[0]
    half = PEER_QDIM // 2
    q = qp_ref[...].astype(BF16)

    scores = []
    ranks = []
    for p in range(2):
        s = _nt_dot(sk_ref[0, p], q[:, p * half:(p + 1) * half])

        def keep(r, m, p=p):
            tv_scr[p, r:r + 1, :] = m

        ranks.append(_extract_topk(s, PEER_TOPK, keep))
        scores.append(s)

    tv1 = tv_scr[0]
    tv2 = tv_scr[1]
    off = 0
    for j, cnt in enumerate(CAND_COUNTS):
        cand_scr[off:off + cnt, :] = tv1[j:j + 1, :] + tv2[0:cnt, :]
        off += cnt
    cand_scr[N_CAND:CAND_ROWS, :] = jnp.full((CAND_ROWS - N_CAND, tt), -jnp.inf, F32)
    cand = cand_scr[...]
    crank = _extract_topk(cand, PEER_TOPK, lambda r, m: None)
    chosen = crank < PEER_TOPK
    cmax = cand[0:1, :]
    z = jnp.sum(jnp.where(chosen, jnp.exp(cand - cmax), 0.0), axis=0, keepdims=True)
    cnt_f = chosen.astype(F32)
    off = 0
    for j, cnt in enumerate(CAND_COUNTS):
        kj_scr[j:j + 1, :] = jnp.sum(cnt_f[off:off + cnt, :], axis=0, keepdims=True)
        off += cnt

    rank1, rank2 = ranks
    k1 = jnp.zeros((N_KEYS, tt), F32)
    for j in range(PEER_TOPK):
        k1 = jnp.where(rank1 == j, kj_scr[j:j + 1, :], k1)
    k1_o[0] = k1
    w1_o[0] = jnp.where(rank1 < PEER_TOPK, jnp.exp(scores[0] - tv1[0:1, :]), 0.0) / z
    r2_o[0] = rank2.astype(F32)
    w2_o[0] = jnp.where(rank2 < PEER_TOPK, jnp.exp(scores[1] - tv2[0:1, :]), 0.0)


def _topk_call(qp, sub_keys):
    n = qp.shape[0]
    tt = ROW_BLOCK
    out = pl.BlockSpec((1, N_KEYS, tt), lambda t, h: (h, 0, t))
    shape = jax.ShapeDtypeStruct((PEER_HEADS, N_KEYS, n), F32)
    return pl.pallas_call(
        _topk_kernel,
        grid=(n // tt, PEER_HEADS),
        in_specs=[
            pl.BlockSpec((tt, PEER_QDIM), lambda t, h: (t, h)),
            pl.BlockSpec((1, 2, N_KEYS, PEER_QDIM // 2), lambda t, h: (h, 0, 0, 0)),
        ],
        out_specs=[out, out, out, out],
        out_shape=[shape, shape, shape, shape],
        scratch_shapes=[
            pltpu.VMEM((2, PEER_TOPK, tt), F32),
            pltpu.VMEM((CAND_ROWS, tt), F32),
            pltpu.VMEM((PEER_TOPK, tt), F32),
        ],
        compiler_params=_cparams(("arbitrary", "arbitrary")),
        name="peer_topk",
    )(qp, sub_keys)


def _erf(x):
    x = jnp.clip(x, -4.0, 4.0)
    x2 = x * x
    a = jnp.float32(-2.72614225801306e-10)
    for c in (2.77068142495902e-08, -2.10102402082508e-06, -5.69250639462346e-05,
              -7.34990630326855e-04, -2.95459980854025e-03, -1.60960333262415e-02):
        a = a * x2 + jnp.float32(c)
    b = jnp.float32(-1.45660718464996e-05)
    for c in (-2.13374055278905e-04, -1.68282697438203e-03, -7.37332916720468e-03,
              -1.42647390514189e-02):
        b = b * x2 + jnp.float32(c)
    return x * a / b


def _dense_kernel(hq_ref, u_ref, vt_ref, r2_ref, w2_ref, k1_ref, w1_ref, o_ref, h_scr, a_scr):
    e = pl.program_id(1)
    n_i1 = u_ref.shape[0] // N_KEYS
    tt = hq_ref.shape[0]
    lane_chunks = tt // 128

    @pl.when(e == 0)
    def _():
        o_ref[...] = jnp.zeros(o_ref.shape, F32)

    h_scr[...] = _nt_dot(u_ref[...], hq_ref[...])

    i1_rows = pl.ds(pl.multiple_of(e * n_i1, 8), n_i1)

    for ii in range(n_i1):

        def chunk(c, carry, ii=ii):
            lanes = pl.ds(pl.multiple_of(c * 128, 128), 128)
            g = jnp.zeros((N_KEYS, 128), F32)
            for h in range(PEER_HEADS):
                k1_row = k1_ref[h, i1_rows, lanes][ii:ii + 1, :]
                w1_row = w1_ref[h, i1_rows, lanes][ii:ii + 1, :]
                keep = r2_ref[h, :, lanes] < k1_row
                g = g + jnp.where(keep, w2_ref[h, :, lanes] * w1_row, 0.0)
            hh = h_scr[ii * N_KEYS:(ii + 1) * N_KEYS, lanes]
            act = 0.5 * hh * (1.0 + _erf(hh * (2.0 ** -0.5)))
            a_scr[ii * N_KEYS:(ii + 1) * N_KEYS, lanes] = (act * g).astype(BF16)
            return carry

        lax.fori_loop(0, lane_chunks, chunk, 0)

    o_ref[...] += jnp.dot(vt_ref[...], a_scr[...], preferred_element_type=F32)


def _dense_call(hq, u_tab, vt_tab, r2, w2, k1, w1):
    n = hq.shape[0]
    n_exp = u_tab.shape[0]
    tt = PEER_TOKENS
    eb = PEER_EXPERTS
    tab = pl.BlockSpec((PEER_HEADS, N_KEYS, tt), lambda t, e: (0, 0, t))
    return pl.pallas_call(
        _dense_kernel,
        grid=(n // tt, n_exp // eb),
        in_specs=[
            pl.BlockSpec((tt, D_MODEL), lambda t, e: (t, 0)),
            pl.BlockSpec((eb, D_MODEL), lambda t, e: (e, 0)),
            pl.BlockSpec((D_MODEL, eb), lambda t, e: (0, e)),
            tab, tab, tab, tab,
        ],
        out_specs=pl.BlockSpec((D_MODEL, tt), lambda t, e: (0, t)),
        out_shape=jax.ShapeDtypeStruct((D_MODEL, n), F32),
        scratch_shapes=[pltpu.VMEM((eb, tt), F32), pltpu.VMEM((eb, tt), BF16)],
        compiler_params=_cparams(("arbitrary", "arbitrary")),
        name="peer_dense",
    )(hq, u_tab, vt_tab, r2, w2, k1, w1)


def _ln2_kernel(x1_ref, ft_ref, mod_ref, lg_ref, lb_ref, o_ref):
    g2 = mod_ref[0, 5:6, :]
    f = ft_ref[...].T
    o_ref[...] = _layer_norm(DEEPNORM_ALPHA * x1_ref[...] + g2 * f, lg_ref[...], lb_ref[...])


def _ln2_call(x1, ft, mods, ln_g, ln_b, ctx_blocks):
    n = x1.shape[0]
    tb = ROW_BLOCK
    return pl.pallas_call(
        _ln2_kernel,
        grid=(n // tb,),
        in_specs=[
            pl.BlockSpec((tb, D_MODEL), lambda i: (i, 0)),
            pl.BlockSpec((D_MODEL, tb), lambda i: (0, i)),
            pl.BlockSpec((1, N_MOD, D_MODEL), lambda i: (jnp.where(i < ctx_blocks, 0, 1), 0, 0)),
            pl.BlockSpec((1, D_MODEL), lambda i: (0, 0)),
            pl.BlockSpec((1, D_MODEL), lambda i: (0, 0)),
        ],
        out_specs=pl.BlockSpec((tb, D_MODEL), lambda i: (i, 0)),
        out_shape=jax.ShapeDtypeStruct((n, D_MODEL), F32),
        compiler_params=_cparams(("arbitrary",)),
        name="ln2",
    )(x1, ft, mods, ln_g, ln_b)


def _rope_tables(ctx_len, s_len):
    t = jnp.arange(s_len, dtype=jnp.int32)
    pos = jnp.stack([t // GRID_W, t % GRID_W], axis=-1).astype(F32)
    quarter = HEAD_DIM // 4
    inv_freq = ROPE_THETA ** (-jnp.arange(quarter, dtype=F32) / quarter)
    ang = pos[:, :, None] * inv_freq
    cos = jnp.repeat(jnp.cos(ang), 2, axis=1).reshape(s_len, HEAD_DIM)
    sin = jnp.repeat(jnp.sin(ang), 2, axis=1).reshape(s_len, HEAD_DIM)
    first_half = (jnp.arange(HEAD_DIM) % 32) < 16
    sa = jnp.where(first_half, -sin, 0.0)
    sb = jnp.where(first_half, 0.0, sin)

    def full(tab, fill):
        tab = jnp.concatenate([jnp.full((ctx_len, HEAD_DIM), fill, F32), tab], axis=0)
        return jnp.tile(tab, (1, 2))

    return full(cos, 1.0), full(sa, 0.0), full(sb, 0.0)


def kernel(x, c, ctx, c_ctx, w_mod, b_mod, w_in, q_norm, k_norm, conv_w, sink, w_branch, w_out,
           ln1_g, ln1_b, w_pq, sub_keys, u_tab, v_tab, ln2_g, ln2_b):
    bsz, s_len, d = x.shape
    ctx_len = ctx.shape[1]
    depth = w_mod.shape[0]
    assert bsz == 1 and d == D_MODEL and depth == DEPTH
    assert ctx_len % ROW_BLOCK == 0 and s_len % KEY_CHUNK == 0
    assert (ctx_len + s_len) % PEER_TOKENS == 0
    ctx_blocks = ctx_len // ROW_BLOCK

    x_all = jnp.concatenate([ctx[0], x[0]], axis=0)
    cos_t, sa_t, sb_t = _rope_tables(ctx_len, s_len)

    cc = jnp.zeros((8, D_MODEL), F32).at[0].set(c_ctx).at[1].set(c[0])
    mod_all = _mod_call(cc, w_mod.astype(BF16), b_mod[:, None, :])

    for layer in range(depth):
        mods = mod_all[layer, 0:2].reshape(2, N_MOD, D_MODEL)
        proj = _inproj_call(x_all, mods, w_in[layer].astype(BF16), ctx_blocks)
        qg = jnp.tile(q_norm[layer], 2)[None, :]
        kg = jnp.tile(k_norm[layer], 2)[None, :]
        qa, ka, va, qc, kc, vc = _prep_call(proj, cos_t, sa_t, sb_t, qg, kg)
        ya = _attn_a_call(qa, ka, va, ctx_len)
        yc = _attn_c_call(sink[layer], qc, kc, vc, ctx_len)
        x1, hq, qp = _merge_call(x_all, mods, ya, yc, proj, conv_w[layer],
                                 w_branch[layer].astype(BF16), w_out[layer].astype(BF16),
                                 ln1_g[layer][None, :], ln1_b[layer][None, :],
                                 w_pq[layer].astype(BF16), ctx_blocks)
        r2, w2, k1, w1 = _topk_call(qp, sub_keys[layer].astype(BF16))
        ft = _dense_call(hq, u_tab[layer].astype(BF16), v_tab[layer].T.astype(BF16), r2, w2, k1, w1)
        x_all = _ln2_call(x1, ft, mods, ln2_g[layer][None, :], ln2_b[layer][None, :], ctx_blocks)

    return x_all[ctx_len:][None]
```

```python
import functools
import math

import jax
import jax.numpy as jnp
from jax import lax
from jax.experimental import pallas as pl
from jax.experimental.pallas import tpu as pltpu

F32 = jnp.float32
BF16 = jnp.bfloat16

D_MODEL = 1024
HEAD_DIM = 64
N_HEADS = 8
N_KV = 2
GROUP = N_HEADS // N_KV
GRID_W = 64
WINDOW = 128
CONV_W = 3
B_WIDTH = 512
N_MOD = 6
ROPE_THETA = 10000.0
PEER_HEADS = 8
N_KEYS = 128
PEER_TOPK = 16
PEER_QDIM = 256
DEPTH = 2
IN_COLS = 6144
DEEPNORM_ALPHA = (2 * DEPTH) ** 0.25
LN_EPS = 1e-6
RMS_EPS = 1e-6
NEG_INF = -1e30
ATTN_SCALE = HEAD_DIM ** -0.5

ROW_BLOCK = 256
KEY_CHUNK = 512
PEER_TOKENS = 640
PEER_EXPERTS = 1024
VMEM_LIMIT = 56 * 1024 * 1024

CAND_COUNTS = tuple(PEER_TOPK // (j + 1) for j in range(PEER_TOPK))
N_CAND = sum(CAND_COUNTS)
CAND_ROWS = 56


def _cparams(sem):
    return pltpu.CompilerParams(dimension_semantics=sem, vmem_limit_bytes=VMEM_LIMIT)


def _nt_dot(a, b):
    return lax.dot_general(a, b, (((1,), (1,)), ((), ())), preferred_element_type=F32)


def _layer_norm(r, g, b):
    mu = jnp.mean(r, axis=-1, keepdims=True)
    rc = r - mu
    var = jnp.mean(rc * rc, axis=-1, keepdims=True)
    return rc * lax.rsqrt(var + LN_EPS) * g + b


def _mod_kernel(c_ref, w_ref, b_ref, o_ref):
    c = c_ref[...]
    h = (c * jax.nn.sigmoid(c)).astype(BF16)
    o_ref[0] = jnp.dot(h, w_ref[0], preferred_element_type=F32) + b_ref[0]


def _mod_call(cc, w_mod, b_mod):
    depth = w_mod.shape[0]
    cols = w_mod.shape[2]
    tn = 1024
    return pl.pallas_call(
        _mod_kernel,
        grid=(depth, cols // tn),
        in_specs=[
            pl.BlockSpec((8, D_MODEL), lambda l, j: (0, 0)),
            pl.BlockSpec((1, D_MODEL, tn), lambda l, j: (l, 0, j)),
            pl.BlockSpec((1, 1, tn), lambda l, j: (l, 0, j)),
        ],
        out_specs=pl.BlockSpec((1, 8, tn), lambda l, j: (l, 0, j)),
        out_shape=jax.ShapeDtypeStruct((depth, 8, cols), F32),
        compiler_params=_cparams(("arbitrary", "arbitrary")),
        name="mod",
    )(cc, w_mod, b_mod)


def _inproj_kernel(x_ref, mod_ref, w_ref, o_ref):
    sh = mod_ref[0, 0:1, :]
    sc = mod_ref[0, 1:2, :]
    hm = (x_ref[...] * (1.0 + sc) + sh).astype(BF16)
    o_ref[...] = jnp.dot(hm, w_ref[...], preferred_element_type=F32)


def _inproj_call(x_all, mods, w_in, ctx_blocks):
    n = x_all.shape[0]
    tn = 1024
    tb = ROW_BLOCK
    return pl.pallas_call(
        _inproj_kernel,
        grid=(IN_COLS // tn, n // tb),
        in_specs=[
            pl.BlockSpec((tb, D_MODEL), lambda j, i: (i, 0)),
            pl.BlockSpec((1, N_MOD, D_MODEL), lambda j, i: (jnp.where(i < ctx_blocks, 0, 1), 0, 0)),
            pl.BlockSpec((D_MODEL, tn), lambda j, i: (0, j)),
        ],
        out_specs=pl.BlockSpec((tb, tn), lambda j, i: (i, j)),
        out_shape=jax.ShapeDtypeStruct((n, IN_COLS), F32),
        compiler_params=_cparams(("arbitrary", "arbitrary")),
        name="inproj",
    )(x_all, mods, w_in)


def _prep_kernel(p_ref, cos_ref, sa_ref, sb_ref, qg_ref, kg_ref,
                 qa_o, ka_o, va_o, qc_o, kc_o, vc_o):
    x = p_ref[...]
    cos = cos_ref[...]
    sa = sa_ref[...]
    sb = sb_ref[...]

    def tile(t, width):
        reps = width // 128
        return t if reps == 1 else jnp.concatenate([t] * reps, axis=1)

    def rope(y):
        w = y.shape[1]
        return (y * tile(cos, w)
                + pltpu.roll(y, w - 16, 1) * tile(sa, w)
                + pltpu.roll(y, 16, 1) * tile(sb, w))

    def inv_rms(xh):
        return lax.rsqrt(jnp.mean(xh * xh, axis=-1, keepdims=True) + RMS_EPS)

    qa = x[:, 0:512]
    ka = x[:, 512:640]
    va = x[:, 640:768]
    qc = x[:, 768:1280]
    kc = x[:, 1280:1408]
    vc = x[:, 1408:1536]

    qa_r = rope(qa * tile(qg_ref[...], 512))
    ka_r = rope(ka * kg_ref[...])
    qc_r = rope(qc)
    kc_r = rope(kc)

    for h in range(N_HEADS):
        sl = slice(h * HEAD_DIM, (h + 1) * HEAD_DIM)
        qa_o[h] = (qa_r[:, sl] * (inv_rms(qa[:, sl]) * ATTN_SCALE)).astype(BF16)
        qc_o[h] = (qc_r[:, sl] * ATTN_SCALE).astype(BF16)
    lane = lax.broadcasted_iota(jnp.int32, (x.shape[0], HEAD_DIM), 1)
    ones_col = jnp.where(lane == 0, 1.0, 0.0).astype(BF16)
    for h in range(N_KV):
        sl = slice(h * HEAD_DIM, (h + 1) * HEAD_DIM)
        ka_o[h] = (ka_r[:, sl] * inv_rms(ka[:, sl])).astype(BF16)
        va_o[h, :, 0:HEAD_DIM] = va[:, sl].astype(BF16)
        va_o[h, :, HEAD_DIM:2 * HEAD_DIM] = ones_col
        kc_o[h] = kc_r[:, sl].astype(BF16)
        vc_o[h] = vc[:, sl].astype(BF16)


def _prep_call(proj, cos_t, sa_t, sb_t, qg, kg):
    n = proj.shape[0]
    tb = ROW_BLOCK
    tab = pl.BlockSpec((tb, 128), lambda i: (i, 0))
    gain = pl.BlockSpec((1, 128), lambda i: (0, 0))

    def hm(nh, width=HEAD_DIM):
        return pl.BlockSpec((nh, tb, width), lambda i: (0, i, 0))

    def hshape(nh, width=HEAD_DIM):
        return jax.ShapeDtypeStruct((nh, n, width), BF16)

    return pl.pallas_call(
        _prep_kernel,
        grid=(n // tb,),
        in_specs=[pl.BlockSpec((tb, 1536), lambda i: (i, 0)), tab, tab, tab, gain, gain],
        out_specs=[hm(N_HEADS), hm(N_KV), hm(N_KV, 2 * HEAD_DIM), hm(N_HEADS), hm(N_KV), hm(N_KV)],
        out_shape=[hshape(N_HEADS), hshape(N_KV), hshape(N_KV, 2 * HEAD_DIM), hshape(N_HEADS), hshape(N_KV),
                   hshape(N_KV)],
        compiler_params=_cparams(("arbitrary",)),
        name="prep",
    )(proj, cos_t, sa_t, sb_t, qg, kg)


def _attn_a_kernel(q_ref, k_ref, v_ref, o_ref, s_scr, m_scr, acc_scr, *, ctx_len, ctx_blocks, lat_chunks):
    i = pl.program_id(1)
    tq = q_ref.shape[1]
    q = q_ref[...].reshape(GROUP * tq, HEAD_DIM)

    def scores(off, width):
        return _nt_dot(q, k_ref[0, pl.ds(off, width), :])

    def update(s, off, width):
        m_old = m_scr[...]
        m_new = jnp.maximum(m_old, jnp.max(s, axis=-1, keepdims=True))
        alpha = jnp.exp(m_old - m_new)
        p = jnp.exp(s - jnp.concatenate([m_new] * (width // 128), axis=1))
        acc_scr[...] = alpha * acc_scr[...] + jnp.dot(
            p.astype(BF16), v_ref[0, pl.ds(off, width), :], preferred_element_type=F32)
        m_scr[...] = m_new

    def chunk_off(c):
        return pl.multiple_of(ctx_len + c * KEY_CHUNK, 128)

    m_scr[...] = jnp.full(m_scr.shape, -jnp.inf, F32)
    acc_scr[...] = jnp.zeros(acc_scr.shape, F32)
    update(scores(0, ctx_len), 0, ctx_len)

    @pl.when(i >= ctx_blocks)
    def _():
        s_scr[0] = scores(chunk_off(0), KEY_CHUNK)

        def body(c2, carry):
            c = 2 * c2
            s_scr[1] = scores(chunk_off(c + 1), KEY_CHUNK)
            update(s_scr[0], chunk_off(c), KEY_CHUNK)
            s_scr[0] = scores(chunk_off(jnp.minimum(c + 2, lat_chunks - 1)), KEY_CHUNK)
            update(s_scr[1], chunk_off(c + 1), KEY_CHUNK)
            return carry

        lax.fori_loop(0, lat_chunks // 2, body, 0)

    acc = acc_scr[...]
    o = acc[:, 0:HEAD_DIM] / acc[:, HEAD_DIM:HEAD_DIM + 1]
    o_ref[...] = o.reshape(GROUP, tq, HEAD_DIM).astype(BF16)


def _attn_a_call(q, k, v, ctx_len):
    n = q.shape[1]
    tq = ROW_BLOCK
    lat_chunks = (n - ctx_len) // KEY_CHUNK
    assert lat_chunks % 2 == 0
    kern = functools.partial(_attn_a_kernel, ctx_len=ctx_len, ctx_blocks=ctx_len // tq, lat_chunks=lat_chunks)
    return pl.pallas_call(
        kern,
        grid=(N_KV, n // tq),
        in_specs=[
            pl.BlockSpec((GROUP, tq, HEAD_DIM), lambda j, i: (j, i, 0)),
            pl.BlockSpec((1, n, HEAD_DIM), lambda j, i: (j, 0, 0)),
            pl.BlockSpec((1, n, 2 * HEAD_DIM), lambda j, i: (j, 0, 0)),
        ],
        out_specs=pl.BlockSpec((GROUP, tq, HEAD_DIM), lambda j, i: (j, i, 0)),
        out_shape=jax.ShapeDtypeStruct((N_HEADS, n, HEAD_DIM), BF16),
        scratch_shapes=[
            pltpu.VMEM((2, GROUP * tq, KEY_CHUNK), F32),
            pltpu.VMEM((GROUP * tq, 128), F32),
            pltpu.VMEM((GROUP * tq, 2 * HEAD_DIM), F32),
        ],
        compiler_params=_cparams(("arbitrary", "arbitrary")),
        name="attn_global",
    )(q, k, v)


def _attn_c_kernel(sink_ref, q_ref, k_ref, v_ref, o_ref, *, ctx_len, ctx_blocks, s_len):
    j = pl.program_id(0)
    i = pl.program_id(1)
    tq = q_ref.shape[1]
    rows = GROUP * tq
    n_loc = tq + 2 * WINDOW
    q = q_ref[...].reshape(rows, HEAD_DIM)

    p0 = i * tq - ctx_len
    ws = pl.multiple_of(jnp.clip(p0 - WINDOW, 0, s_len - n_loc), 128)
    k_loc = k_ref[0, pl.ds(ctx_len + ws, n_loc), :]
    v_loc = v_ref[0, pl.ds(ctx_len + ws, n_loc), :]
    k_ctx = k_ref[0, 0:ctx_len, :]
    v_ctx = v_ref[0, 0:ctx_len, :]

    s_loc = _nt_dot(q, k_loc)
    t_pos = p0 + (lax.broadcasted_iota(jnp.int32, (rows, n_loc), 0) & (tq - 1))
    s_pos = ws + lax.broadcasted_iota(jnp.int32, (rows, n_loc), 1)
    valid = (jnp.abs(t_pos - s_pos) <= WINDOW) & (i >= ctx_blocks)
    s_loc = jnp.where(valid, s_loc, NEG_INF)
    s_ctx = _nt_dot(q, k_ctx)

    grp = lax.broadcasted_iota(jnp.int32, (rows, 1), 0) // tq
    sink = jnp.full((rows, 1), sink_ref[j * GROUP + GROUP - 1], F32)
    for g in range(GROUP - 1):
        sink = jnp.where(grp == g, sink_ref[j * GROUP + g], sink)

    m = jnp.maximum(jnp.maximum(jnp.max(s_loc, axis=-1, keepdims=True),
                                jnp.max(s_ctx, axis=-1, keepdims=True)), sink)
    p_loc = jnp.exp(s_loc - m)
    p_ctx = jnp.exp(s_ctx - m)
    l = (jnp.sum(p_loc, axis=-1, keepdims=True) + jnp.sum(p_ctx, axis=-1, keepdims=True)
         + jnp.exp(sink - m))
    o = (jnp.dot(p_loc.astype(BF16), v_loc, preferred_element_type=F32)
         + jnp.dot(p_ctx.astype(BF16), v_ctx, preferred_element_type=F32)) / l
    o_ref[...] = o.reshape(GROUP, tq, HEAD_DIM).astype(BF16)


def _attn_c_call(sink, q, k, v, ctx_len):
    n = q.shape[1]
    tq = ROW_BLOCK
    kern = functools.partial(_attn_c_kernel, ctx_len=ctx_len, ctx_blocks=ctx_len // tq, s_len=n - ctx_len)
    return pl.pallas_call(
        kern,
        grid=(N_KV, n // tq),
        in_specs=[
            pl.BlockSpec(memory_space=pltpu.SMEM),
            pl.BlockSpec((GROUP, tq, HEAD_DIM), lambda j, i: (j, i, 0)),
            pl.BlockSpec((1, n, HEAD_DIM), lambda j, i: (j, 0, 0)),
            pl.BlockSpec((1, n, HEAD_DIM), lambda j, i: (j, 0, 0)),
        ],
        out_specs=pl.BlockSpec((GROUP, tq, HEAD_DIM), lambda j, i: (j, i, 0)),
        out_shape=jax.ShapeDtypeStruct((N_HEADS, n, HEAD_DIM), BF16),
        compiler_params=_cparams(("arbitrary", "arbitrary")),
        name="attn_window",
    )(sink, q, k, v)


def _merge_kernel(x_ref, mod_ref, ya_ref, yc_ref, xb_ref, gb_ref, gc_ref,
                  xbp_ref, gcp_ref, xbn_ref, gcn_ref, gl_ref, cw_ref, wb_ref, wo_ref,
                  lg_ref, lb_ref, wpq_ref, x1_ref, hq_ref, qp_ref, *, n_blocks, ctx_blocks):
    i = pl.program_id(0)
    tb = x_ref.shape[0]

    z = gc_ref[...] * xb_ref[...]
    z_before = gcp_ref[7:8, :] * xbp_ref[7:8, :]
    z_after = gcn_ref[0:1, :] * xbn_ref[0:1, :]
    seq_start = (i == 0) | (i == ctx_blocks)
    seq_end = (i == ctx_blocks - 1) | (i == n_blocks - 1)
    z_before = jnp.where(seq_start, 0.0, z_before)
    z_after = jnp.where(seq_end, 0.0, z_after)
    rid = lax.broadcasted_iota(jnp.int32, z.shape, 0)
    z_prev = jnp.where(rid == 0, z_before, pltpu.roll(z, 1, 0))
    z_next = jnp.where(rid == tb - 1, z_after, pltpu.roll(z, tb - 1, 0))
    cw = cw_ref[...]
    yb = gb_ref[...] * (cw[0:1, :] * z_prev + cw[1:2, :] * z + cw[2:3, :] * z_next)

    def head_proj(y_ref, b):
        acc = jnp.dot(y_ref[0], wb_ref[b, 0:HEAD_DIM, :], preferred_element_type=F32)
        for h in range(1, N_HEADS):
            acc = acc + jnp.dot(y_ref[h], wb_ref[b, h * HEAD_DIM:(h + 1) * HEAD_DIM, :],
                                preferred_element_type=F32)
        return acc

    pa = head_proj(ya_ref, 0)
    pb = jnp.dot(yb.astype(BF16), wb_ref[1], preferred_element_type=F32)
    pc = head_proj(yc_ref, 2)
    gates = jax.nn.sigmoid(gl_ref[...])
    m = (gates[:, 0:D_MODEL] * pa + gates[:, D_MODEL:2 * D_MODEL] * pb
         + gates[:, 2 * D_MODEL:3 * D_MODEL] * pc)
    y = jnp.dot(m.astype(BF16), wo_ref[...], preferred_element_type=F32)

    g1 = mod_ref[0, 2:3, :]
    sh2 = mod_ref[0, 3:4, :]
    sc2 = mod_ref[0, 4:5, :]
    x1 = _layer_norm(DEEPNORM_ALPHA * x_ref[...] + g1 * y, lg_ref[...], lb_ref[...])
    x1_ref[...] = x1
    hq = (x1 * (1.0 + sc2) + sh2).astype(BF16)
    hq_ref[...] = hq
    qp_ref[...] = jnp.dot(hq, wpq_ref[...], preferred_element_type=F32)


def _merge_call(x_all, mods, ya, yc, proj, conv_w, w_branch, w_out, ln_g, ln_b, w_pq, ctx_blocks):
    n = x_all.shape[0]
    tb = ROW_BLOCK
    nb = n // tb
    halo = tb // 8
    last8 = n // 8 - 1
    kern = functools.partial(_merge_kernel, n_blocks=nb, ctx_blocks=ctx_blocks)
    qcols = PEER_HEADS * PEER_QDIM

    def full(shape):
        return pl.BlockSpec(shape, lambda i: (0,) * len(shape))

    def cols512(c):
        return pl.BlockSpec((tb, B_WIDTH), lambda i: (i, c))

    def before(c):
        return pl.BlockSpec((8, B_WIDTH), lambda i: (jnp.maximum(i * halo - 1, 0), c))

    def after(c):
        return pl.BlockSpec((8, B_WIDTH), lambda i: (jnp.minimum((i + 1) * halo, last8), c))

    return pl.pallas_call(
        kern,
        grid=(nb,),
        in_specs=[
            pl.BlockSpec((tb, D_MODEL), lambda i: (i, 0)),
            pl.BlockSpec((1, N_MOD, D_MODEL), lambda i: (jnp.where(i < ctx_blocks, 0, 1), 0, 0)),
            pl.BlockSpec((N_HEADS, tb, HEAD_DIM), lambda i: (0, i, 0)),
            pl.BlockSpec((N_HEADS, tb, HEAD_DIM), lambda i: (0, i, 0)),
            cols512(3), cols512(4), cols512(5),
            before(3), before(5), after(3), after(5),
            pl.BlockSpec((tb, 3 * D_MODEL), lambda i: (i, 1)),
            full((CONV_W, B_WIDTH)),
            full((3, B_WIDTH, D_MODEL)),
            full((D_MODEL, D_MODEL)),
            full((1, D_MODEL)), full((1, D_MODEL)),
            full((D_MODEL, qcols)),
        ],
        out_specs=[
            pl.BlockSpec((tb, D_MODEL), lambda i: (i, 0)),
            pl.BlockSpec((tb, D_MODEL), lambda i: (i, 0)),
            pl.BlockSpec((tb, qcols), lambda i: (i, 0)),
        ],
        out_shape=[
            jax.ShapeDtypeStruct((n, D_MODEL), F32),
            jax.ShapeDtypeStruct((n, D_MODEL), BF16),
            jax.ShapeDtypeStruct((n, qcols), F32),
        ],
        compiler_params=_cparams(("arbitrary",)),
        name="merge",
    )(x_all, mods, ya, yc, proj, proj, proj, proj, proj, proj, proj, proj,
      conv_w, w_branch, w_out, ln_g, ln_b, w_pq)


def _extract_topk(s, rounds, on_round):
    rows = s.shape[0]
    ridx = lax.broadcasted_iota(jnp.int32, s.shape, 0)
    rank = jnp.full(s.shape, rounds, jnp.int32)
    for r in range(rounds):
        m = jnp.max(s, axis=0, keepdims=True)
        first = jnp.min(jnp.where(s == m, ridx, rows), axis=0, keepdims=True)
        hit = ridx == first
        rank = jnp.where(hit, r, rank)
        s = jnp.where(hit, -jnp.inf, s)
        on_round(r, m)
    return rank


def _topk_kernel(qp_ref, sk_ref, r2_o, w2_o, k1_o, w1_o, tv_scr, cand_scr, kj_scr):
    tt = qp_ref.shape[0]
    half = PEER_QDIM // 2
    q = qp_ref[...].astype(BF16)

    scores = []
    ranks = []
    for p in range(2):
        s = _nt_dot(sk_ref[0, p], q[:, p * half:(p + 1) * half])

        def keep(r, m, p=p):
            tv_scr[p, r:r + 1, :] = m

        ranks.append(_extract_topk(s, PEER_TOPK, keep))
        scores.append(s)

    tv1 = tv_scr[0]
    tv2 = tv_scr[1]
    off = 0
    for j, cnt in enumerate(CAND_COUNTS):
        cand_scr[off:off + cnt, :] = tv1[j:j + 1, :] + tv2[0:cnt, :]
        off += cnt
    cand_scr[N_CAND:CAND_ROWS, :] = jnp.full((CAND_ROWS - N_CAND, tt), -jnp.inf, F32)
    cand = cand_scr[...]
    crank = _extract_topk(cand, PEER_TOPK, lambda r, m: None)
    chosen = crank < PEER_TOPK
    cmax = cand[0:1, :]
    z = jnp.sum(jnp.where(chosen, jnp.exp(cand - cmax), 0.0), axis=0, keepdims=True)
    cnt_f = chosen.astype(F32)
    off = 0
    for j, cnt in enumerate(CAND_COUNTS):
        kj_scr[j:j + 1, :] = jnp.sum(cnt_f[off:off + cnt, :], axis=0, keepdims=True)
        off += cnt

    rank1, rank2 = ranks
    k1 = jnp.zeros((N_KEYS, tt), F32)
    for j in range(PEER_TOPK):
        k1 = jnp.where(rank1 == j, kj_scr[j:j + 1, :], k1)
    k1_o[0] = k1
    w1_o[0] = jnp.where(rank1 < PEER_TOPK, jnp.exp(scores[0] - tv1[0:1, :]), 0.0) / z
    r2_o[0] = rank2.astype(F32)
    w2_o[0] = jnp.where(rank2 < PEER_TOPK, jnp.exp(scores[1] - tv2[0:1, :]), 0.0)


def _topk_call(qp, sub_keys):
    n = qp.shape[0]
    tt = ROW_BLOCK
    out = pl.BlockSpec((1, N_KEYS, tt), lambda t, h: (h, 0, t))
    shape = jax.ShapeDtypeStruct((PEER_HEADS, N_KEYS, n), F32)
    return pl.pallas_call(
        _topk_kernel,
        grid=(n // tt, PEER_HEADS),
        in_specs=[
            pl.BlockSpec((tt, PEER_QDIM), lambda t, h: (t, h)),
            pl.BlockSpec((1, 2, N_KEYS, PEER_QDIM // 2), lambda t, h: (h, 0, 0, 0)),
        ],
        out_specs=[out, out, out, out],
        out_shape=[shape, shape, shape, shape],
        scratch_shapes=[
            pltpu.VMEM((2, PEER_TOPK, tt), F32),
            pltpu.VMEM((CAND_ROWS, tt), F32),
            pltpu.VMEM((PEER_TOPK, tt), F32),
        ],
        compiler_params=_cparams(("arbitrary", "arbitrary")),
        name="peer_topk",
    )(qp, sub_keys)


def _erf(x):
    x = jnp.clip(x, -4.0, 4.0)
    x2 = x * x
    a = jnp.float32(-2.72614225801306e-10)
    for c in (2.77068142495902e-08, -2.10102402082508e-06, -5.69250639462346e-05,
              -7.34990630326855e-04, -2.95459980854025e-03, -1.60960333262415e-02):
        a = a * x2 + jnp.float32(c)
    b = jnp.float32(-1.45660718464996e-05)
    for c in (-2.13374055278905e-04, -1.68282697438203e-03, -7.37332916720468e-03,
              -1.42647390514189e-02):
        b = b * x2 + jnp.float32(c)
    return x * a / b


def _dense_kernel(hq_ref, u_ref, vt_ref, r2_ref, w2_ref, k1_ref, w1_ref, o_ref, h_scr, a_scr):
    e = pl.program_id(1)
    n_i1 = u_ref.shape[0] // N_KEYS
    tt = hq_ref.shape[0]
    lane_chunks = tt // 128

    @pl.when(e == 0)
    def _():
        o_ref[...] = jnp.zeros(o_ref.shape, F32)

    h_scr[...] = _nt_dot(u_ref[...], hq_ref[...])

    i1_rows = pl.ds(pl.multiple_of(e * n_i1, 8), n_i1)

    for ii in range(n_i1):

        def chunk(c, carry, ii=ii):
            lanes = pl.ds(pl.multiple_of(c * 128, 128), 128)
            g = jnp.zeros((N_KEYS, 128), F32)
            for h in range(PEER_HEADS):
                k1_row = k1_ref[h, i1_rows, lanes][ii:ii + 1, :]
                w1_row = w1_ref[h, i1_rows, lanes][ii:ii + 1, :]
                keep = r2_ref[h, :, lanes] < k1_row
                g = g + jnp.where(keep, w2_ref[h, :, lanes] * w1_row, 0.0)
            hh = h_scr[ii * N_KEYS:(ii + 1) * N_KEYS, lanes]
            act = 0.5 * hh * (1.0 + _erf(hh * (2.0 ** -0.5)))
            a_scr[ii * N_KEYS:(ii + 1) * N_KEYS, lanes] = (act * g).astype(BF16)
            return carry

        lax.fori_loop(0, lane_chunks, chunk, 0)

    o_ref[...] += jnp.dot(vt_ref[...], a_scr[...], preferred_element_type=F32)


def _dense_call(hq, u_tab, vt_tab, r2, w2, k1, w1):
    n = hq.shape[0]
    n_exp = u_tab.shape[0]
    tt = PEER_TOKENS
    eb = PEER_EXPERTS
    tab = pl.BlockSpec((PEER_HEADS, N_KEYS, tt), lambda t, e: (0, 0, t))
    return pl.pallas_call(
        _dense_kernel,
        grid=(n // tt, n_exp // eb),
        in_specs=[
            pl.BlockSpec((tt, D_MODEL), lambda t, e: (t, 0)),
            pl.BlockSpec((eb, D_MODEL), lambda t, e: (e, 0)),
            pl.BlockSpec((D_MODEL, eb), lambda t, e: (0, e)),
            tab, tab, tab, tab,
        ],
        out_specs=pl.BlockSpec((D_MODEL, tt), lambda t, e: (0, t)),
        out_shape=jax.ShapeDtypeStruct((D_MODEL, n), F32),
        scratch_shapes=[pltpu.VMEM((eb, tt), F32), pltpu.VMEM((eb, tt), BF16)],
        compiler_params=_cparams(("arbitrary", "arbitrary")),
        name="peer_dense",
    )(hq, u_tab, vt_tab, r2, w2, k1, w1)


def _ln2_kernel(x1_ref, ft_ref, mod_ref, lg_ref, lb_ref, o_ref):
    g2 = mod_ref[0, 5:6, :]
    f = ft_ref[...].T
    o_ref[...] = _layer_norm(DEEPNORM_ALPHA * x1_ref[...] + g2 * f, lg_ref[...], lb_ref[...])


def _ln2_call(x1, ft, mods, ln_g, ln_b, ctx_blocks):
    n = x1.shape[0]
    tb = ROW_BLOCK
    return pl.pallas_call(
        _ln2_kernel,
        grid=(n // tb,),
        in_specs=[
            pl.BlockSpec((tb, D_MODEL), lambda i: (i, 0)),
            pl.BlockSpec((D_MODEL, tb), lambda i: (0, i)),
            pl.BlockSpec((1, N_MOD, D_MODEL), lambda i: (jnp.where(i < ctx_blocks, 0, 1), 0, 0)),
            pl.BlockSpec((1, D_MODEL), lambda i: (0, 0)),
            pl.BlockSpec((1, D_MODEL), lambda i: (0, 0)),
        ],
        out_specs=pl.BlockSpec((tb, D_MODEL), lambda i: (i, 0)),
        out_shape=jax.ShapeDtypeStruct((n, D_MODEL), F32),
        compiler_params=_cparams(("arbitrary",)),
        name="ln2",
    )(x1, ft, mods, ln_g, ln_b)


def _rope_tables(ctx_len, s_len):
    t = jnp.arange(s_len, dtype=jnp.int32)
    pos = jnp.stack([t // GRID_W, t % GRID_W], axis=-1).astype(F32)
    quarter = HEAD_DIM // 4
    inv_freq = ROPE_THETA ** (-jnp.arange(quarter, dtype=F32) / quarter)
    ang = pos[:, :, None] * inv_freq
    cos = jnp.repeat(jnp.cos(ang), 2, axis=1).reshape(s_len, HEAD_DIM)
    sin = jnp.repeat(jnp.sin(ang), 2, axis=1).reshape(s_len, HEAD_DIM)
    first_half = (jnp.arange(HEAD_DIM) % 32) < 16
    sa = jnp.where(first_half, -sin, 0.0)
    sb = jnp.where(first_half, 0.0, sin)

    def full(tab, fill):
        tab = jnp.concatenate([jnp.full((ctx_len, HEAD_DIM), fill, F32), tab], axis=0)
        return jnp.tile(tab, (1, 2))

    return full(cos, 1.0), full(sa, 0.0), full(sb, 0.0)


def kernel(x, c, ctx, c_ctx, w_mod, b_mod, w_in, q_norm, k_norm, conv_w, sink, w_branch, w_out,
           ln1_g, ln1_b, w_pq, sub_keys, u_tab, v_tab, ln2_g, ln2_b):
    bsz, s_len, d = x.shape
    ctx_len = ctx.shape[1]
    depth = w_mod.shape[0]
    assert bsz == 1 and d == D_MODEL and depth == DEPTH
    assert ctx_len % ROW_BLOCK == 0 and s_len % KEY_CHUNK == 0
    assert (ctx_len + s_len) % PEER_TOKENS == 0
    ctx_blocks = ctx_len // ROW_BLOCK

    x_all = jnp.concatenate([ctx[0], x[0]], axis=0)
    cos_t, sa_t, sb_t = _rope_tables(ctx_len, s_len)

    cc = jnp.zeros((8, D_MODEL), F32).at[0].set(c_ctx).at[1].set(c[0])
    mod_all = _mod_call(cc, w_mod.astype(BF16), b_mod[:, None, :])

    for layer in range(depth):
        mods = mod_all[layer, 0:2].reshape(2, N_MOD, D_MODEL)
        proj = _inproj_call(x_all, mods, w_in[layer].astype(BF16), ctx_blocks)
        qg = jnp.tile(q_norm[layer], 2)[None, :]
        kg = jnp.tile(k_norm[layer], 2)[None, :]
        qa, ka, va, qc, kc, vc = _prep_call(proj, cos_t, sa_t, sb_t, qg, kg)
        ya = _attn_a_call(qa, ka, va, ctx_len)
        yc = _attn_c_call(sink[layer], qc, kc, vc, ctx_len)
        x1, hq, qp = _merge_call(x_all, mods, ya, yc, proj, conv_w[layer],
                                 w_branch[layer].astype(BF16), w_out[layer].astype(BF16),
                                 ln1_g[layer][None, :], ln1_b[layer][None, :],
                                 w_pq[layer].astype(BF16), ctx_blocks)
        r2, w2, k1, w1 = _topk_call(qp, sub_keys[layer].astype(BF16))
        ft = _dense_call(hq, u_tab[layer].astype(BF16), v_tab[layer].T.astype(BF16), r2, w2, k1, w1)
        x_all = _ln2_call(x1, ft, mods, ln2_g[layer][None, :], ln2_b[layer][None, :], ctx_blocks)

    return x_all[ctx_len:][None]
```

```python
import functools
import math

import jax
import jax.numpy as jnp
from jax import lax
from jax.experimental import pallas as pl
from jax.experimental.pallas import tpu as pltpu

F32 = jnp.float32
BF16 = jnp.bfloat16

D_MODEL = 1024
HEAD_DIM = 64
N_HEADS = 8
N_KV = 2
GROUP = N_HEADS // N_KV
GRID_W = 64
WINDOW = 128
CONV_W = 3
B_WIDTH = 512
N_MOD = 6
ROPE_THETA = 10000.0
PEER_HEADS = 8
N_KEYS = 128
PEER_TOPK = 16
PEER_QDIM = 256
DEPTH = 2
IN_COLS = 6144
DEEPNORM_ALPHA = (2 * DEPTH) ** 0.25
LN_EPS = 1e-6
RMS_EPS = 1e-6
NEG_INF = -1e30
ATTN_SCALE = HEAD_DIM ** -0.5

ROW_BLOCK = 256
KEY_CHUNK = 512
PEER_TOKENS = 1280
PEER_LANES = 256
PEER_EXPERTS = 1024
VMEM_LIMIT = 56 * 1024 * 1024

CAND_COUNTS = tuple(PEER_TOPK // (j + 1) for j in range(PEER_TOPK))
N_CAND = sum(CAND_COUNTS)
CAND_ROWS = 56


def _cparams(sem):
    return pltpu.CompilerParams(dimension_semantics=sem, vmem_limit_bytes=VMEM_LIMIT)


def _nt_dot(a, b):
    return lax.dot_general(a, b, (((1,), (1,)), ((), ())), preferred_element_type=F32)


def _layer_norm(r, g, b):
    mu = jnp.mean(r, axis=-1, keepdims=True)
    rc = r - mu
    var = jnp.mean(rc * rc, axis=-1, keepdims=True)
    return rc * lax.rsqrt(var + LN_EPS) * g + b


def _mod_kernel(c_ref, w_ref, b_ref, o_ref):
    c = c_ref[...]
    h = (c * jax.nn.sigmoid(c)).astype(BF16)
    o_ref[0] = jnp.dot(h, w_ref[0], preferred_element_type=F32) + b_ref[0]


def _mod_call(cc, w_mod, b_mod):
    depth = w_mod.shape[0]
    cols = w_mod.shape[2]
    tn = 1024
    return pl.pallas_call(
        _mod_kernel,
        grid=(depth, cols // tn),
        in_specs=[
            pl.BlockSpec((8, D_MODEL), lambda l, j: (0, 0)),
            pl.BlockSpec((1, D_MODEL, tn), lambda l, j: (l, 0, j)),
            pl.BlockSpec((1, 1, tn), lambda l, j: (l, 0, j)),
        ],
        out_specs=pl.BlockSpec((1, 8, tn), lambda l, j: (l, 0, j)),
        out_shape=jax.ShapeDtypeStruct((depth, 8, cols), F32),
        compiler_params=_cparams(("arbitrary", "arbitrary")),
        name="mod",
    )(cc, w_mod, b_mod)


def _inproj_kernel(x_ref, mod_ref, w_ref, o_ref):
    sh = mod_ref[0, 0:1, :]
    sc = mod_ref[0, 1:2, :]
    hm = (x_ref[...] * (1.0 + sc) + sh).astype(BF16)
    o_ref[...] = jnp.dot(hm, w_ref[...], preferred_element_type=F32)


def _inproj_call(x_all, mods, w_in, ctx_blocks):
    n = x_all.shape[0]
    tn = 1024
    tb = ROW_BLOCK
    return pl.pallas_call(
        _inproj_kernel,
        grid=(IN_COLS // tn, n // tb),
        in_specs=[
            pl.BlockSpec((tb, D_MODEL), lambda j, i: (i, 0)),
            pl.BlockSpec((1, N_MOD, D_MODEL), lambda j, i: (jnp.where(i < ctx_blocks, 0, 1), 0, 0)),
            pl.BlockSpec((D_MODEL, tn), lambda j, i: (0, j)),
        ],
        out_specs=pl.BlockSpec((tb, tn), lambda j, i: (i, j)),
        out_shape=jax.ShapeDtypeStruct((n, IN_COLS), F32),
        compiler_params=_cparams(("arbitrary", "arbitrary")),
        name="inproj",
    )(x_all, mods, w_in)


def _prep_kernel(p_ref, cos_ref, sa_ref, sb_ref, qg_ref, kg_ref,
                 qa_o, ka_o, va_o, qc_o, kc_o, vc_o):
    x = p_ref[...]
    cos = cos_ref[...]
    sa = sa_ref[...]
    sb = sb_ref[...]

    def tile(t, width):
        reps = width // 128
        return t if reps == 1 else jnp.concatenate([t] * reps, axis=1)

    def rope(y):
        w = y.shape[1]
        return (y * tile(cos, w)
                + pltpu.roll(y, w - 16, 1) * tile(sa, w)
                + pltpu.roll(y, 16, 1) * tile(sb, w))

    def inv_rms(xh):
        return lax.rsqrt(jnp.mean(xh * xh, axis=-1, keepdims=True) + RMS_EPS)

    qa = x[:, 0:512]
    ka = x[:, 512:640]
    va = x[:, 640:768]
    qc = x[:, 768:1280]
    kc = x[:, 1280:1408]
    vc = x[:, 1408:1536]

    qa_r = rope(qa * tile(qg_ref[...], 512))
    ka_r = rope(ka * kg_ref[...])
    qc_r = rope(qc)
    kc_r = rope(kc)

    for h in range(N_HEADS):
        sl = slice(h * HEAD_DIM, (h + 1) * HEAD_DIM)
        qa_o[h] = (qa_r[:, sl] * (inv_rms(qa[:, sl]) * ATTN_SCALE)).astype(BF16)
        qc_o[h] = (qc_r[:, sl] * ATTN_SCALE).astype(BF16)
    lane = lax.broadcasted_iota(jnp.int32, (x.shape[0], HEAD_DIM), 1)
    ones_col = jnp.where(lane == 0, 1.0, 0.0).astype(BF16)
    for h in range(N_KV):
        sl = slice(h * HEAD_DIM, (h + 1) * HEAD_DIM)
        ka_o[h] = (ka_r[:, sl] * inv_rms(ka[:, sl])).astype(BF16)
        va_o[h, :, 0:HEAD_DIM] = va[:, sl].astype(BF16)
        va_o[h, :, HEAD_DIM:2 * HEAD_DIM] = ones_col
        kc_o[h] = kc_r[:, sl].astype(BF16)
        vc_o[h] = vc[:, sl].astype(BF16)


def _prep_call(proj, cos_t, sa_t, sb_t, qg, kg):
    n = proj.shape[0]
    tb = ROW_BLOCK
    tab = pl.BlockSpec((tb, 128), lambda i: (i, 0))
    gain = pl.BlockSpec((1, 128), lambda i: (0, 0))

    def hm(nh, width=HEAD_DIM):
        return pl.BlockSpec((nh, tb, width), lambda i: (0, i, 0))

    def hshape(nh, width=HEAD_DIM):
        return jax.ShapeDtypeStruct((nh, n, width), BF16)

    return pl.pallas_call(
        _prep_kernel,
        grid=(n // tb,),
        in_specs=[pl.BlockSpec((tb, 1536), lambda i: (i, 0)), tab, tab, tab, gain, gain],
        out_specs=[hm(N_HEADS), hm(N_KV), hm(N_KV, 2 * HEAD_DIM), hm(N_HEADS), hm(N_KV), hm(N_KV)],
        out_shape=[hshape(N_HEADS), hshape(N_KV), hshape(N_KV, 2 * HEAD_DIM), hshape(N_HEADS), hshape(N_KV),
                   hshape(N_KV)],
        compiler_params=_cparams(("arbitrary",)),
        name="prep",
    )(proj, cos_t, sa_t, sb_t, qg, kg)


def _attn_a_kernel(q_ref, k_ref, v_ref, o_ref, s_scr, m_scr, acc_scr, *, ctx_len, ctx_blocks, lat_chunks):
    i = pl.program_id(1)
    tq = q_ref.shape[1]
    q = q_ref[...].reshape(GROUP * tq, HEAD_DIM)

    def scores(off, width):
        return _nt_dot(q, k_ref[0, pl.ds(off, width), :])

    def update(s, off, width):
        m_old = m_scr[...]
        m_new = jnp.maximum(m_old, jnp.max(s, axis=-1, keepdims=True))
        alpha = jnp.exp(m_old - m_new)
        p = jnp.exp(s - jnp.concatenate([m_new] * (width // 128), axis=1))
        acc_scr[...] = alpha * acc_scr[...] + jnp.dot(
            p.astype(BF16), v_ref[0, pl.ds(off, width), :], preferred_element_type=F32)
        m_scr[...] = m_new

    def chunk_off(c):
        return pl.multiple_of(ctx_len + c * KEY_CHUNK, 128)

    m_scr[...] = jnp.full(m_scr.shape, -jnp.inf, F32)
    acc_scr[...] = jnp.zeros(acc_scr.shape, F32)
    update(scores(0, ctx_len), 0, ctx_len)

    @pl.when(i >= ctx_blocks)
    def _():
        s_scr[0] = scores(chunk_off(0), KEY_CHUNK)

        def body(c2, carry):
            c = 2 * c2
            s_scr[1] = scores(chunk_off(c + 1), KEY_CHUNK)
            update(s_scr[0], chunk_off(c), KEY_CHUNK)
            s_scr[0] = scores(chunk_off(jnp.minimum(c + 2, lat_chunks - 1)), KEY_CHUNK)
            update(s_scr[1], chunk_off(c + 1), KEY_CHUNK)
            return carry

        lax.fori_loop(0, lat_chunks // 2, body, 0)

    acc = acc_scr[...]
    o = acc[:, 0:HEAD_DIM] / acc[:, HEAD_DIM:HEAD_DIM + 1]
    o_ref[...] = o.reshape(GROUP, tq, HEAD_DIM).astype(BF16)


def _attn_a_call(q, k, v, ctx_len):
    n = q.shape[1]
    tq = ROW_BLOCK
    lat_chunks = (n - ctx_len) // KEY_CHUNK
    assert lat_chunks % 2 == 0
    kern = functools.partial(_attn_a_kernel, ctx_len=ctx_len, ctx_blocks=ctx_len // tq, lat_chunks=lat_chunks)
    return pl.pallas_call(
        kern,
        grid=(N_KV, n // tq),
        in_specs=[
            pl.BlockSpec((GROUP, tq, HEAD_DIM), lambda j, i: (j, i, 0)),
            pl.BlockSpec((1, n, HEAD_DIM), lambda j, i: (j, 0, 0)),
            pl.BlockSpec((1, n, 2 * HEAD_DIM), lambda j, i: (j, 0, 0)),
        ],
        out_specs=pl.BlockSpec((GROUP, tq, HEAD_DIM), lambda j, i: (j, i, 0)),
        out_shape=jax.ShapeDtypeStruct((N_HEADS, n, HEAD_DIM), BF16),
        scratch_shapes=[
            pltpu.VMEM((2, GROUP * tq, KEY_CHUNK), F32),
            pltpu.VMEM((GROUP * tq, 128), F32),
            pltpu.VMEM((GROUP * tq, 2 * HEAD_DIM), F32),
        ],
        compiler_params=_cparams(("arbitrary", "arbitrary")),
        name="attn_global",
    )(q, k, v)


def _attn_c_kernel(sink_ref, q_ref, k_ref, v_ref, o_ref, *, ctx_len, ctx_blocks, s_len):
    j = pl.program_id(0)
    i = pl.program_id(1)
    tq = q_ref.shape[1]
    rows = GROUP * tq
    n_loc = tq + 2 * WINDOW
    q = q_ref[...].reshape(rows, HEAD_DIM)

    p0 = i * tq - ctx_len
    ws = pl.multiple_of(jnp.clip(p0 - WINDOW, 0, s_len - n_loc), 128)
    k_loc = k_ref[0, pl.ds(ctx_len + ws, n_loc), :]
    v_loc = v_ref[0, pl.ds(ctx_len + ws, n_loc), :]
    k_ctx = k_ref[0, 0:ctx_len, :]
    v_ctx = v_ref[0, 0:ctx_len, :]

    s_loc = _nt_dot(q, k_loc)
    t_pos = p0 + (lax.broadcasted_iota(jnp.int32, (rows, n_loc), 0) & (tq - 1))
    s_pos = ws + lax.broadcasted_iota(jnp.int32, (rows, n_loc), 1)
    valid = (jnp.abs(t_pos - s_pos) <= WINDOW) & (i >= ctx_blocks)
    s_loc = jnp.where(valid, s_loc, NEG_INF)
    s_ctx = _nt_dot(q, k_ctx)

    grp = lax.broadcasted_iota(jnp.int32, (rows, 1), 0) // tq
    sink = jnp.full((rows, 1), sink_ref[j * GROUP + GROUP - 1], F32)
    for g in range(GROUP - 1):
        sink = jnp.where(grp == g, sink_ref[j * GROUP + g], sink)

    m = jnp.maximum(jnp.maximum(jnp.max(s_loc, axis=-1, keepdims=True),
                                jnp.max(s_ctx, axis=-1, keepdims=True)), sink)
    p_loc = jnp.exp(s_loc - m)
    p_ctx = jnp.exp(s_ctx - m)
    l = (jnp.sum(p_loc, axis=-1, keepdims=True) + jnp.sum(p_ctx, axis=-1, keepdims=True)
         + jnp.exp(sink - m))
    o = (jnp.dot(p_loc.astype(BF16), v_loc, preferred_element_type=F32)
         + jnp.dot(p_ctx.astype(BF16), v_ctx, preferred_element_type=F32)) / l
    o_ref[...] = o.reshape(GROUP, tq, HEAD_DIM).astype(BF16)


def _attn_c_call(sink, q, k, v, ctx_len):
    n = q.shape[1]
    tq = ROW_BLOCK
    kern = functools.partial(_attn_c_kernel, ctx_len=ctx_len, ctx_blocks=ctx_len // tq, s_len=n - ctx_len)
    return pl.pallas_call(
        kern,
        grid=(N_KV, n // tq),
        in_specs=[
            pl.BlockSpec(memory_space=pltpu.SMEM),
            pl.BlockSpec((GROUP, tq, HEAD_DIM), lambda j, i: (j, i, 0)),
            pl.BlockSpec((1, n, HEAD_DIM), lambda j, i: (j, 0, 0)),
            pl.BlockSpec((1, n, HEAD_DIM), lambda j, i: (j, 0, 0)),
        ],
        out_specs=pl.BlockSpec((GROUP, tq, HEAD_DIM), lambda j, i: (j, i, 0)),
        out_shape=jax.ShapeDtypeStruct((N_HEADS, n, HEAD_DIM), BF16),
        compiler_params=_cparams(("arbitrary", "arbitrary")),
        name="attn_window",
    )(sink, q, k, v)


def _merge_kernel(x_ref, mod_ref, ya_ref, yc_ref, xb_ref, gb_ref, gc_ref,
                  xbp_ref, gcp_ref, xbn_ref, gcn_ref, gl_ref, cw_ref, wb_ref, wo_ref,
                  lg_ref, lb_ref, wpq_ref, x1_ref, hq_ref, qp_ref, *, n_blocks, ctx_blocks):
    i = pl.program_id(0)
    tb = x_ref.shape[0]

    z = gc_ref[...] * xb_ref[...]
    z_before = gcp_ref[7:8, :] * xbp_ref[7:8, :]
    z_after = gcn_ref[0:1, :] * xbn_ref[0:1, :]
    seq_start = (i == 0) | (i == ctx_blocks)
    seq_end = (i == ctx_blocks - 1) | (i == n_blocks - 1)
    z_before = jnp.where(seq_start, 0.0, z_before)
    z_after = jnp.where(seq_end, 0.0, z_after)
    rid = lax.broadcasted_iota(jnp.int32, z.shape, 0)
    z_prev = jnp.where(rid == 0, z_before, pltpu.roll(z, 1, 0))
    z_next = jnp.where(rid == tb - 1, z_after, pltpu.roll(z, tb - 1, 0))
    cw = cw_ref[...]
    yb = gb_ref[...] * (cw[0:1, :] * z_prev + cw[1:2, :] * z + cw[2:3, :] * z_next)

    def head_proj(y_ref, b):
        acc = jnp.dot(y_ref[0], wb_ref[b, 0:HEAD_DIM, :], preferred_element_type=F32)
        for h in range(1, N_HEADS):
            acc = acc + jnp.dot(y_ref[h], wb_ref[b, h * HEAD_DIM:(h + 1) * HEAD_DIM, :],
                                preferred_element_type=F32)
        return acc

    pa = head_proj(ya_ref, 0)
    pb = jnp.dot(yb.astype(BF16), wb_ref[1], preferred_element_type=F32)
    pc = head_proj(yc_ref, 2)
    gates = jax.nn.sigmoid(gl_ref[...])
    m = (gates[:, 0:D_MODEL] * pa + gates[:, D_MODEL:2 * D_MODEL] * pb
         + gates[:, 2 * D_MODEL:3 * D_MODEL] * pc)
    y = jnp.dot(m.astype(BF16), wo_ref[...], preferred_element_type=F32)

    g1 = mod_ref[0, 2:3, :]
    sh2 = mod_ref[0, 3:4, :]
    sc2 = mod_ref[0, 4:5, :]
    x1 = _layer_norm(DEEPNORM_ALPHA * x_ref[...] + g1 * y, lg_ref[...], lb_ref[...])
    x1_ref[...] = x1
    hq = (x1 * (1.0 + sc2) + sh2).astype(BF16)
    hq_ref[...] = hq
    qp_ref[...] = jnp.dot(hq, wpq_ref[...], preferred_element_type=F32)


def _merge_call(x_all, mods, ya, yc, proj, conv_w, w_branch, w_out, ln_g, ln_b, w_pq, ctx_blocks):
    n = x_all.shape[0]
    tb = ROW_BLOCK
    nb = n // tb
    halo = tb // 8
    last8 = n // 8 - 1
    kern = functools.partial(_merge_kernel, n_blocks=nb, ctx_blocks=ctx_blocks)
    qcols = PEER_HEADS * PEER_QDIM

    def full(shape):
        return pl.BlockSpec(shape, lambda i: (0,) * len(shape))

    def cols512(c):
        return pl.BlockSpec((tb, B_WIDTH), lambda i: (i, c))

    def before(c):
        return pl.BlockSpec((8, B_WIDTH), lambda i: (jnp.maximum(i * halo - 1, 0), c))

    def after(c):
        return pl.BlockSpec((8, B_WIDTH), lambda i: (jnp.minimum((i + 1) * halo, last8), c))

    return pl.pallas_call(
        kern,
        grid=(nb,),
        in_specs=[
            pl.BlockSpec((tb, D_MODEL), lambda i: (i, 0)),
            pl.BlockSpec((1, N_MOD, D_MODEL), lambda i: (jnp.where(i < ctx_blocks, 0, 1), 0, 0)),
            pl.BlockSpec((N_HEADS, tb, HEAD_DIM), lambda i: (0, i, 0)),
            pl.BlockSpec((N_HEADS, tb, HEAD_DIM), lambda i: (0, i, 0)),
            cols512(3), cols512(4), cols512(5),
            before(3), before(5), after(3), after(5),
            pl.BlockSpec((tb, 3 * D_MODEL), lambda i: (i, 1)),
            full((CONV_W, B_WIDTH)),
            full((3, B_WIDTH, D_MODEL)),
            full((D_MODEL, D_MODEL)),
            full((1, D_MODEL)), full((1, D_MODEL)),
            full((D_MODEL, qcols)),
        ],
        out_specs=[
            pl.BlockSpec((tb, D_MODEL), lambda i: (i, 0)),
            pl.BlockSpec((tb, D_MODEL), lambda i: (i, 0)),
            pl.BlockSpec((tb, qcols), lambda i: (i, 0)),
        ],
        out_shape=[
            jax.ShapeDtypeStruct((n, D_MODEL), F32),
            jax.ShapeDtypeStruct((n, D_MODEL), BF16),
            jax.ShapeDtypeStruct((n, qcols), F32),
        ],
        compiler_params=_cparams(("arbitrary",)),
        name="merge",
    )(x_all, mods, ya, yc, proj, proj, proj, proj, proj, proj, proj, proj,
      conv_w, w_branch, w_out, ln_g, ln_b, w_pq)


def _extract_topk(s, rounds, on_round):
    rows = s.shape[0]
    ridx = lax.broadcasted_iota(jnp.int32, s.shape, 0)
    rank = jnp.full(s.shape, rounds, jnp.int32)
    for r in range(rounds):
        m = jnp.max(s, axis=0, keepdims=True)
        first = jnp.min(jnp.where(s == m, ridx, rows), axis=0, keepdims=True)
        hit = ridx == first
        rank = jnp.where(hit, r, rank)
        s = jnp.where(hit, -jnp.inf, s)
        on_round(r, m)
    return rank


def _topk_kernel(qp_ref, sk_ref, r2_o, w2_o, k1_o, w1_o, tv_scr, cand_scr, kj_scr):
    tt = qp_ref.shape[0]
    half = PEER_QDIM // 2
    q = qp_ref[...].astype(BF16)

    scores = []
    ranks = []
    for p in range(2):
        s = _nt_dot(sk_ref[0, p], q[:, p * half:(p + 1) * half])

        def keep(r, m, p=p):
            tv_scr[p, r:r + 1, :] = m

        ranks.append(_extract_topk(s, PEER_TOPK, keep))
        scores.append(s)

    tv1 = tv_scr[0]
    tv2 = tv_scr[1]
    off = 0
    for j, cnt in enumerate(CAND_COUNTS):
        cand_scr[off:off + cnt, :] = tv1[j:j + 1, :] + tv2[0:cnt, :]
        off += cnt
    cand_scr[N_CAND:CAND_ROWS, :] = jnp.full((CAND_ROWS - N_CAND, tt), -jnp.inf, F32)
    cand = cand_scr[...]
    crank = _extract_topk(cand, PEER_TOPK, lambda r, m: None)
    chosen = crank < PEER_TOPK
    cmax = cand[0:1, :]
    z = jnp.sum(jnp.where(chosen, jnp.exp(cand - cmax), 0.0), axis=0, keepdims=True)
    cnt_f = chosen.astype(F32)
    off = 0
    for j, cnt in enumerate(CAND_COUNTS):
        kj_scr[j:j + 1, :] = jnp.sum(cnt_f[off:off + cnt, :], axis=0, keepdims=True)
        off += cnt

    rank1, rank2 = ranks
    k1 = jnp.zeros((N_KEYS, tt), F32)
    for j in range(PEER_TOPK):
        k1 = jnp.where(rank1 == j, kj_scr[j:j + 1, :], k1)
    def twice_bf16(x):
        hi = lax.bitcast_convert_type(x.astype(BF16).astype(F32), jnp.uint32)
        return lax.bitcast_convert_type(hi | (hi >> 16), F32)

    k1_o[0] = twice_bf16(k1)
    w1_o[0] = twice_bf16(jnp.where(rank1 < PEER_TOPK, jnp.exp(scores[0] - tv1[0:1, :]), 0.0) / z)
    r2_o[0] = rank2.astype(F32).astype(BF16)
    w2_o[0] = jnp.where(rank2 < PEER_TOPK, jnp.exp(scores[1] - tv2[0:1, :]), 0.0).astype(BF16)


def _topk_call(qp, sub_keys):
    n = qp.shape[0]
    tt = ROW_BLOCK
    out = pl.BlockSpec((1, N_KEYS, tt), lambda t, h: (h, 0, t))
    shape = jax.ShapeDtypeStruct((PEER_HEADS, N_KEYS, n), F32)
    shape_bf16 = jax.ShapeDtypeStruct((PEER_HEADS, N_KEYS, n), BF16)
    return pl.pallas_call(
        _topk_kernel,
        grid=(n // tt, PEER_HEADS),
        in_specs=[
            pl.BlockSpec((tt, PEER_QDIM), lambda t, h: (t, h)),
            pl.BlockSpec((1, 2, N_KEYS, PEER_QDIM // 2), lambda t, h: (h, 0, 0, 0)),
        ],
        out_specs=[out, out, out, out],
        out_shape=[shape_bf16, shape_bf16, shape, shape],
        scratch_shapes=[
            pltpu.VMEM((2, PEER_TOPK, tt), F32),
            pltpu.VMEM((CAND_ROWS, tt), F32),
            pltpu.VMEM((PEER_TOPK, tt), F32),
        ],
        compiler_params=_cparams(("arbitrary", "arbitrary")),
        name="peer_topk",
    )(qp, sub_keys)


_ERF_A = (-2.72614225801306e-10, 2.77068142495902e-08, -2.10102402082508e-06, -5.69250639462346e-05,
          -7.34990630326855e-04, -2.95459980854025e-03, -1.60960333262415e-02)
_ERF_B = (-1.45660718464996e-05, -2.13374055278905e-04, -1.68282697438203e-03, -7.37332916720468e-03,
          -1.42647390514189e-02)


def _gelu(x):
    dt = x.dtype
    lim = 4.0 * math.sqrt(2.0)
    xc = jnp.clip(x, -lim, lim)
    t = xc * xc
    deg_a = len(_ERF_A) - 1
    deg_b = len(_ERF_B) - 1
    a2 = [c * (0.5 / math.sqrt(2.0)) / 2.0 ** (deg_a - i) for i, c in enumerate(_ERF_A)]
    b2 = [c / 2.0 ** (deg_b - i) for i, c in enumerate(_ERF_B)]
    a = jnp.asarray(a2[0], dt)
    for c in a2[1:]:
        a = a * t + jnp.asarray(c, dt)
    b = jnp.asarray(b2[0], dt)
    for c in b2[1:]:
        b = b * t + jnp.asarray(c, dt)
    return 0.5 * x + (x * xc) * (a / b)


def _dense_kernel(hq_ref, u_ref, vt_ref, r2_ref, w2_ref, k1_ref, w1_ref, o_ref, h_scr, a_scr, *, n_eb):
    s = pl.program_id(0)
    eb = u_ref.shape[0]
    tt = hq_ref.shape[0]
    n_i1 = eb // N_KEYS

    @pl.when(s == 0)
    def _():
        h_scr[...] = jnp.zeros(h_scr.shape, BF16)
        a_scr[...] = jnp.zeros(a_scr.shape, BF16)

    @pl.when(jnp.maximum(s - 2, 0) % n_eb == 0)
    def _():
        o_ref[...] = jnp.zeros(o_ref.shape, F32)

    def rows16(ref, h, ii, cs):
        row = ref[h, ii:ii + 1, cs]
        return pltpu.bitcast(jnp.broadcast_to(row, (8, PEER_LANES)), BF16)

    for c in range(tt // PEER_LANES):
        cs = slice(c * PEER_LANES, (c + 1) * PEER_LANES)

        o_ref[:, cs] += jnp.dot(vt_ref[...], a_scr[:, cs], preferred_element_type=F32)

        for ii in range(n_i1):
            rows = slice(ii * N_KEYS, (ii + 1) * N_KEYS)
            g = jnp.zeros((N_KEYS, PEER_LANES), BF16)
            for h in range(PEER_HEADS):
                k1b = jnp.concatenate([rows16(k1_ref, h, ii, cs)] * (N_KEYS // 16), axis=0)
                w1b = jnp.concatenate([rows16(w1_ref, h, ii, cs)] * (N_KEYS // 16), axis=0)
                keep = r2_ref[h, :, cs] < k1b
                g = g + jnp.where(keep, w2_ref[h, :, cs] * w1b, 0.0)
            a_scr[rows, cs] = _gelu(h_scr[rows, cs]) * g

        h_scr[:, cs] = _nt_dot(u_ref[...], hq_ref[cs, :]).astype(BF16)


def _dense_call(hq, u_tab, vt_tab, r2, w2, k1, w1):
    n = hq.shape[0]
    n_exp = u_tab.shape[0]
    tt = PEER_TOKENS
    eb = PEER_EXPERTS
    n_eb = n_exp // eb
    steps = (n // tt) * n_eb

    def blk(s, lag):
        return jnp.clip(s - lag, 0, steps - 1)

    tab = pl.BlockSpec((PEER_HEADS, N_KEYS, tt), lambda s: (0, 0, blk(s, 1) // n_eb))
    row = pl.BlockSpec((PEER_HEADS, eb // N_KEYS, tt), lambda s: (0, blk(s, 1) % n_eb, blk(s, 1) // n_eb))
    return pl.pallas_call(
        functools.partial(_dense_kernel, n_eb=n_eb),
        grid=(steps + 2,),
        in_specs=[
            pl.BlockSpec((tt, D_MODEL), lambda s: (blk(s, 0) // n_eb, 0)),
            pl.BlockSpec((eb, D_MODEL), lambda s: (blk(s, 0) % n_eb, 0)),
            pl.BlockSpec((D_MODEL, eb), lambda s: (0, blk(s, 2) % n_eb)),
            tab, tab, row, row,
        ],
        out_specs=pl.BlockSpec((D_MODEL, tt), lambda s: (0, blk(s, 2) // n_eb)),
        out_shape=jax.ShapeDtypeStruct((D_MODEL, n), F32),
        scratch_shapes=[pltpu.VMEM((eb, tt), BF16), pltpu.VMEM((eb, tt), BF16)],
        compiler_params=_cparams(("arbitrary",)),
        name="peer_dense",
    )(hq, u_tab, vt_tab, r2, w2, k1, w1)


def _ln2_kernel(x1_ref, ft_ref, mod_ref, lg_ref, lb_ref, o_ref):
    g2 = mod_ref[0, 5:6, :]
    f = ft_ref[...].T
    o_ref[...] = _layer_norm(DEEPNORM_ALPHA * x1_ref[...] + g2 * f, lg_ref[...], lb_ref[...])


def _ln2_call(x1, ft, mods, ln_g, ln_b, ctx_blocks):
    n = x1.shape[0]
    tb = ROW_BLOCK
    return pl.pallas_call(
        _ln2_kernel,
        grid=(n // tb,),
        in_specs=[
            pl.BlockSpec((tb, D_MODEL), lambda i: (i, 0)),
            pl.BlockSpec((D_MODEL, tb), lambda i: (0, i)),
            pl.BlockSpec((1, N_MOD, D_MODEL), lambda i: (jnp.where(i < ctx_blocks, 0, 1), 0, 0)),
            pl.BlockSpec((1, D_MODEL), lambda i: (0, 0)),
            pl.BlockSpec((1, D_MODEL), lambda i: (0, 0)),
        ],
        out_specs=pl.BlockSpec((tb, D_MODEL), lambda i: (i, 0)),
        out_shape=jax.ShapeDtypeStruct((n, D_MODEL), F32),
        compiler_params=_cparams(("arbitrary",)),
        name="ln2",
    )(x1, ft, mods, ln_g, ln_b)


def _rope_tables(ctx_len, s_len):
    t = jnp.arange(s_len, dtype=jnp.int32)
    pos = jnp.stack([t // GRID_W, t % GRID_W], axis=-1).astype(F32)
    quarter = HEAD_DIM // 4
    inv_freq = ROPE_THETA ** (-jnp.arange(quarter, dtype=F32) / quarter)
    ang = pos[:, :, None] * inv_freq
    cos = jnp.repeat(jnp.cos(ang), 2, axis=1).reshape(s_len, HEAD_DIM)
    sin = jnp.repeat(jnp.sin(ang), 2, axis=1).reshape(s_len, HEAD_DIM)
    first_half = (jnp.arange(HEAD_DIM) % 32) < 16
    sa = jnp.where(first_half, -sin, 0.0)
    sb = jnp.where(first_half, 0.0, sin)

    def full(tab, fill):
        tab = jnp.concatenate([jnp.full((ctx_len, HEAD_DIM), fill, F32), tab], axis=0)
        return jnp.tile(tab, (1, 2))

    return full(cos, 1.0), full(sa, 0.0), full(sb, 0.0)


def kernel(x, c, ctx, c_ctx, w_mod, b_mod, w_in, q_norm, k_norm, conv_w, sink, w_branch, w_out,
           ln1_g, ln1_b, w_pq, sub_keys, u_tab, v_tab, ln2_g, ln2_b):
    bsz, s_len, d = x.shape
    ctx_len = ctx.shape[1]
    depth = w_mod.shape[0]
    assert bsz == 1 and d == D_MODEL and depth == DEPTH
    assert ctx_len % ROW_BLOCK == 0 and s_len % KEY_CHUNK == 0
    assert (ctx_len + s_len) % PEER_TOKENS == 0
    ctx_blocks = ctx_len // ROW_BLOCK

    x_all = jnp.concatenate([ctx[0], x[0]], axis=0)
    cos_t, sa_t, sb_t = _rope_tables(ctx_len, s_len)

    cc = jnp.zeros((8, D_MODEL), F32).at[0].set(c_ctx).at[1].set(c[0])
    mod_all = _mod_call(cc, w_mod.astype(BF16), b_mod[:, None, :])

    for layer in range(depth):
        mods = mod_all[layer, 0:2].reshape(2, N_MOD, D_MODEL)
        proj = _inproj_call(x_all, mods, w_in[layer].astype(BF16), ctx_blocks)
        qg = jnp.tile(q_norm[layer], 2)[None, :]
        kg = jnp.tile(k_norm[layer], 2)[None, :]
        qa, ka, va, qc, kc, vc = _prep_call(proj, cos_t, sa_t, sb_t, qg, kg)
        ya = _attn_a_call(qa, ka, va, ctx_len)
        yc = _attn_c_call(sink[layer], qc, kc, vc, ctx_len)
        x1, hq, qp = _merge_call(x_all, mods, ya, yc, proj, conv_w[layer],
                                 w_branch[layer].astype(BF16), w_out[layer].astype(BF16),
                                 ln1_g[layer][None, :], ln1_b[layer][None, :],
                                 w_pq[layer].astype(BF16), ctx_blocks)
        r2, w2, k1, w1 = _topk_call(qp, sub_keys[layer].astype(BF16))
        ft = _dense_call(hq, u_tab[layer].astype(BF16), v_tab[layer].T.astype(BF16), r2, w2, k1, w1)
        x_all = _ln2_call(x1, ft, mods, ln2_g[layer][None, :], ln2_b[layer][None, :], ctx_blocks)

    return x_all[ctx_len:][None]
```

```python
import functools
import math

import jax
import jax.numpy as jnp
from jax import lax
from jax.experimental import pallas as pl
from jax.experimental.pallas import tpu as pltpu

F32 = jnp.float32
BF16 = jnp.bfloat16

D_MODEL = 1024
HEAD_DIM = 64
N_HEADS = 8
N_KV = 2
GROUP = N_HEADS // N_KV
GRID_W = 64
WINDOW = 128
CONV_W = 3
B_WIDTH = 512
N_MOD = 6
ROPE_THETA = 10000.0
PEER_HEADS = 8
N_KEYS = 128
PEER_TOPK = 16
PEER_QDIM = 256
DEPTH = 2
IN_COLS = 6144
DEEPNORM_ALPHA = (2 * DEPTH) ** 0.25
LN_EPS = 1e-6
RMS_EPS = 1e-6
NEG_INF = -1e30
ATTN_SCALE = HEAD_DIM ** -0.5

ROW_BLOCK = 256
INPROJ_ROWS = 1280
KEY_CHUNK = 512
PEER_TOKENS = 1280
PEER_LANES = 256
TOPK_HEADS = 2
PEER_EXPERTS = 1024
VMEM_LIMIT = 56 * 1024 * 1024

CAND_COUNTS = tuple(PEER_TOPK // (j + 1) for j in range(PEER_TOPK))
N_CAND = sum(CAND_COUNTS)
CAND_ROWS = 56


def _cparams(sem):
    return pltpu.CompilerParams(dimension_semantics=sem, vmem_limit_bytes=VMEM_LIMIT)


def _nt_dot(a, b):
    return lax.dot_general(a, b, (((1,), (1,)), ((), ())), preferred_element_type=F32)


def _layer_norm(r, g, b):
    mu = jnp.mean(r, axis=-1, keepdims=True)
    rc = r - mu
    var = jnp.mean(rc * rc, axis=-1, keepdims=True)
    return rc * lax.rsqrt(var + LN_EPS) * g + b


def _mod_kernel(c_ref, w_ref, b_ref, o_ref):
    c = c_ref[...]
    h = (c * jax.nn.sigmoid(c)).astype(BF16)
    o_ref[0] = jnp.dot(h, w_ref[0], preferred_element_type=F32) + b_ref[0]


def _mod_call(cc, w_mod, b_mod):
    depth = w_mod.shape[0]
    cols = w_mod.shape[2]
    tn = 1024
    return pl.pallas_call(
        _mod_kernel,
        grid=(depth, cols // tn),
        in_specs=[
            pl.BlockSpec((8, D_MODEL), lambda l, j: (0, 0)),
            pl.BlockSpec((1, D_MODEL, tn), lambda l, j: (l, 0, j)),
            pl.BlockSpec((1, 1, tn), lambda l, j: (l, 0, j)),
        ],
        out_specs=pl.BlockSpec((1, 8, tn), lambda l, j: (l, 0, j)),
        out_shape=jax.ShapeDtypeStruct((depth, 8, cols), F32),
        compiler_params=_cparams(("arbitrary", "arbitrary")),
        name="mod",
    )(cc, w_mod, b_mod)


def _inproj_kernel(x_ref, mod_ref, w_ref, o_ref, *, ctx_len):
    tb = x_ref.shape[0]
    row = pl.program_id(1) * tb + lax.broadcasted_iota(jnp.int32, (tb, 1), 0)
    is_ctx = row < ctx_len
    sh = jnp.where(is_ctx, mod_ref[0, 0:1, :], mod_ref[1, 0:1, :])
    sc = jnp.where(is_ctx, mod_ref[0, 1:2, :], mod_ref[1, 1:2, :])
    hm = (x_ref[...] * (1.0 + sc) + sh).astype(BF16)
    o_ref[...] = jnp.dot(hm, w_ref[...], preferred_element_type=F32)


def _inproj_call(x_all, mods, w_in, ctx_len):
    n = x_all.shape[0]
    tn = 1024
    tb = INPROJ_ROWS
    return pl.pallas_call(
        functools.partial(_inproj_kernel, ctx_len=ctx_len),
        grid=(IN_COLS // tn, n // tb),
        in_specs=[
            pl.BlockSpec((tb, D_MODEL), lambda j, i: (i, 0)),
            pl.BlockSpec((2, N_MOD, D_MODEL), lambda j, i: (0, 0, 0)),
            pl.BlockSpec((D_MODEL, tn), lambda j, i: (0, j)),
        ],
        out_specs=pl.BlockSpec((tb, tn), lambda j, i: (i, j)),
        out_shape=jax.ShapeDtypeStruct((n, IN_COLS), F32),
        compiler_params=_cparams(("arbitrary", "arbitrary")),
        name="inproj",
    )(x_all, mods, w_in)


def _prep_kernel(p_ref, cos_ref, sa_ref, sb_ref, qg_ref, kg_ref,
                 qa_o, ka_o, va_o, qc_o, kc_o, vc_o):
    x = p_ref[...]
    cos = cos_ref[...]
    sa = sa_ref[...]
    sb = sb_ref[...]

    def tile(t, width):
        reps = width // 128
        return t if reps == 1 else jnp.concatenate([t] * reps, axis=1)

    def rope(y):
        w = y.shape[1]
        return (y * tile(cos, w)
                + pltpu.roll(y, w - 16, 1) * tile(sa, w)
                + pltpu.roll(y, 16, 1) * tile(sb, w))

    def inv_rms(xh):
        return lax.rsqrt(jnp.mean(xh * xh, axis=-1, keepdims=True) + RMS_EPS)

    qa = x[:, 0:512]
    ka = x[:, 512:640]
    va = x[:, 640:768]
    qc = x[:, 768:1280]
    kc = x[:, 1280:1408]
    vc = x[:, 1408:1536]

    qa_r = rope(qa * tile(qg_ref[...], 512))
    ka_r = rope(ka * kg_ref[...])
    qc_r = rope(qc)
    kc_r = rope(kc)

    for h in range(N_HEADS):
        sl = slice(h * HEAD_DIM, (h + 1) * HEAD_DIM)
        qa_o[h] = (qa_r[:, sl] * (inv_rms(qa[:, sl]) * ATTN_SCALE)).astype(BF16)
        qc_o[h] = (qc_r[:, sl] * ATTN_SCALE).astype(BF16)
    lane = lax.broadcasted_iota(jnp.int32, (x.shape[0], HEAD_DIM), 1)
    ones_col = jnp.where(lane == 0, 1.0, 0.0).astype(BF16)
    for h in range(N_KV):
        sl = slice(h * HEAD_DIM, (h + 1) * HEAD_DIM)
        ka_o[h] = (ka_r[:, sl] * inv_rms(ka[:, sl])).astype(BF16)
        va_o[h, :, 0:HEAD_DIM] = va[:, sl].astype(BF16)
        va_o[h, :, HEAD_DIM:2 * HEAD_DIM] = ones_col
        kc_o[h] = kc_r[:, sl].astype(BF16)
        vc_o[h] = vc[:, sl].astype(BF16)


def _prep_call(proj, cos_t, sa_t, sb_t, qg, kg):
    n = proj.shape[0]
    tb = ROW_BLOCK
    tab = pl.BlockSpec((tb, 128), lambda i: (i, 0))
    gain = pl.BlockSpec((1, 128), lambda i: (0, 0))

    def hm(nh, width=HEAD_DIM):
        return pl.BlockSpec((nh, tb, width), lambda i: (0, i, 0))

    def hshape(nh, width=HEAD_DIM):
        return jax.ShapeDtypeStruct((nh, n, width), BF16)

    return pl.pallas_call(
        _prep_kernel,
        grid=(n // tb,),
        in_specs=[pl.BlockSpec((tb, 1536), lambda i: (i, 0)), tab, tab, tab, gain, gain],
        out_specs=[hm(N_HEADS), hm(N_KV), hm(N_KV, 2 * HEAD_DIM), hm(N_HEADS), hm(N_KV), hm(N_KV)],
        out_shape=[hshape(N_HEADS), hshape(N_KV), hshape(N_KV, 2 * HEAD_DIM), hshape(N_HEADS), hshape(N_KV),
                   hshape(N_KV)],
        compiler_params=_cparams(("arbitrary",)),
        name="prep",
    )(proj, cos_t, sa_t, sb_t, qg, kg)


def _attn_a_kernel(q_ref, k_ref, v_ref, o_ref, s_scr, m_scr, acc_scr, *,
                   ctx_len, ctx_blocks, lat_chunks, unroll):
    i = pl.program_id(1)
    tq = q_ref.shape[1]
    q = q_ref[...].reshape(GROUP * tq, HEAD_DIM)

    def scores(off, width):
        return _nt_dot(q, k_ref[0, pl.ds(off, width), :])

    def update(s, off, width):
        m_old = m_scr[...]
        m_new = jnp.maximum(m_old, jnp.max(s, axis=-1, keepdims=True))
        alpha = jnp.exp(m_old - m_new)
        p = jnp.exp(s - jnp.concatenate([m_new] * (width // 128), axis=1))
        acc_scr[...] = alpha * acc_scr[...] + jnp.dot(
            p.astype(BF16), v_ref[0, pl.ds(off, width), :], preferred_element_type=F32)
        m_scr[...] = m_new

    def chunk_off(c):
        return pl.multiple_of(ctx_len + c * KEY_CHUNK, 128)

    m_scr[...] = jnp.full(m_scr.shape, -jnp.inf, F32)
    acc_scr[...] = jnp.zeros(acc_scr.shape, F32)
    update(scores(0, ctx_len), 0, ctx_len)

    @pl.when(i >= ctx_blocks)
    def _():
        s_scr[0] = scores(chunk_off(0), KEY_CHUNK)

        def body(it, carry):
            for u in range(0, unroll, 2):
                c = unroll * it + u
                s_scr[1] = scores(chunk_off(c + 1), KEY_CHUNK)
                update(s_scr[0], chunk_off(c), KEY_CHUNK)
                s_scr[0] = scores(chunk_off(jnp.minimum(c + 2, lat_chunks - 1)), KEY_CHUNK)
                update(s_scr[1], chunk_off(c + 1), KEY_CHUNK)
            return carry

        lax.fori_loop(0, lat_chunks // unroll, body, 0)

    acc = acc_scr[...]
    o = acc[:, 0:HEAD_DIM] / acc[:, HEAD_DIM:HEAD_DIM + 1]
    o_ref[...] = o.reshape(GROUP, tq, HEAD_DIM).astype(BF16)


def _attn_a_call(q, k, v, ctx_len):
    n = q.shape[1]
    tq = ROW_BLOCK
    lat_chunks = (n - ctx_len) // KEY_CHUNK
    assert lat_chunks % 2 == 0
    unroll = 4 if lat_chunks % 4 == 0 else 2
    kern = functools.partial(_attn_a_kernel, ctx_len=ctx_len, ctx_blocks=ctx_len // tq, lat_chunks=lat_chunks,
                             unroll=unroll)
    return pl.pallas_call(
        kern,
        grid=(N_KV, n // tq),
        in_specs=[
            pl.BlockSpec((GROUP, tq, HEAD_DIM), lambda j, i: (j, i, 0)),
            pl.BlockSpec((1, n, HEAD_DIM), lambda j, i: (j, 0, 0)),
            pl.BlockSpec((1, n, 2 * HEAD_DIM), lambda j, i: (j, 0, 0)),
        ],
        out_specs=pl.BlockSpec((GROUP, tq, HEAD_DIM), lambda j, i: (j, i, 0)),
        out_shape=jax.ShapeDtypeStruct((N_HEADS, n, HEAD_DIM), BF16),
        scratch_shapes=[
            pltpu.VMEM((2, GROUP * tq, KEY_CHUNK), F32),
            pltpu.VMEM((GROUP * tq, 128), F32),
            pltpu.VMEM((GROUP * tq, 2 * HEAD_DIM), F32),
        ],
        compiler_params=_cparams(("arbitrary", "arbitrary")),
        name="attn_global",
    )(q, k, v)


def _attn_c_kernel(sink_ref, q_ref, k_ref, v_ref, o_ref, *, ctx_len, ctx_blocks, s_len):
    j = pl.program_id(0)
    i = pl.program_id(1)
    tq = q_ref.shape[1]
    rows = GROUP * tq
    n_loc = tq + 2 * WINDOW
    q = q_ref[...].reshape(rows, HEAD_DIM)

    p0 = i * tq - ctx_len
    ws = pl.multiple_of(jnp.clip(p0 - WINDOW, 0, s_len - n_loc), 128)
    k_loc = k_ref[0, pl.ds(ctx_len + ws, n_loc), :]
    v_loc = v_ref[0, pl.ds(ctx_len + ws, n_loc), :]
    k_ctx = k_ref[0, 0:ctx_len, :]
    v_ctx = v_ref[0, 0:ctx_len, :]

    s_loc = _nt_dot(q, k_loc)
    t_pos = p0 + (lax.broadcasted_iota(jnp.int32, (rows, n_loc), 0) & (tq - 1))
    s_pos = ws + lax.broadcasted_iota(jnp.int32, (rows, n_loc), 1)
    valid = (jnp.abs(t_pos - s_pos) <= WINDOW) & (i >= ctx_blocks)
    s_loc = jnp.where(valid, s_loc, NEG_INF)
    s_ctx = _nt_dot(q, k_ctx)

    grp = lax.broadcasted_iota(jnp.int32, (rows, 1), 0) // tq
    sink = jnp.full((rows, 1), sink_ref[j * GROUP + GROUP - 1], F32)
    for g in range(GROUP - 1):
        sink = jnp.where(grp == g, sink_ref[j * GROUP + g], sink)

    m = jnp.maximum(jnp.maximum(jnp.max(s_loc, axis=-1, keepdims=True),
                                jnp.max(s_ctx, axis=-1, keepdims=True)), sink)
    p_loc = jnp.exp(s_loc - m)
    p_ctx = jnp.exp(s_ctx - m)
    l = (jnp.sum(p_loc, axis=-1, keepdims=True) + jnp.sum(p_ctx, axis=-1, keepdims=True)
         + jnp.exp(sink - m))
    o = (jnp.dot(p_loc.astype(BF16), v_loc, preferred_element_type=F32)
         + jnp.dot(p_ctx.astype(BF16), v_ctx, preferred_element_type=F32)) / l
    o_ref[...] = o.reshape(GROUP, tq, HEAD_DIM).astype(BF16)


def _attn_c_call(sink, q, k, v, ctx_len):
    n = q.shape[1]
    tq = ROW_BLOCK
    kern = functools.partial(_attn_c_kernel, ctx_len=ctx_len, ctx_blocks=ctx_len // tq, s_len=n - ctx_len)
    return pl.pallas_call(
        kern,
        grid=(N_KV, n // tq),
        in_specs=[
            pl.BlockSpec(memory_space=pltpu.SMEM),
            pl.BlockSpec((GROUP, tq, HEAD_DIM), lambda j, i: (j, i, 0)),
            pl.BlockSpec((1, n, HEAD_DIM), lambda j, i: (j, 0, 0)),
            pl.BlockSpec((1, n, HEAD_DIM), lambda j, i: (j, 0, 0)),
        ],
        out_specs=pl.BlockSpec((GROUP, tq, HEAD_DIM), lambda j, i: (j, i, 0)),
        out_shape=jax.ShapeDtypeStruct((N_HEADS, n, HEAD_DIM), BF16),
        compiler_params=_cparams(("arbitrary", "arbitrary")),
        name="attn_window",
    )(sink, q, k, v)


def _merge_kernel(x_ref, mod_ref, ya_ref, yc_ref, xb_ref, gb_ref, gc_ref,
                  xbp_ref, gcp_ref, xbn_ref, gcn_ref, gl_ref, cw_ref, wb_ref, wo_ref,
                  lg_ref, lb_ref, wpq_ref, x1_ref, hq_ref, qp_ref, *, n_blocks, ctx_blocks):
    i = pl.program_id(0)
    tb = x_ref.shape[0]

    z = gc_ref[...] * xb_ref[...]
    z_before = gcp_ref[7:8, :] * xbp_ref[7:8, :]
    z_after = gcn_ref[0:1, :] * xbn_ref[0:1, :]
    seq_start = (i == 0) | (i == ctx_blocks)
    seq_end = (i == ctx_blocks - 1) | (i == n_blocks - 1)
    z_before = jnp.where(seq_start, 0.0, z_before)
    z_after = jnp.where(seq_end, 0.0, z_after)
    rid = lax.broadcasted_iota(jnp.int32, z.shape, 0)
    z_prev = jnp.where(rid == 0, z_before, pltpu.roll(z, 1, 0))
    z_next = jnp.where(rid == tb - 1, z_after, pltpu.roll(z, tb - 1, 0))
    cw = cw_ref[...]
    yb = gb_ref[...] * (cw[0:1, :] * z_prev + cw[1:2, :] * z + cw[2:3, :] * z_next)

    def head_proj(y_ref, b):
        acc = jnp.dot(y_ref[0], wb_ref[b, 0:HEAD_DIM, :], preferred_element_type=F32)
        for h in range(1, N_HEADS):
            acc = acc + jnp.dot(y_ref[h], wb_ref[b, h * HEAD_DIM:(h + 1) * HEAD_DIM, :],
                                preferred_element_type=F32)
        return acc

    pa = head_proj(ya_ref, 0)
    pb = jnp.dot(yb.astype(BF16), wb_ref[1], preferred_element_type=F32)
    pc = head_proj(yc_ref, 2)
    gates = jax.nn.sigmoid(gl_ref[...])
    m = (gates[:, 0:D_MODEL] * pa + gates[:, D_MODEL:2 * D_MODEL] * pb
         + gates[:, 2 * D_MODEL:3 * D_MODEL] * pc)
    y = jnp.dot(m.astype(BF16), wo_ref[...], preferred_element_type=F32)

    g1 = mod_ref[0, 2:3, :]
    sh2 = mod_ref[0, 3:4, :]
    sc2 = mod_ref[0, 4:5, :]
    x1 = _layer_norm(DEEPNORM_ALPHA * x_ref[...] + g1 * y, lg_ref[...], lb_ref[...])
    x1_ref[...] = x1
    hq = (x1 * (1.0 + sc2) + sh2).astype(BF16)
    hq_ref[...] = hq
    qp_ref[...] = jnp.dot(hq, wpq_ref[...], preferred_element_type=F32)


def _merge_call(x_all, mods, ya, yc, proj, conv_w, w_branch, w_out, ln_g, ln_b, w_pq, ctx_blocks):
    n = x_all.shape[0]
    tb = ROW_BLOCK
    nb = n // tb
    halo = tb // 8
    last8 = n // 8 - 1
    kern = functools.partial(_merge_kernel, n_blocks=nb, ctx_blocks=ctx_blocks)
    qcols = PEER_HEADS * PEER_QDIM

    def full(shape):
        return pl.BlockSpec(shape, lambda i: (0,) * len(shape))

    def cols512(c):
        return pl.BlockSpec((tb, B_WIDTH), lambda i: (i, c))

    def before(c):
        return pl.BlockSpec((8, B_WIDTH), lambda i: (jnp.maximum(i * halo - 1, 0), c))

    def after(c):
        return pl.BlockSpec((8, B_WIDTH), lambda i: (jnp.minimum((i + 1) * halo, last8), c))

    return pl.pallas_call(
        kern,
        grid=(nb,),
        in_specs=[
            pl.BlockSpec((tb, D_MODEL), lambda i: (i, 0)),
            pl.BlockSpec((1, N_MOD, D_MODEL), lambda i: (jnp.where(i < ctx_blocks, 0, 1), 0, 0)),
            pl.BlockSpec((N_HEADS, tb, HEAD_DIM), lambda i: (0, i, 0)),
            pl.BlockSpec((N_HEADS, tb, HEAD_DIM), lambda i: (0, i, 0)),
            cols512(3), cols512(4), cols512(5),
            before(3), before(5), after(3), after(5),
            pl.BlockSpec((tb, 3 * D_MODEL), lambda i: (i, 1)),
            full((CONV_W, B_WIDTH)),
            full((3, B_WIDTH, D_MODEL)),
            full((D_MODEL, D_MODEL)),
            full((1, D_MODEL)), full((1, D_MODEL)),
            full((D_MODEL, qcols)),
        ],
        out_specs=[
            pl.BlockSpec((tb, D_MODEL), lambda i: (i, 0)),
            pl.BlockSpec((tb, D_MODEL), lambda i: (i, 0)),
            pl.BlockSpec((tb, qcols), lambda i: (i, 0)),
        ],
        out_shape=[
            jax.ShapeDtypeStruct((n, D_MODEL), F32),
            jax.ShapeDtypeStruct((n, D_MODEL), BF16),
            jax.ShapeDtypeStruct((n, qcols), F32),
        ],
        compiler_params=_cparams(("arbitrary",)),
        name="merge",
    )(x_all, mods, ya, yc, proj, proj, proj, proj, proj, proj, proj, proj,
      conv_w, w_branch, w_out, ln_g, ln_b, w_pq)


def _extract_rounds(chains, rounds, on_round, tie_break):
    rows = chains[0].shape[0]
    ridx = lax.broadcasted_iota(jnp.int32, chains[0].shape, 0)
    chains = list(chains)
    ranks = [jnp.full(c.shape, rounds, jnp.int32) for c in chains]
    for r in range(rounds):
        for i, s in enumerate(chains):
            m = jnp.max(s, axis=0, keepdims=True)
            if tie_break:
                first = jnp.min(jnp.where(s == m, ridx, rows), axis=0, keepdims=True)
                hit = ridx == first
            else:
                hit = s == m
            ranks[i] = jnp.where(hit, r, ranks[i])
            chains[i] = jnp.where(hit, -jnp.inf, s)
            on_round(i, r, m)
    return ranks


def _extract_topk(chains, rounds, on_round, rank_ref):
    ranks = _extract_rounds(chains, rounds, on_round, tie_break=False)
    most = jnp.float32(0.0)
    for i, rank in enumerate(ranks):
        rank_ref[i] = rank
        most = jnp.maximum(most, jnp.max(jnp.sum((rank < rounds).astype(F32), axis=0, keepdims=True)))

    @pl.when(most > rounds)
    def _():
        for i, rank in enumerate(_extract_rounds(chains, rounds, on_round, tie_break=True)):
            rank_ref[i] = rank


def _topk_kernel(qp_ref, sk_ref, r2_o, w2_o, k1_o, w1_o, tv_scr, cand_scr, kj_scr, rank_scr, crank_scr):
    tt = qp_ref.shape[0]
    half = PEER_QDIM // 2
    q = qp_ref[...].astype(BF16)
    heads = range(TOPK_HEADS)

    scores = [_nt_dot(sk_ref[h, p], q[:, (2 * h + p) * half:(2 * h + p + 1) * half])
              for h in heads for p in range(2)]

    def keep(i, r, m):
        tv_scr[i, r:r + 1, :] = m

    _extract_topk(scores, PEER_TOPK, keep, rank_scr)

    for h in heads:
        tv1 = tv_scr[2 * h]
        tv2 = tv_scr[2 * h + 1]
        off = 0
        for j, cnt in enumerate(CAND_COUNTS):
            cand_scr[h, off:off + cnt, :] = tv1[j:j + 1, :] + tv2[0:cnt, :]
            off += cnt
        cand_scr[h, N_CAND:CAND_ROWS, :] = jnp.full((CAND_ROWS - N_CAND, tt), -jnp.inf, F32)
    cands = [cand_scr[h] for h in heads]
    _extract_topk(cands, PEER_TOPK, lambda i, r, m: None, crank_scr)

    def twice_bf16(x):
        hi = lax.bitcast_convert_type(x.astype(BF16).astype(F32), jnp.uint32)
        return lax.bitcast_convert_type(hi | (hi >> 16), F32)

    for h in heads:
        cand = cands[h]
        chosen = crank_scr[h] < PEER_TOPK
        cmax = cand[0:1, :]
        z = jnp.sum(jnp.where(chosen, jnp.exp(cand - cmax), 0.0), axis=0, keepdims=True)
        cnt_f = chosen.astype(F32)
        off = 0
        for j, cnt in enumerate(CAND_COUNTS):
            kj_scr[h, j:j + 1, :] = jnp.sum(cnt_f[off:off + cnt, :], axis=0, keepdims=True)
            off += cnt

        rank1 = rank_scr[2 * h]
        rank2 = rank_scr[2 * h + 1]
        k1 = jnp.zeros((N_KEYS, tt), F32)
        for j in range(PEER_TOPK):
            k1 = jnp.where(rank1 == j, kj_scr[h, j:j + 1, :], k1)
        k1_o[h] = twice_bf16(k1)
        w1_o[h] = twice_bf16(
            jnp.where(rank1 < PEER_TOPK, jnp.exp(scores[2 * h] - tv_scr[2 * h, 0:1, :]), 0.0) / z)
        r2_o[h] = rank2.astype(F32).astype(BF16)
        w2_o[h] = jnp.where(rank2 < PEER_TOPK, jnp.exp(scores[2 * h + 1] - tv_scr[2 * h + 1, 0:1, :]),
                            0.0).astype(BF16)


def _topk_call(qp, sub_keys):
    n = qp.shape[0]
    tt = ROW_BLOCK
    hp = TOPK_HEADS
    out = pl.BlockSpec((hp, N_KEYS, tt), lambda t, h: (h, 0, t))
    shape = jax.ShapeDtypeStruct((PEER_HEADS, N_KEYS, n), F32)
    shape_bf16 = jax.ShapeDtypeStruct((PEER_HEADS, N_KEYS, n), BF16)
    return pl.pallas_call(
        _topk_kernel,
        grid=(n // tt, PEER_HEADS // hp),
        in_specs=[
            pl.BlockSpec((tt, hp * PEER_QDIM), lambda t, h: (t, h)),
            pl.BlockSpec((hp, 2, N_KEYS, PEER_QDIM // 2), lambda t, h: (h, 0, 0, 0)),
        ],
        out_specs=[out, out, out, out],
        out_shape=[shape_bf16, shape_bf16, shape, shape],
        scratch_shapes=[
            pltpu.VMEM((2 * hp, PEER_TOPK, tt), F32),
            pltpu.VMEM((hp, CAND_ROWS, tt), F32),
            pltpu.VMEM((hp, PEER_TOPK, tt), F32),
            pltpu.VMEM((2 * hp, N_KEYS, tt), jnp.int32),
            pltpu.VMEM((hp, CAND_ROWS, tt), jnp.int32),
        ],
        compiler_params=_cparams(("arbitrary", "arbitrary")),
        name="peer_topk",
    )(qp, sub_keys)


_ERF_A = (-2.72614225801306e-10, 2.77068142495902e-08, -2.10102402082508e-06, -5.69250639462346e-05,
          -7.34990630326855e-04, -2.95459980854025e-03, -1.60960333262415e-02)
_ERF_B = (-1.45660718464996e-05, -2.13374055278905e-04, -1.68282697438203e-03, -7.37332916720468e-03,
          -1.42647390514189e-02)


def _gelu(x):
    dt = x.dtype
    lim = 4.0 * math.sqrt(2.0)
    xc = jnp.clip(x, -lim, lim)
    t = xc * xc
    deg_a = len(_ERF_A) - 1
    deg_b = len(_ERF_B) - 1
    a2 = [c * (0.5 / math.sqrt(2.0)) / 2.0 ** (deg_a - i) for i, c in enumerate(_ERF_A)]
    b2 = [c / 2.0 ** (deg_b - i) for i, c in enumerate(_ERF_B)]
    a = jnp.asarray(a2[0], dt)
    for c in a2[1:]:
        a = a * t + jnp.asarray(c, dt)
    b = jnp.asarray(b2[0], dt)
    for c in b2[1:]:
        b = b * t + jnp.asarray(c, dt)
    return 0.5 * x + (x * xc) * (a / b)


def _dense_kernel(hq_ref, u_ref, vt_ref, r2_ref, w2_ref, k1_ref, w1_ref, o_ref, h_scr, a_scr, *, n_eb):
    s = pl.program_id(0)
    eb = u_ref.shape[0]
    tt = hq_ref.shape[0]
    n_i1 = eb // N_KEYS

    @pl.when(s == 0)
    def _():
        h_scr[...] = jnp.zeros(h_scr.shape, BF16)
        a_scr[...] = jnp.zeros(a_scr.shape, BF16)

    @pl.when(jnp.maximum(s - 2, 0) % n_eb == 0)
    def _():
        o_ref[...] = jnp.zeros(o_ref.shape, F32)

    def rows16(ref, h, ii, cs):
        row = ref[h, ii:ii + 1, cs]
        return pltpu.bitcast(jnp.broadcast_to(row, (8, PEER_LANES)), BF16)

    for c in range(tt // PEER_LANES):
        cs = slice(c * PEER_LANES, (c + 1) * PEER_LANES)

        o_ref[:, cs] += jnp.dot(vt_ref[...], a_scr[:, cs], preferred_element_type=F32)

        for ii in range(n_i1):
            rows = slice(ii * N_KEYS, (ii + 1) * N_KEYS)
            g = jnp.zeros((N_KEYS, PEER_LANES), BF16)
            for h in range(PEER_HEADS):
                k1b = jnp.concatenate([rows16(k1_ref, h, ii, cs)] * (N_KEYS // 16), axis=0)
                w1b = jnp.concatenate([rows16(w1_ref, h, ii, cs)] * (N_KEYS // 16), axis=0)
                keep = r2_ref[h, :, cs] < k1b
                g = g + jnp.where(keep, w2_ref[h, :, cs] * w1b, 0.0)
            a_scr[rows, cs] = _gelu(h_scr[rows, cs]) * g

        h_scr[:, cs] = _nt_dot(u_ref[...], hq_ref[cs, :]).astype(BF16)


def _dense_call(hq, u_tab, vt_tab, r2, w2, k1, w1):
    n = hq.shape[0]
    n_exp = u_tab.shape[0]
    tt = PEER_TOKENS
    eb = PEER_EXPERTS
    n_eb = n_exp // eb
    steps = (n // tt) * n_eb

    def blk(s, lag):
        return jnp.clip(s - lag, 0, steps - 1)

    tab = pl.BlockSpec((PEER_HEADS, N_KEYS, tt), lambda s: (0, 0, blk(s, 1) // n_eb))
    row = pl.BlockSpec((PEER_HEADS, eb // N_KEYS, tt), lambda s: (0, blk(s, 1) % n_eb, blk(s, 1) // n_eb))
    return pl.pallas_call(
        functools.partial(_dense_kernel, n_eb=n_eb),
        grid=(steps + 2,),
        in_specs=[
            pl.BlockSpec((tt, D_MODEL), lambda s: (blk(s, 0) // n_eb, 0)),
            pl.BlockSpec((eb, D_MODEL), lambda s: (blk(s, 0) % n_eb, 0)),
            pl.BlockSpec((D_MODEL, eb), lambda s: (0, blk(s, 2) % n_eb)),
            tab, tab, row, row,
        ],
        out_specs=pl.BlockSpec((D_MODEL, tt), lambda s: (0, blk(s, 2) // n_eb)),
        out_shape=jax.ShapeDtypeStruct((D_MODEL, n), F32),
        scratch_shapes=[pltpu.VMEM((eb, tt), BF16), pltpu.VMEM((eb, tt), BF16)],
        compiler_params=_cparams(("arbitrary",)),
        name="peer_dense",
    )(hq, u_tab, vt_tab, r2, w2, k1, w1)


def _ln2_kernel(x1_ref, ft_ref, mod_ref, lg_ref, lb_ref, o_ref):
    g2 = mod_ref[0, 5:6, :]
    f = ft_ref[...].T
    o_ref[...] = _layer_norm(DEEPNORM_ALPHA * x1_ref[...] + g2 * f, lg_ref[...], lb_ref[...])


def _ln2_call(x1, ft, mods, ln_g, ln_b, ctx_blocks):
    n = x1.shape[0]
    tb = ROW_BLOCK
    return pl.pallas_call(
        _ln2_kernel,
        grid=(n // tb,),
        in_specs=[
            pl.BlockSpec((tb, D_MODEL), lambda i: (i, 0)),
            pl.BlockSpec((D_MODEL, tb), lambda i: (0, i)),
            pl.BlockSpec((1, N_MOD, D_MODEL), lambda i: (jnp.where(i < ctx_blocks, 0, 1), 0, 0)),
            pl.BlockSpec((1, D_MODEL), lambda i: (0, 0)),
            pl.BlockSpec((1, D_MODEL), lambda i: (0, 0)),
        ],
        out_specs=pl.BlockSpec((tb, D_MODEL), lambda i: (i, 0)),
        out_shape=jax.ShapeDtypeStruct((n, D_MODEL), F32),
        compiler_params=_cparams(("arbitrary",)),
        name="ln2",
    )(x1, ft, mods, ln_g, ln_b)


def _rope_tables(ctx_len, s_len):
    t = jnp.arange(s_len, dtype=jnp.int32)
    pos = jnp.stack([t // GRID_W, t % GRID_W], axis=-1).astype(F32)
    quarter = HEAD_DIM // 4
    inv_freq = ROPE_THETA ** (-jnp.arange(quarter, dtype=F32) / quarter)
    ang = pos[:, :, None] * inv_freq
    cos = jnp.repeat(jnp.cos(ang), 2, axis=1).reshape(s_len, HEAD_DIM)
    sin = jnp.repeat(jnp.sin(ang), 2, axis=1).reshape(s_len, HEAD_DIM)
    first_half = (jnp.arange(HEAD_DIM) % 32) < 16
    sa = jnp.where(first_half, -sin, 0.0)
    sb = jnp.where(first_half, 0.0, sin)

    def full(tab, fill):
        tab = jnp.concatenate([jnp.full((ctx_len, HEAD_DIM), fill, F32), tab], axis=0)
        return jnp.tile(tab, (1, 2))

    return full(cos, 1.0), full(sa, 0.0), full(sb, 0.0)


def kernel(x, c, ctx, c_ctx, w_mod, b_mod, w_in, q_norm, k_norm, conv_w, sink, w_branch, w_out,
           ln1_g, ln1_b, w_pq, sub_keys, u_tab, v_tab, ln2_g, ln2_b):
    bsz, s_len, d = x.shape
    ctx_len = ctx.shape[1]
    depth = w_mod.shape[0]
    assert bsz == 1 and d == D_MODEL and depth == DEPTH
    assert ctx_len % ROW_BLOCK == 0 and s_len % KEY_CHUNK == 0
    assert (ctx_len + s_len) % PEER_TOKENS == 0 and (ctx_len + s_len) % INPROJ_ROWS == 0
    ctx_blocks = ctx_len // ROW_BLOCK

    x_all = jnp.concatenate([ctx[0], x[0]], axis=0)
    cos_t, sa_t, sb_t = _rope_tables(ctx_len, s_len)

    cc = jnp.zeros((8, D_MODEL), F32).at[0].set(c_ctx).at[1].set(c[0])
    mod_all = _mod_call(cc, w_mod.astype(BF16), b_mod[:, None, :])

    for layer in range(depth):
        mods = mod_all[layer, 0:2].reshape(2, N_MOD, D_MODEL)
        proj = _inproj_call(x_all, mods, w_in[layer].astype(BF16), ctx_len)
        qg = jnp.tile(q_norm[layer], 2)[None, :]
        kg = jnp.tile(k_norm[layer], 2)[None, :]
        qa, ka, va, qc, kc, vc = _prep_call(proj, cos_t, sa_t, sb_t, qg, kg)
        ya = _attn_a_call(qa, ka, va, ctx_len)
        yc = _attn_c_call(sink[layer], qc, kc, vc, ctx_len)
        x1, hq, qp = _merge_call(x_all, mods, ya, yc, proj, conv_w[layer],
                                 w_branch[layer].astype(BF16), w_out[layer].astype(BF16),
                                 ln1_g[layer][None, :], ln1_b[layer][None, :],
                                 w_pq[layer].astype(BF16), ctx_blocks)
        r2, w2, k1, w1 = _topk_call(qp, sub_keys[layer].astype(BF16))
        ft = _dense_call(hq, u_tab[layer].astype(BF16), v_tab[layer].T.astype(BF16), r2, w2, k1, w1)
        x_all = _ln2_call(x1, ft, mods, ln2_g[layer][None, :], ln2_b[layer][None, :], ctx_blocks)

    return x_all[ctx_len:][None]
```

```python
import functools
import math

import jax
import jax.numpy as jnp
import numpy as np
from jax import lax
from jax.experimental import pallas as pl
from jax.experimental.pallas import tpu as pltpu

F32 = jnp.float32
BF16 = jnp.bfloat16

D_MODEL = 1024
HEAD_DIM = 64
N_HEADS = 8
N_KV = 2
GROUP = N_HEADS // N_KV
GRID_W = 64
WINDOW = 128
CONV_W = 3
B_WIDTH = 512
N_MOD = 6
ROPE_THETA = 10000.0
PEER_HEADS = 8
N_KEYS = 128
PEER_TOPK = 16
PEER_QDIM = 256
DEPTH = 2
IN_COLS = 6144
DEEPNORM_ALPHA = (2 * DEPTH) ** 0.25
LN_EPS = 1e-6
RMS_EPS = 1e-6
NEG_INF = -1e30
ATTN_SCALE = HEAD_DIM ** -0.5

ROW_BLOCK = 256
INPROJ_ROWS = 1280
KEY_CHUNK = 512
PEER_TOKENS = 1280
PEER_LANES = 256
TOPK_HEADS = 2
PEER_EXPERTS = 1024
VMEM_LIMIT = 56 * 1024 * 1024

CAND_COUNTS = tuple(PEER_TOPK // (j + 1) for j in range(PEER_TOPK))
N_CAND = sum(CAND_COUNTS)
CAND_ROWS = 56


def _cparams(sem):
    return pltpu.CompilerParams(dimension_semantics=sem, vmem_limit_bytes=VMEM_LIMIT)


def _nt_dot(a, b):
    return lax.dot_general(a, b, (((1,), (1,)), ((), ())), preferred_element_type=F32)


def _layer_norm(r, g, b):
    mu = jnp.mean(r, axis=-1, keepdims=True)
    rc = r - mu
    var = jnp.mean(rc * rc, axis=-1, keepdims=True)
    return rc * lax.rsqrt(var + LN_EPS) * g + b


def _mod_kernel(c_ref, w_ref, b_ref, o_ref):
    c = c_ref[...]
    h = (c * jax.nn.sigmoid(c)).astype(BF16)
    o_ref[0] = jnp.dot(h, w_ref[0], preferred_element_type=F32) + b_ref[0]


def _mod_call(cc, w_mod, b_mod):
    depth = w_mod.shape[0]
    cols = w_mod.shape[2]
    tn = 1024
    return pl.pallas_call(
        _mod_kernel,
        grid=(depth, cols // tn),
        in_specs=[
            pl.BlockSpec((8, D_MODEL), lambda l, j: (0, 0)),
            pl.BlockSpec((1, D_MODEL, tn), lambda l, j: (l, 0, j)),
            pl.BlockSpec((1, 1, tn), lambda l, j: (l, 0, j)),
        ],
        out_specs=pl.BlockSpec((1, 8, tn), lambda l, j: (l, 0, j)),
        out_shape=jax.ShapeDtypeStruct((depth, 8, cols), F32),
        compiler_params=_cparams(("arbitrary", "arbitrary")),
        name="mod",
    )(cc, w_mod, b_mod)


def _inproj_kernel(x_ref, mod_ref, w_ref, o_ref, *, ctx_len):
    tb = x_ref.shape[0]
    row = pl.program_id(1) * tb + lax.broadcasted_iota(jnp.int32, (tb, 1), 0)
    is_ctx = row < ctx_len
    sh = jnp.where(is_ctx, mod_ref[0, 0:1, :], mod_ref[1, 0:1, :])
    sc = jnp.where(is_ctx, mod_ref[0, 1:2, :], mod_ref[1, 1:2, :])
    hm = (x_ref[...] * (1.0 + sc) + sh).astype(BF16)
    o_ref[...] = jnp.dot(hm, w_ref[...], preferred_element_type=F32)


def _inproj_call(x_all, mods, w_in, ctx_len):
    n = x_all.shape[0]
    tn = 1024
    tb = INPROJ_ROWS
    return pl.pallas_call(
        functools.partial(_inproj_kernel, ctx_len=ctx_len),
        grid=(IN_COLS // tn, n // tb),
        in_specs=[
            pl.BlockSpec((tb, D_MODEL), lambda j, i: (i, 0)),
            pl.BlockSpec((2, N_MOD, D_MODEL), lambda j, i: (0, 0, 0)),
            pl.BlockSpec((D_MODEL, tn), lambda j, i: (0, j)),
        ],
        out_specs=pl.BlockSpec((tb, tn), lambda j, i: (i, j)),
        out_shape=jax.ShapeDtypeStruct((n, IN_COLS), F32),
        compiler_params=_cparams(("arbitrary", "arbitrary")),
        name="inproj",
    )(x_all, mods, w_in)


def _prep_kernel(p_ref, cos_ref, sa_ref, sb_ref, qg_ref, kg_ref,
                 qa_o, ka_o, va_o, qc_o, kc_o, vc_o):
    x = p_ref[...]
    cos = cos_ref[...]
    sa = sa_ref[...]
    sb = sb_ref[...]

    def tile(t, width):
        reps = width // 128
        return t if reps == 1 else jnp.concatenate([t] * reps, axis=1)

    def rope(y):
        w = y.shape[1]
        return (y * tile(cos, w)
                + pltpu.roll(y, w - 16, 1) * tile(sa, w)
                + pltpu.roll(y, 16, 1) * tile(sb, w))

    def inv_rms(xh):
        return lax.rsqrt(jnp.mean(xh * xh, axis=-1, keepdims=True) + RMS_EPS)

    qa = x[:, 0:512]
    ka = x[:, 512:640]
    va = x[:, 640:768]
    qc = x[:, 768:1280]
    kc = x[:, 1280:1408]
    vc = x[:, 1408:1536]

    qa_r = rope(qa * tile(qg_ref[...], 512))
    ka_r = rope(ka * kg_ref[...])
    qc_r = rope(qc)
    kc_r = rope(kc)

    for h in range(N_HEADS):
        sl = slice(h * HEAD_DIM, (h + 1) * HEAD_DIM)
        qa_o[h] = (qa_r[:, sl] * (inv_rms(qa[:, sl]) * ATTN_SCALE)).astype(BF16)
        qc_o[h] = (qc_r[:, sl] * ATTN_SCALE).astype(BF16)
    lane = lax.broadcasted_iota(jnp.int32, (x.shape[0], HEAD_DIM), 1)
    ones_col = jnp.where(lane == 0, 1.0, 0.0).astype(BF16)
    for h in range(N_KV):
        sl = slice(h * HEAD_DIM, (h + 1) * HEAD_DIM)
        ka_o[h] = (ka_r[:, sl] * inv_rms(ka[:, sl])).astype(BF16)
        va_o[h, :, 0:HEAD_DIM] = va[:, sl].astype(BF16)
        va_o[h, :, HEAD_DIM:2 * HEAD_DIM] = ones_col
        kc_o[h] = kc_r[:, sl].astype(BF16)
        vc_o[h] = vc[:, sl].astype(BF16)


def _prep_call(proj, cos_t, sa_t, sb_t, qg, kg):
    n = proj.shape[0]
    tb = ROW_BLOCK
    tab = pl.BlockSpec((tb, 128), lambda i: (i, 0))
    gain = pl.BlockSpec((1, 128), lambda i: (0, 0))

    def hm(nh, width=HEAD_DIM):
        return pl.BlockSpec((nh, tb, width), lambda i: (0, i, 0))

    def hshape(nh, width=HEAD_DIM):
        return jax.ShapeDtypeStruct((nh, n, width), BF16)

    return pl.pallas_call(
        _prep_kernel,
        grid=(n // tb,),
        in_specs=[pl.BlockSpec((tb, 1536), lambda i: (i, 0)), tab, tab, tab, gain, gain],
        out_specs=[hm(N_HEADS), hm(N_KV), hm(N_KV, 2 * HEAD_DIM), hm(N_HEADS), hm(N_KV), hm(N_KV)],
        out_shape=[hshape(N_HEADS), hshape(N_KV), hshape(N_KV, 2 * HEAD_DIM), hshape(N_HEADS), hshape(N_KV),
                   hshape(N_KV)],
        compiler_params=_cparams(("arbitrary",)),
        name="prep",
    )(proj, cos_t, sa_t, sb_t, qg, kg)


def _store_heads(o_ref, o, tq):
    for g in range(GROUP):
        o_ref[:, g * HEAD_DIM:(g + 1) * HEAD_DIM] = o[g * tq:(g + 1) * tq, :].astype(BF16)


def _attn_a_kernel(q_ref, k_ref, v_ref, o_ref, s_scr, mx_scr, m_scr, acc_scr, *,
                   ctx_len, ctx_blocks, lat_chunks, unroll):
    i = pl.program_id(1)
    tq = q_ref.shape[1]
    rows = GROUP * tq
    q = q_ref[...].reshape(rows, HEAD_DIM)

    def scores(slot, off, width):
        s = _nt_dot(q, k_ref[0, pl.ds(off, width), :])
        s_scr[slot, :, 0:width] = s
        mx_scr[slot] = jnp.broadcast_to(jnp.max(s, axis=-1, keepdims=True), (rows, 128))

    def update(slot, off, width):
        m_old = m_scr[...]
        m_new = jnp.maximum(m_old, mx_scr[slot])
        alpha = jnp.exp(m_old - m_new)
        p = jnp.exp(s_scr[slot, :, 0:width] - jnp.concatenate([m_new] * (width // 128), axis=1))
        acc_scr[...] = alpha * acc_scr[...] + jnp.dot(
            p.astype(BF16), v_ref[0, pl.ds(off, width), :], preferred_element_type=F32)
        m_scr[...] = m_new

    def chunk_off(c):
        return pl.multiple_of(ctx_len + c * KEY_CHUNK, 128)

    m_scr[...] = jnp.full(m_scr.shape, -jnp.inf, F32)
    acc_scr[...] = jnp.zeros(acc_scr.shape, F32)
    scores(0, 0, ctx_len)
    update(0, 0, ctx_len)

    @pl.when(i >= ctx_blocks)
    def _():
        scores(0, chunk_off(0), KEY_CHUNK)

        def body(it, carry):
            for u in range(0, unroll, 2):
                c = unroll * it + u
                scores(1, chunk_off(c + 1), KEY_CHUNK)
                update(0, chunk_off(c), KEY_CHUNK)
                scores(0, chunk_off(jnp.minimum(c + 2, lat_chunks - 1)), KEY_CHUNK)
                update(1, chunk_off(c + 1), KEY_CHUNK)
            return carry

        lax.fori_loop(0, lat_chunks // unroll, body, 0)

    acc = acc_scr[...]
    _store_heads(o_ref, acc[:, 0:HEAD_DIM] / acc[:, HEAD_DIM:HEAD_DIM + 1], tq)


def _attn_a_call(q, k, v, ctx_len):
    n = q.shape[1]
    tq = ROW_BLOCK
    lat_chunks = (n - ctx_len) // KEY_CHUNK
    assert lat_chunks % 2 == 0
    unroll = 4 if lat_chunks % 4 == 0 else 2
    kern = functools.partial(_attn_a_kernel, ctx_len=ctx_len, ctx_blocks=ctx_len // tq, lat_chunks=lat_chunks,
                             unroll=unroll)
    return pl.pallas_call(
        kern,
        grid=(N_KV, n // tq),
        in_specs=[
            pl.BlockSpec((GROUP, tq, HEAD_DIM), lambda j, i: (j, i, 0)),
            pl.BlockSpec((1, n, HEAD_DIM), lambda j, i: (j, 0, 0)),
            pl.BlockSpec((1, n, 2 * HEAD_DIM), lambda j, i: (j, 0, 0)),
        ],
        out_specs=pl.BlockSpec((tq, GROUP * HEAD_DIM), lambda j, i: (i, j)),
        out_shape=jax.ShapeDtypeStruct((n, N_HEADS * HEAD_DIM), BF16),
        scratch_shapes=[
            pltpu.VMEM((2, GROUP * tq, KEY_CHUNK), F32),
            pltpu.VMEM((2, GROUP * tq, 128), F32),
            pltpu.VMEM((GROUP * tq, 128), F32),
            pltpu.VMEM((GROUP * tq, 2 * HEAD_DIM), F32),
        ],
        compiler_params=_cparams(("arbitrary", "arbitrary")),
        name="attn_global",
    )(q, k, v)


def _attn_c_kernel(sink_ref, q_ref, k_ref, v_ref, o_ref, *, ctx_len, ctx_blocks, s_len):
    j = pl.program_id(0)
    i = pl.program_id(1)
    tq = q_ref.shape[1]
    rows = GROUP * tq
    n_loc = tq + 2 * WINDOW
    q = q_ref[...].reshape(rows, HEAD_DIM)

    p0 = i * tq - ctx_len
    ws = pl.multiple_of(jnp.clip(p0 - WINDOW, 0, s_len - n_loc), 128)
    k_loc = k_ref[0, pl.ds(ctx_len + ws, n_loc), :]
    v_loc = v_ref[0, pl.ds(ctx_len + ws, n_loc), :]
    k_ctx = k_ref[0, 0:ctx_len, :]
    v_ctx = v_ref[0, 0:ctx_len, :]

    dist = (lax.broadcasted_iota(jnp.int32, (tq, n_loc), 0) - lax.broadcasted_iota(jnp.int32, (tq, n_loc), 1)
            + (p0 - ws))
    valid = (jnp.abs(dist) <= WINDOW) & (i >= ctx_blocks)
    s_loc = jnp.where(valid[None], _nt_dot(q, k_loc).reshape(GROUP, tq, n_loc), NEG_INF).reshape(rows, n_loc)
    s_ctx = _nt_dot(q, k_ctx)

    grp = lax.broadcasted_iota(jnp.int32, (rows, 1), 0) // tq
    sink = jnp.full((rows, 1), sink_ref[j * GROUP + GROUP - 1], F32)
    for g in range(GROUP - 1):
        sink = jnp.where(grp == g, sink_ref[j * GROUP + g], sink)

    m = jnp.maximum(jnp.maximum(jnp.max(s_loc, axis=-1, keepdims=True),
                                jnp.max(s_ctx, axis=-1, keepdims=True)), sink)
    p_loc = jnp.exp(s_loc - m)
    p_ctx = jnp.exp(s_ctx - m)
    l = (jnp.sum(p_loc, axis=-1, keepdims=True) + jnp.sum(p_ctx, axis=-1, keepdims=True)
         + jnp.exp(sink - m))
    o = (jnp.dot(p_loc.astype(BF16), v_loc, preferred_element_type=F32)
         + jnp.dot(p_ctx.astype(BF16), v_ctx, preferred_element_type=F32)) / l
    _store_heads(o_ref, o, tq)


def _attn_c_call(sink, q, k, v, ctx_len):
    n = q.shape[1]
    tq = ROW_BLOCK
    kern = functools.partial(_attn_c_kernel, ctx_len=ctx_len, ctx_blocks=ctx_len // tq, s_len=n - ctx_len)
    return pl.pallas_call(
        kern,
        grid=(N_KV, n // tq),
        in_specs=[
            pl.BlockSpec(memory_space=pltpu.SMEM),
            pl.BlockSpec((GROUP, tq, HEAD_DIM), lambda j, i: (j, i, 0)),
            pl.BlockSpec((1, n, HEAD_DIM), lambda j, i: (j, 0, 0)),
            pl.BlockSpec((1, n, HEAD_DIM), lambda j, i: (j, 0, 0)),
        ],
        out_specs=pl.BlockSpec((tq, GROUP * HEAD_DIM), lambda j, i: (i, j)),
        out_shape=jax.ShapeDtypeStruct((n, N_HEADS * HEAD_DIM), BF16),
        compiler_params=_cparams(("arbitrary", "arbitrary")),
        name="attn_window",
    )(sink, q, k, v)


def _merge_kernel(x_ref, mod_ref, ya_ref, yc_ref, xb_ref, gb_ref, gc_ref,
                  xbp_ref, gcp_ref, xbn_ref, gcn_ref, gl_ref, cw_ref, wb_ref, wo_ref,
                  lg_ref, lb_ref, wpq_ref, x1_ref, hq_ref, qp_ref, *, n_blocks, ctx_blocks):
    i = pl.program_id(0)
    tb = x_ref.shape[0]

    z = gc_ref[...] * xb_ref[...]
    z_before = gcp_ref[7:8, :] * xbp_ref[7:8, :]
    z_after = gcn_ref[0:1, :] * xbn_ref[0:1, :]
    seq_start = (i == 0) | (i == ctx_blocks)
    seq_end = (i == ctx_blocks - 1) | (i == n_blocks - 1)
    z_before = jnp.where(seq_start, 0.0, z_before)
    z_after = jnp.where(seq_end, 0.0, z_after)
    rid = lax.broadcasted_iota(jnp.int32, z.shape, 0)
    z_prev = jnp.where(rid == 0, z_before, pltpu.roll(z, 1, 0))
    z_next = jnp.where(rid == tb - 1, z_after, pltpu.roll(z, tb - 1, 0))
    cw = cw_ref[...]
    yb = gb_ref[...] * (cw[0:1, :] * z_prev + cw[1:2, :] * z + cw[2:3, :] * z_next)

    pa = jnp.dot(ya_ref[...], wb_ref[0], preferred_element_type=F32)
    pb = jnp.dot(yb.astype(BF16), wb_ref[1], preferred_element_type=F32)
    pc = jnp.dot(yc_ref[...], wb_ref[2], preferred_element_type=F32)
    gates = jax.nn.sigmoid(gl_ref[...])
    m = (gates[:, 0:D_MODEL] * pa + gates[:, D_MODEL:2 * D_MODEL] * pb
         + gates[:, 2 * D_MODEL:3 * D_MODEL] * pc)
    y = jnp.dot(m.astype(BF16), wo_ref[...], preferred_element_type=F32)

    g1 = mod_ref[0, 2:3, :]
    sh2 = mod_ref[0, 3:4, :]
    sc2 = mod_ref[0, 4:5, :]
    x1 = _layer_norm(DEEPNORM_ALPHA * x_ref[...] + g1 * y, lg_ref[...], lb_ref[...])
    x1_ref[...] = x1
    hq = (x1 * (1.0 + sc2) + sh2).astype(BF16)
    hq_ref[...] = hq
    qp_ref[...] = jnp.dot(hq, wpq_ref[...], preferred_element_type=F32)


def _merge_call(x_all, mods, ya, yc, proj, conv_w, w_branch, w_out, ln_g, ln_b, w_pq, ctx_blocks):
    n = x_all.shape[0]
    tb = ROW_BLOCK
    nb = n // tb
    halo = tb // 8
    last8 = n // 8 - 1
    kern = functools.partial(_merge_kernel, n_blocks=nb, ctx_blocks=ctx_blocks)
    qcols = PEER_HEADS * PEER_QDIM

    def full(shape):
        return pl.BlockSpec(shape, lambda i: (0,) * len(shape))

    def cols512(c):
        return pl.BlockSpec((tb, B_WIDTH), lambda i: (i, c))

    def before(c):
        return pl.BlockSpec((8, B_WIDTH), lambda i: (jnp.maximum(i * halo - 1, 0), c))

    def after(c):
        return pl.BlockSpec((8, B_WIDTH), lambda i: (jnp.minimum((i + 1) * halo, last8), c))

    return pl.pallas_call(
        kern,
        grid=(nb,),
        in_specs=[
            pl.BlockSpec((tb, D_MODEL), lambda i: (i, 0)),
            pl.BlockSpec((1, N_MOD, D_MODEL), lambda i: (jnp.where(i < ctx_blocks, 0, 1), 0, 0)),
            pl.BlockSpec((tb, N_HEADS * HEAD_DIM), lambda i: (i, 0)),
            pl.BlockSpec((tb, N_HEADS * HEAD_DIM), lambda i: (i, 0)),
            cols512(3), cols512(4), cols512(5),
            before(3), before(5), after(3), after(5),
            pl.BlockSpec((tb, 3 * D_MODEL), lambda i: (i, 1)),
            full((CONV_W, B_WIDTH)),
            full((3, B_WIDTH, D_MODEL)),
            full((D_MODEL, D_MODEL)),
            full((1, D_MODEL)), full((1, D_MODEL)),
            full((D_MODEL, qcols)),
        ],
        out_specs=[
            pl.BlockSpec((tb, D_MODEL), lambda i: (i, 0)),
            pl.BlockSpec((tb, D_MODEL), lambda i: (i, 0)),
            pl.BlockSpec((tb, qcols), lambda i: (i, 0)),
        ],
        out_shape=[
            jax.ShapeDtypeStruct((n, D_MODEL), F32),
            jax.ShapeDtypeStruct((n, D_MODEL), BF16),
            jax.ShapeDtypeStruct((n, qcols), F32),
        ],
        compiler_params=_cparams(("arbitrary",)),
        name="merge",
    )(x_all, mods, ya, yc, proj, proj, proj, proj, proj, proj, proj, proj,
      conv_w, w_branch, w_out, ln_g, ln_b, w_pq)


def _extract_rounds(chains, rounds, on_round, tie_break):
    rows = chains[0].shape[0]
    ridx = lax.broadcasted_iota(jnp.int32, chains[0].shape, 0)
    chains = list(chains)
    ranks = [jnp.full(c.shape, rounds, jnp.int32) for c in chains]
    for r in range(rounds):
        for i, s in enumerate(chains):
            m = jnp.max(s, axis=0, keepdims=True)
            if tie_break:
                first = jnp.min(jnp.where(s == m, ridx, rows), axis=0, keepdims=True)
                hit = ridx == first
            else:
                hit = s == m
            ranks[i] = jnp.where(hit, r, ranks[i])
            chains[i] = jnp.where(hit, -jnp.inf, s)
            on_round(i, r, m)
    return ranks


def _extract_topk(chains, rounds, on_round, rank_ref):
    ranks = _extract_rounds(chains, rounds, on_round, tie_break=False)
    most = jnp.float32(0.0)
    for i, rank in enumerate(ranks):
        rank_ref[i] = rank
        most = jnp.maximum(most, jnp.max(jnp.sum((rank < rounds).astype(F32), axis=0, keepdims=True)))

    @pl.when(most > rounds)
    def _():
        for i, rank in enumerate(_extract_rounds(chains, rounds, on_round, tie_break=True)):
            rank_ref[i] = rank


def _topk_kernel(qp_ref, sk_ref, r2_o, w2_o, k1_o, w1_o, tv_scr, cand_scr, kj_scr, rank_scr, crank_scr):
    tt = qp_ref.shape[0]
    half = PEER_QDIM // 2
    q = qp_ref[...].astype(BF16)
    heads = range(TOPK_HEADS)

    scores = [_nt_dot(sk_ref[h, p], q[:, (2 * h + p) * half:(2 * h + p + 1) * half])
              for h in heads for p in range(2)]

    def keep(i, r, m):
        tv_scr[i, r:r + 1, :] = m

    _extract_topk(scores, PEER_TOPK, keep, rank_scr)

    for h in heads:
        tv1 = tv_scr[2 * h]
        tv2 = tv_scr[2 * h + 1]
        off = 0
        for j, cnt in enumerate(CAND_COUNTS):
            cand_scr[h, off:off + cnt, :] = tv1[j:j + 1, :] + tv2[0:cnt, :]
            off += cnt
        cand_scr[h, N_CAND:CAND_ROWS, :] = jnp.full((CAND_ROWS - N_CAND, tt), -jnp.inf, F32)
    cands = [cand_scr[h] for h in heads]
    _extract_topk(cands, PEER_TOPK, lambda i, r, m: None, crank_scr)

    def twice_bf16(x):
        hi = lax.bitcast_convert_type(x.astype(BF16).astype(F32), jnp.uint32)
        return lax.bitcast_convert_type(hi | (hi >> 16), F32)

    for h in heads:
        cand = cands[h]
        chosen = crank_scr[h] < PEER_TOPK
        cmax = cand[0:1, :]
        z = jnp.sum(jnp.where(chosen, jnp.exp(cand - cmax), 0.0), axis=0, keepdims=True)
        cnt_f = chosen.astype(F32)
        off = 0
        for j, cnt in enumerate(CAND_COUNTS):
            kj_scr[h, j:j + 1, :] = jnp.sum(cnt_f[off:off + cnt, :], axis=0, keepdims=True)
            off += cnt

        rank1 = rank_scr[2 * h]
        rank2 = rank_scr[2 * h + 1]
        k1 = jnp.zeros((N_KEYS, tt), F32)
        for j in range(PEER_TOPK):
            k1 = jnp.where(rank1 == j, kj_scr[h, j:j + 1, :], k1)
        k1_o[h] = twice_bf16(k1)
        w1_o[h] = twice_bf16(
            jnp.where(rank1 < PEER_TOPK, jnp.exp(scores[2 * h] - tv_scr[2 * h, 0:1, :]), 0.0) / z)
        r2_o[h] = rank2.astype(F32).astype(BF16)
        w2_o[h] = jnp.where(rank2 < PEER_TOPK, jnp.exp(scores[2 * h + 1] - tv_scr[2 * h + 1, 0:1, :]),
                            0.0).astype(BF16)


def _topk_call(qp, sub_keys):
    n = qp.shape[0]
    tt = ROW_BLOCK
    hp = TOPK_HEADS
    out = pl.BlockSpec((hp, N_KEYS, tt), lambda t, h: (h, 0, t))
    shape = jax.ShapeDtypeStruct((PEER_HEADS, N_KEYS, n), F32)
    shape_bf16 = jax.ShapeDtypeStruct((PEER_HEADS, N_KEYS, n), BF16)
    return pl.pallas_call(
        _topk_kernel,
        grid=(n // tt, PEER_HEADS // hp),
        in_specs=[
            pl.BlockSpec((tt, hp * PEER_QDIM), lambda t, h: (t, h)),
            pl.BlockSpec((hp, 2, N_KEYS, PEER_QDIM // 2), lambda t, h: (h, 0, 0, 0)),
        ],
        out_specs=[out, out, out, out],
        out_shape=[shape_bf16, shape_bf16, shape, shape],
        scratch_shapes=[
            pltpu.VMEM((2 * hp, PEER_TOPK, tt), F32),
            pltpu.VMEM((hp, CAND_ROWS, tt), F32),
            pltpu.VMEM((hp, PEER_TOPK, tt), F32),
            pltpu.VMEM((2 * hp, N_KEYS, tt), jnp.int32),
            pltpu.VMEM((hp, CAND_ROWS, tt), jnp.int32),
        ],
        compiler_params=_cparams(("arbitrary", "arbitrary")),
        name="peer_topk",
    )(qp, sub_keys)


_ERF_A = (-2.72614225801306e-10, 2.77068142495902e-08, -2.10102402082508e-06, -5.69250639462346e-05,
          -7.34990630326855e-04, -2.95459980854025e-03, -1.60960333262415e-02)
_ERF_B = (-1.45660718464996e-05, -2.13374055278905e-04, -1.68282697438203e-03, -7.37332916720468e-03,
          -1.42647390514189e-02)


def _gelu(x):
    dt = x.dtype
    lim = 4.0 * math.sqrt(2.0)
    xc = jnp.clip(x, -lim, lim)
    t = xc * xc
    deg_a = len(_ERF_A) - 1
    deg_b = len(_ERF_B) - 1
    a2 = [c * (0.5 / math.sqrt(2.0)) / 2.0 ** (deg_a - i) for i, c in enumerate(_ERF_A)]
    b2 = [c / 2.0 ** (deg_b - i) for i, c in enumerate(_ERF_B)]
    a = jnp.asarray(a2[0], dt)
    for c in a2[1:]:
        a = a * t + jnp.asarray(c, dt)
    b = jnp.asarray(b2[0], dt)
    for c in b2[1:]:
        b = b * t + jnp.asarray(c, dt)
    return 0.5 * x + (x * xc) * (a / b)


def _dense_kernel(hq_ref, u_ref, vt_ref, r2_ref, w2_ref, k1_ref, w1_ref, o_ref, h_scr, a_scr, *, n_eb):
    s = pl.program_id(0)
    eb = u_ref.shape[0]
    tt = hq_ref.shape[0]
    n_i1 = eb // N_KEYS

    @pl.when(s == 0)
    def _():
        h_scr[...] = jnp.zeros(h_scr.shape, BF16)
        a_scr[...] = jnp.zeros(a_scr.shape, BF16)

    @pl.when(jnp.maximum(s - 2, 0) % n_eb == 0)
    def _():
        o_ref[...] = jnp.zeros(o_ref.shape, F32)

    def rows16(ref, h, ii, cs):
        row = ref[h, ii:ii + 1, cs]
        return pltpu.bitcast(jnp.broadcast_to(row, (8, PEER_LANES)), BF16)

    for c in range(tt // PEER_LANES):
        cs = slice(c * PEER_LANES, (c + 1) * PEER_LANES)

        o_ref[:, cs] += jnp.dot(vt_ref[...], a_scr[:, cs], preferred_element_type=F32)

        for ii in range(n_i1):
            rows = slice(ii * N_KEYS, (ii + 1) * N_KEYS)
            g = jnp.zeros((N_KEYS, PEER_LANES), BF16)
            for h in range(PEER_HEADS):
                k1b = jnp.concatenate([rows16(k1_ref, h, ii, cs)] * (N_KEYS // 16), axis=0)
                w1b = jnp.concatenate([rows16(w1_ref, h, ii, cs)] * (N_KEYS // 16), axis=0)
                keep = r2_ref[h, :, cs] < k1b
                g = g + jnp.where(keep, w2_ref[h, :, cs] * w1b, 0.0)
            a_scr[rows, cs] = _gelu(h_scr[rows, cs]) * g

        h_scr[:, cs] = _nt_dot(u_ref[...], hq_ref[cs, :]).astype(BF16)


def _dense_call(hq, u_tab, vt_tab, r2, w2, k1, w1):
    n = hq.shape[0]
    n_exp = u_tab.shape[0]
    tt = PEER_TOKENS
    eb = PEER_EXPERTS
    n_eb = n_exp // eb
    steps = (n // tt) * n_eb

    def blk(s, lag):
        return jnp.clip(s - lag, 0, steps - 1)

    tab = pl.BlockSpec((PEER_HEADS, N_KEYS, tt), lambda s: (0, 0, blk(s, 1) // n_eb))
    row = pl.BlockSpec((PEER_HEADS, eb // N_KEYS, tt), lambda s: (0, blk(s, 1) % n_eb, blk(s, 1) // n_eb))
    return pl.pallas_call(
        functools.partial(_dense_kernel, n_eb=n_eb),
        grid=(steps + 2,),
        in_specs=[
            pl.BlockSpec((tt, D_MODEL), lambda s: (blk(s, 0) // n_eb, 0)),
            pl.BlockSpec((eb, D_MODEL), lambda s: (blk(s, 0) % n_eb, 0)),
            pl.BlockSpec((D_MODEL, eb), lambda s: (0, blk(s, 2) % n_eb)),
            tab, tab, row, row,
        ],
        out_specs=pl.BlockSpec((D_MODEL, tt), lambda s: (0, blk(s, 2) // n_eb)),
        out_shape=jax.ShapeDtypeStruct((D_MODEL, n), F32),
        scratch_shapes=[pltpu.VMEM((eb, tt), BF16), pltpu.VMEM((eb, tt), BF16)],
        compiler_params=_cparams(("arbitrary",)),
        name="peer_dense",
    )(hq, u_tab, vt_tab, r2, w2, k1, w1)


def _ln2_kernel(x1_ref, ft_ref, mod_ref, lg_ref, lb_ref, o_ref):
    g2 = mod_ref[0, 5:6, :]
    f = ft_ref[...].T
    o_ref[...] = _layer_norm(DEEPNORM_ALPHA * x1_ref[...] + g2 * f, lg_ref[...], lb_ref[...])


def _ln2_call(x1, ft, mods, ln_g, ln_b, ctx_blocks):
    n = x1.shape[0]
    tb = ROW_BLOCK
    return pl.pallas_call(
        _ln2_kernel,
        grid=(n // tb,),
        in_specs=[
            pl.BlockSpec((tb, D_MODEL), lambda i: (i, 0)),
            pl.BlockSpec((D_MODEL, tb), lambda i: (0, i)),
            pl.BlockSpec((1, N_MOD, D_MODEL), lambda i: (jnp.where(i < ctx_blocks, 0, 1), 0, 0)),
            pl.BlockSpec((1, D_MODEL), lambda i: (0, 0)),
            pl.BlockSpec((1, D_MODEL), lambda i: (0, 0)),
        ],
        out_specs=pl.BlockSpec((tb, D_MODEL), lambda i: (i, 0)),
        out_shape=jax.ShapeDtypeStruct((n, D_MODEL), F32),
        compiler_params=_cparams(("arbitrary",)),
        name="ln2",
    )(x1, ft, mods, ln_g, ln_b)


def _rope_tables(ctx_len, s_len):
    t = np.arange(s_len)
    pos = np.stack([t // GRID_W, t % GRID_W], axis=-1).astype(np.float64)
    quarter = HEAD_DIM // 4
    inv_freq = ROPE_THETA ** (-np.arange(quarter, dtype=np.float64) / quarter)
    ang = pos[:, :, None] * inv_freq
    cos = np.repeat(np.cos(ang), 2, axis=1).reshape(s_len, HEAD_DIM)
    sin = np.repeat(np.sin(ang), 2, axis=1).reshape(s_len, HEAD_DIM)
    first_half = (np.arange(HEAD_DIM) % 32) < 16
    sa = np.where(first_half, -sin, 0.0)
    sb = np.where(first_half, 0.0, sin)

    def full(tab, fill):
        tab = np.concatenate([np.full((ctx_len, HEAD_DIM), fill), tab], axis=0)
        return jnp.asarray(np.tile(tab, (1, 2)), F32)

    return full(cos, 1.0), full(sa, 0.0), full(sb, 0.0)


def kernel(x, c, ctx, c_ctx, w_mod, b_mod, w_in, q_norm, k_norm, conv_w, sink, w_branch, w_out,
           ln1_g, ln1_b, w_pq, sub_keys, u_tab, v_tab, ln2_g, ln2_b):
    bsz, s_len, d = x.shape
    ctx_len = ctx.shape[1]
    depth = w_mod.shape[0]
    assert bsz == 1 and d == D_MODEL and depth == DEPTH
    assert ctx_len % ROW_BLOCK == 0 and s_len % KEY_CHUNK == 0
    assert (ctx_len + s_len) % PEER_TOKENS == 0 and (ctx_len + s_len) % INPROJ_ROWS == 0
    ctx_blocks = ctx_len // ROW_BLOCK

    x_all = jnp.concatenate([ctx[0], x[0]], axis=0)
    cos_t, sa_t, sb_t = _rope_tables(ctx_len, s_len)

    cc = jnp.zeros((8, D_MODEL), F32).at[0].set(c_ctx).at[1].set(c[0])
    mod_all = _mod_call(cc, w_mod.astype(BF16), b_mod[:, None, :])

    for layer in range(depth):
        mods = mod_all[layer, 0:2].reshape(2, N_MOD, D_MODEL)
        proj = _inproj_call(x_all, mods, w_in[layer].astype(BF16), ctx_len)
        qg = jnp.tile(q_norm[layer], 2)[None, :]
        kg = jnp.tile(k_norm[layer], 2)[None, :]
        qa, ka, va, qc, kc, vc = _prep_call(proj, cos_t, sa_t, sb_t, qg, kg)
        ya = _attn_a_call(qa, ka, va, ctx_len)
        yc = _attn_c_call(sink[layer], qc, kc, vc, ctx_len)
        x1, hq, qp = _merge_call(x_all, mods, ya, yc, proj, conv_w[layer],
                                 w_branch[layer].astype(BF16), w_out[layer].astype(BF16),
                                 ln1_g[layer][None, :], ln1_b[layer][None, :],
                                 w_pq[layer].astype(BF16), ctx_blocks)
        r2, w2, k1, w1 = _topk_call(qp, sub_keys[layer].astype(BF16))
        ft = _dense_call(hq, u_tab[layer].astype(BF16), v_tab[layer].T.astype(BF16), r2, w2, k1, w1)
        x_all = _ln2_call(x1, ft, mods, ln2_g[layer][None, :], ln2_b[layer][None, :], ctx_blocks)

    return x_all[ctx_len:][None]
```

```python
import functools
import math

import jax
import jax.numpy as jnp
import numpy as np
from jax import lax
from jax.experimental import pallas as pl
from jax.experimental.pallas import tpu as pltpu

F32 = jnp.float32
BF16 = jnp.bfloat16

D_MODEL = 1024
HEAD_DIM = 64
N_HEADS = 8
N_KV = 2
GROUP = N_HEADS // N_KV
GRID_W = 64
WINDOW = 128
CONV_W = 3
B_WIDTH = 512
N_MOD = 6
ROPE_THETA = 10000.0
PEER_HEADS = 8
N_KEYS = 128
PEER_TOPK = 16
PEER_QDIM = 256
DEPTH = 2
IN_COLS = 6144
DEEPNORM_ALPHA = (2 * DEPTH) ** 0.25
LN_EPS = 1e-6
RMS_EPS = 1e-6
NEG_INF = -1e30
ATTN_SCALE = HEAD_DIM ** -0.5

ROW_BLOCK = 256
INPROJ_ROWS = 1280
KEY_CHUNK = 1024
BOUND_SLACK = 1.0 + 2.0 ** -6
SAFE_ROW_SUM = 1e-30
PEER_TOKENS = 1280
PEER_LANES = 256
TOPK_HEADS = 2
PEER_EXPERTS = 1024
VMEM_LIMIT = 56 * 1024 * 1024

CAND_COUNTS = tuple(PEER_TOPK // (j + 1) for j in range(PEER_TOPK))
N_CAND = sum(CAND_COUNTS)
CAND_ROWS = 56


def _cparams(sem):
    return pltpu.CompilerParams(dimension_semantics=sem, vmem_limit_bytes=VMEM_LIMIT)


def _nt_dot(a, b):
    return lax.dot_general(a, b, (((1,), (1,)), ((), ())), preferred_element_type=F32)


def _layer_norm(r, g, b):
    mu = jnp.mean(r, axis=-1, keepdims=True)
    rc = r - mu
    var = jnp.mean(rc * rc, axis=-1, keepdims=True)
    return rc * lax.rsqrt(var + LN_EPS) * g + b


def _mod_kernel(c_ref, w_ref, b_ref, o_ref):
    c = c_ref[...]
    h = (c * jax.nn.sigmoid(c)).astype(BF16)
    o_ref[0] = jnp.dot(h, w_ref[0], preferred_element_type=F32) + b_ref[0]


def _mod_call(cc, w_mod, b_mod):
    depth = w_mod.shape[0]
    cols = w_mod.shape[2]
    tn = 1024
    return pl.pallas_call(
        _mod_kernel,
        grid=(depth, cols // tn),
        in_specs=[
            pl.BlockSpec((8, D_MODEL), lambda l, j: (0, 0)),
            pl.BlockSpec((1, D_MODEL, tn), lambda l, j: (l, 0, j)),
            pl.BlockSpec((1, 1, tn), lambda l, j: (l, 0, j)),
        ],
        out_specs=pl.BlockSpec((1, 8, tn), lambda l, j: (l, 0, j)),
        out_shape=jax.ShapeDtypeStruct((depth, 8, cols), F32),
        compiler_params=_cparams(("arbitrary", "arbitrary")),
        name="mod",
    )(cc, w_mod, b_mod)


def _inproj_kernel(x_ref, mod_ref, w_ref, o_ref, *, ctx_len):
    tb = x_ref.shape[0]
    row = pl.program_id(1) * tb + lax.broadcasted_iota(jnp.int32, (tb, 1), 0)
    is_ctx = row < ctx_len
    sh = jnp.where(is_ctx, mod_ref[0, 0:1, :], mod_ref[1, 0:1, :])
    sc = jnp.where(is_ctx, mod_ref[0, 1:2, :], mod_ref[1, 1:2, :])
    hm = (x_ref[...] * (1.0 + sc) + sh).astype(BF16)
    o_ref[...] = jnp.dot(hm, w_ref[...], preferred_element_type=F32)


def _inproj_call(x_all, mods, w_in, ctx_len):
    n = x_all.shape[0]
    tn = 1024
    tb = INPROJ_ROWS
    return pl.pallas_call(
        functools.partial(_inproj_kernel, ctx_len=ctx_len),
        grid=(IN_COLS // tn, n // tb),
        in_specs=[
            pl.BlockSpec((tb, D_MODEL), lambda j, i: (i, 0)),
            pl.BlockSpec((2, N_MOD, D_MODEL), lambda j, i: (0, 0, 0)),
            pl.BlockSpec((D_MODEL, tn), lambda j, i: (0, j)),
        ],
        out_specs=pl.BlockSpec((tb, tn), lambda j, i: (i, j)),
        out_shape=jax.ShapeDtypeStruct((n, IN_COLS), F32),
        compiler_params=_cparams(("arbitrary", "arbitrary")),
        name="inproj",
    )(x_all, mods, w_in)


def _prep_kernel(p_ref, cos_ref, sa_ref, sb_ref, qg_ref, kg_ref,
                 qa_o, ka_o, va_o, qc_o, kc_o, vc_o):
    x = p_ref[...]
    cos = cos_ref[...]
    sa = sa_ref[...]
    sb = sb_ref[...]

    def tile(t, width):
        reps = width // 128
        return t if reps == 1 else jnp.concatenate([t] * reps, axis=1)

    def rope(y):
        w = y.shape[1]
        return (y * tile(cos, w)
                + pltpu.roll(y, w - 16, 1) * tile(sa, w)
                + pltpu.roll(y, 16, 1) * tile(sb, w))

    def inv_rms(xh):
        return lax.rsqrt(jnp.mean(xh * xh, axis=-1, keepdims=True) + RMS_EPS)

    qa = x[:, 0:512]
    ka = x[:, 512:640]
    va = x[:, 640:768]
    qc = x[:, 768:1280]
    kc = x[:, 1280:1408]
    vc = x[:, 1408:1536]

    qa_r = rope(qa * tile(qg_ref[...], 512))
    ka_r = rope(ka * kg_ref[...])
    qc_r = rope(qc)
    kc_r = rope(kc)

    for h in range(N_HEADS):
        sl = slice(h * HEAD_DIM, (h + 1) * HEAD_DIM)
        qa_o[h, :, 0:HEAD_DIM] = (qa_r[:, sl] * (inv_rms(qa[:, sl]) * ATTN_SCALE)).astype(BF16)
        qa_o[h, :, HEAD_DIM:2 * HEAD_DIM] = jnp.zeros((x.shape[0], HEAD_DIM), BF16)
        qc_o[h] = (qc_r[:, sl] * ATTN_SCALE).astype(BF16)
    lane = lax.broadcasted_iota(jnp.int32, (x.shape[0], HEAD_DIM), 1)
    ones_col = jnp.where(lane == 0, 1.0, 0.0).astype(BF16)
    for h in range(N_KV):
        sl = slice(h * HEAD_DIM, (h + 1) * HEAD_DIM)
        ka_o[h, :, 0:HEAD_DIM] = (ka_r[:, sl] * inv_rms(ka[:, sl])).astype(BF16)
        ka_o[h, :, HEAD_DIM:2 * HEAD_DIM] = ones_col
        va_o[h, :, 0:HEAD_DIM] = va[:, sl].astype(BF16)
        va_o[h, :, HEAD_DIM:2 * HEAD_DIM] = ones_col
        kc_o[h] = kc_r[:, sl].astype(BF16)
        vc_o[h, :, 0:HEAD_DIM] = vc[:, sl].astype(BF16)
        vc_o[h, :, HEAD_DIM:2 * HEAD_DIM] = ones_col


def _prep_call(proj, cos_t, sa_t, sb_t, qg, kg):
    n = proj.shape[0]
    tb = ROW_BLOCK
    tab = pl.BlockSpec((tb, 128), lambda i: (i, 0))
    gain = pl.BlockSpec((1, 128), lambda i: (0, 0))

    def hm(nh, width=HEAD_DIM):
        return pl.BlockSpec((nh, tb, width), lambda i: (0, i, 0))

    def hshape(nh, width=HEAD_DIM):
        return jax.ShapeDtypeStruct((nh, n, width), BF16)

    return pl.pallas_call(
        _prep_kernel,
        grid=(n // tb,),
        in_specs=[pl.BlockSpec((tb, 1536), lambda i: (i, 0)), tab, tab, tab, gain, gain],
        out_specs=[hm(N_HEADS, 2 * HEAD_DIM), hm(N_KV, 2 * HEAD_DIM), hm(N_KV, 2 * HEAD_DIM),
                   hm(N_HEADS), hm(N_KV), hm(N_KV, 2 * HEAD_DIM)],
        out_shape=[hshape(N_HEADS, 2 * HEAD_DIM), hshape(N_KV, 2 * HEAD_DIM), hshape(N_KV, 2 * HEAD_DIM),
                   hshape(N_HEADS), hshape(N_KV), hshape(N_KV, 2 * HEAD_DIM)],
        compiler_params=_cparams(("arbitrary",)),
        name="prep",
    )(proj, cos_t, sa_t, sb_t, qg, kg)


def _store_heads(o_ref, o, tq):
    for g in range(GROUP):
        o_ref[:, g * HEAD_DIM:(g + 1) * HEAD_DIM] = o[g * tq:(g + 1) * tq, :].astype(BF16)


def _attn_a_kernel(q_ref, k_ref, v_ref, o_ref, s_scr, mx_scr, m_scr, acc_scr, kmax_scr, *,
                   ctx_len, ctx_blocks, lat_chunks, unroll):
    i = pl.program_id(1)
    tq = q_ref.shape[1]
    rows = GROUP * tq
    key_chunk = s_scr.shape[2]
    q0 = q_ref[...].reshape(rows, 2 * HEAD_DIM)

    @pl.when(i == 0)
    def _():
        kk = k_ref[0, :, 0:HEAD_DIM].astype(F32)
        k_norm2 = jnp.max(jnp.sum(kk * kk, axis=-1, keepdims=True), axis=0, keepdims=True)
        kmax_scr[...] = jnp.broadcast_to(jnp.sqrt(k_norm2), kmax_scr.shape)

    def chunk_off(c):
        return pl.multiple_of(ctx_len + c * key_chunk, 128)

    def run(q, online):
        def scores(slot, off, width):
            s = _nt_dot(q, k_ref[0, pl.ds(off, width), :])
            s_scr[slot, :, 0:width] = s
            if online:
                mx_scr[slot] = jnp.broadcast_to(jnp.max(s, axis=-1, keepdims=True), (rows, 128))

        def update(slot, off, width):
            s = s_scr[slot, :, 0:width]
            v = v_ref[0, pl.ds(off, width), :]
            if online:
                m_old = m_scr[...]
                m_new = jnp.maximum(m_old, mx_scr[slot])
                p = jnp.exp(s - jnp.concatenate([m_new] * (width // 128), axis=1))
                acc_scr[...] = jnp.exp(m_old - m_new) * acc_scr[...] + jnp.dot(
                    p.astype(BF16), v, preferred_element_type=F32)
                m_scr[...] = m_new
            else:
                acc_scr[...] += jnp.dot(jnp.exp(s).astype(BF16), v, preferred_element_type=F32)

        if online:
            m_scr[...] = jnp.full(m_scr.shape, -jnp.inf, F32)
        acc_scr[...] = jnp.zeros(acc_scr.shape, F32)
        scores(0, 0, ctx_len)
        update(0, 0, ctx_len)

        @pl.when(i >= ctx_blocks)
        def _():
            scores(0, chunk_off(0), key_chunk)

            def body(it, carry):
                for u in range(0, unroll, 2):
                    c = unroll * it + u
                    scores(1, chunk_off(c + 1), key_chunk)
                    update(0, chunk_off(c), key_chunk)
                    scores(0, chunk_off(jnp.minimum(c + 2, lat_chunks - 1)), key_chunk)
                    update(1, chunk_off(c + 1), key_chunk)
                return carry

            lax.fori_loop(0, lat_chunks // unroll, body, 0)

        acc = acc_scr[...]
        _store_heads(o_ref, acc[:, 0:HEAD_DIM] / acc[:, HEAD_DIM:HEAD_DIM + 1], tq)

    qf = q0.astype(F32)
    q_norm = jnp.sqrt(jnp.sum(qf * qf, axis=-1, keepdims=True))
    bound = (q_norm * kmax_scr[0:1, 0:1] * BOUND_SLACK).astype(BF16).astype(F32)
    shift_lane = lax.broadcasted_iota(jnp.int32, qf.shape, 1) == HEAD_DIM
    run(jnp.where(shift_lane, -bound, qf).astype(BF16), online=False)

    @pl.when(jnp.logical_not(jnp.min(acc_scr[:, HEAD_DIM:HEAD_DIM + 1]) > SAFE_ROW_SUM))
    def _():
        run(q0, online=True)


def _attn_a_call(q, k, v, ctx_len):
    n = q.shape[1]
    tq = ROW_BLOCK
    s_len = n - ctx_len
    key_chunk = KEY_CHUNK if s_len % (2 * KEY_CHUNK) == 0 else KEY_CHUNK // 2
    lat_chunks = s_len // key_chunk
    assert s_len % key_chunk == 0 and lat_chunks % 2 == 0
    unroll = 4 if lat_chunks % 4 == 0 else 2
    kern = functools.partial(_attn_a_kernel, ctx_len=ctx_len, ctx_blocks=ctx_len // tq, lat_chunks=lat_chunks,
                             unroll=unroll)
    return pl.pallas_call(
        kern,
        grid=(N_KV, n // tq),
        in_specs=[
            pl.BlockSpec((GROUP, tq, 2 * HEAD_DIM), lambda j, i: (j, i, 0)),
            pl.BlockSpec((1, n, 2 * HEAD_DIM), lambda j, i: (j, 0, 0)),
            pl.BlockSpec((1, n, 2 * HEAD_DIM), lambda j, i: (j, 0, 0)),
        ],
        out_specs=pl.BlockSpec((tq, GROUP * HEAD_DIM), lambda j, i: (i, j)),
        out_shape=jax.ShapeDtypeStruct((n, N_HEADS * HEAD_DIM), BF16),
        scratch_shapes=[
            pltpu.VMEM((2, GROUP * tq, key_chunk), F32),
            pltpu.VMEM((2, GROUP * tq, 128), F32),
            pltpu.VMEM((GROUP * tq, 128), F32),
            pltpu.VMEM((GROUP * tq, 2 * HEAD_DIM), F32),
            pltpu.VMEM((8, 128), F32),
        ],
        compiler_params=_cparams(("arbitrary", "arbitrary")),
        name="attn_global",
    )(q, k, v)


def _attn_c_kernel(sink_ref, q_ref, k_ref, v_ref, o_ref, *, ctx_len, ctx_blocks, s_len):
    j = pl.program_id(0)
    i = pl.program_id(1)
    tq = q_ref.shape[1]
    rows = GROUP * tq
    n_loc = tq + 2 * WINDOW
    q = q_ref[...].reshape(rows, HEAD_DIM)

    p0 = i * tq - ctx_len
    ws = pl.multiple_of(jnp.clip(p0 - WINDOW, 0, s_len - n_loc), 128)
    k_loc = k_ref[0, pl.ds(ctx_len + ws, n_loc), :]
    v_loc = v_ref[0, pl.ds(ctx_len + ws, n_loc), :]
    k_ctx = k_ref[0, 0:ctx_len, :]
    v_ctx = v_ref[0, 0:ctx_len, :]

    dist = (lax.broadcasted_iota(jnp.int32, (tq, n_loc), 0) - lax.broadcasted_iota(jnp.int32, (tq, n_loc), 1)
            + (p0 - ws))
    valid = (jnp.abs(dist) <= WINDOW) & (i >= ctx_blocks)
    s_loc = jnp.where(valid[None], _nt_dot(q, k_loc).reshape(GROUP, tq, n_loc), NEG_INF).reshape(rows, n_loc)
    s_ctx = _nt_dot(q, k_ctx)

    grp = lax.broadcasted_iota(jnp.int32, (rows, 1), 0) // tq
    sink = jnp.full((rows, 1), sink_ref[j * GROUP + GROUP - 1], F32)
    for g in range(GROUP - 1):
        sink = jnp.where(grp == g, sink_ref[j * GROUP + g], sink)

    m = jnp.maximum(jnp.maximum(jnp.max(s_loc, axis=-1, keepdims=True),
                                jnp.max(s_ctx, axis=-1, keepdims=True)), sink)
    p_loc = jnp.exp(s_loc - m)
    p_ctx = jnp.exp(s_ctx - m)
    acc = (jnp.dot(p_loc.astype(BF16), v_loc, preferred_element_type=F32)
           + jnp.dot(p_ctx.astype(BF16), v_ctx, preferred_element_type=F32))
    o = acc[:, 0:HEAD_DIM] / (acc[:, HEAD_DIM:HEAD_DIM + 1] + jnp.exp(sink - m))
    _store_heads(o_ref, o, tq)


def _attn_c_call(sink, q, k, v, ctx_len):
    n = q.shape[1]
    tq = ROW_BLOCK
    kern = functools.partial(_attn_c_kernel, ctx_len=ctx_len, ctx_blocks=ctx_len // tq, s_len=n - ctx_len)
    return pl.pallas_call(
        kern,
        grid=(N_KV, n // tq),
        in_specs=[
            pl.BlockSpec(memory_space=pltpu.SMEM),
            pl.BlockSpec((GROUP, tq, HEAD_DIM), lambda j, i: (j, i, 0)),
            pl.BlockSpec((1, n, HEAD_DIM), lambda j, i: (j, 0, 0)),
            pl.BlockSpec((1, n, 2 * HEAD_DIM), lambda j, i: (j, 0, 0)),
        ],
        out_specs=pl.BlockSpec((tq, GROUP * HEAD_DIM), lambda j, i: (i, j)),
        out_shape=jax.ShapeDtypeStruct((n, N_HEADS * HEAD_DIM), BF16),
        compiler_params=_cparams(("arbitrary", "arbitrary")),
        name="attn_window",
    )(sink, q, k, v)


def _merge_kernel(x_ref, mod_ref, ya_ref, yc_ref, xb_ref, gb_ref, gc_ref,
                  xbp_ref, gcp_ref, xbn_ref, gcn_ref, gl_ref, cw_ref, wb_ref, wo_ref,
                  lg_ref, lb_ref, wpq_ref, x1_ref, hq_ref, qp_ref, *, n_blocks, ctx_blocks):
    i = pl.program_id(0)
    tb = x_ref.shape[0]

    z = gc_ref[...] * xb_ref[...]
    z_before = gcp_ref[7:8, :] * xbp_ref[7:8, :]
    z_after = gcn_ref[0:1, :] * xbn_ref[0:1, :]
    seq_start = (i == 0) | (i == ctx_blocks)
    seq_end = (i == ctx_blocks - 1) | (i == n_blocks - 1)
    z_before = jnp.where(seq_start, 0.0, z_before)
    z_after = jnp.where(seq_end, 0.0, z_after)
    rid = lax.broadcasted_iota(jnp.int32, z.shape, 0)
    z_prev = jnp.where(rid == 0, z_before, pltpu.roll(z, 1, 0))
    z_next = jnp.where(rid == tb - 1, z_after, pltpu.roll(z, tb - 1, 0))
    cw = cw_ref[...]
    yb = gb_ref[...] * (cw[0:1, :] * z_prev + cw[1:2, :] * z + cw[2:3, :] * z_next)

    pa = jnp.dot(ya_ref[...], wb_ref[0], preferred_element_type=F32)
    pb = jnp.dot(yb.astype(BF16), wb_ref[1], preferred_element_type=F32)
    pc = jnp.dot(yc_ref[...], wb_ref[2], preferred_element_type=F32)
    gates = jax.nn.sigmoid(gl_ref[...])
    m = (gates[:, 0:D_MODEL] * pa + gates[:, D_MODEL:2 * D_MODEL] * pb
         + gates[:, 2 * D_MODEL:3 * D_MODEL] * pc)
    y = jnp.dot(m.astype(BF16), wo_ref[...], preferred_element_type=F32)

    g1 = mod_ref[0, 2:3, :]
    sh2 = mod_ref[0, 3:4, :]
    sc2 = mod_ref[0, 4:5, :]
    x1 = _layer_norm(DEEPNORM_ALPHA * x_ref[...] + g1 * y, lg_ref[...], lb_ref[...])
    x1_ref[...] = x1
    hq = (x1 * (1.0 + sc2) + sh2).astype(BF16)
    hq_ref[...] = hq
    qp_ref[...] = jnp.dot(hq, wpq_ref[...], preferred_element_type=F32)


def _merge_call(x_all, mods, ya, yc, proj, conv_w, w_branch, w_out, ln_g, ln_b, w_pq, ctx_blocks):
    n = x_all.shape[0]
    tb = ROW_BLOCK
    nb = n // tb
    halo = tb // 8
    last8 = n // 8 - 1
    kern = functools.partial(_merge_kernel, n_blocks=nb, ctx_blocks=ctx_blocks)
    qcols = PEER_HEADS * PEER_QDIM

    def full(shape):
        return pl.BlockSpec(shape, lambda i: (0,) * len(shape))

    def cols512(c):
        return pl.BlockSpec((tb, B_WIDTH), lambda i: (i, c))

    def before(c):
        return pl.BlockSpec((8, B_WIDTH), lambda i: (jnp.maximum(i * halo - 1, 0), c))

    def after(c):
        return pl.BlockSpec((8, B_WIDTH), lambda i: (jnp.minimum((i + 1) * halo, last8), c))

    return pl.pallas_call(
        kern,
        grid=(nb,),
        in_specs=[
            pl.BlockSpec((tb, D_MODEL), lambda i: (i, 0)),
            pl.BlockSpec((1, N_MOD, D_MODEL), lambda i: (jnp.where(i < ctx_blocks, 0, 1), 0, 0)),
            pl.BlockSpec((tb, N_HEADS * HEAD_DIM), lambda i: (i, 0)),
            pl.BlockSpec((tb, N_HEADS * HEAD_DIM), lambda i: (i, 0)),
            cols512(3), cols512(4), cols512(5),
            before(3), before(5), after(3), after(5),
            pl.BlockSpec((tb, 3 * D_MODEL), lambda i: (i, 1)),
            full((CONV_W, B_WIDTH)),
            full((3, B_WIDTH, D_MODEL)),
            full((D_MODEL, D_MODEL)),
            full((1, D_MODEL)), full((1, D_MODEL)),
            full((D_MODEL, qcols)),
        ],
        out_specs=[
            pl.BlockSpec((tb, D_MODEL), lambda i: (i, 0)),
            pl.BlockSpec((tb, D_MODEL), lambda i: (i, 0)),
            pl.BlockSpec((tb, qcols), lambda i: (i, 0)),
        ],
        out_shape=[
            jax.ShapeDtypeStruct((n, D_MODEL), F32),
            jax.ShapeDtypeStruct((n, D_MODEL), BF16),
            jax.ShapeDtypeStruct((n, qcols), F32),
        ],
        compiler_params=_cparams(("arbitrary",)),
        name="merge",
    )(x_all, mods, ya, yc, proj, proj, proj, proj, proj, proj, proj, proj,
      conv_w, w_branch, w_out, ln_g, ln_b, w_pq)


def _extract_rounds(chains, rounds, on_round, tie_break):
    rows = chains[0].shape[0]
    ridx = lax.broadcasted_iota(jnp.int32, chains[0].shape, 0)
    chains = list(chains)
    ranks = [jnp.full(c.shape, rounds, jnp.int32) for c in chains]
    for r in range(rounds):
        for i, s in enumerate(chains):
            m = jnp.max(s, axis=0, keepdims=True)
            if tie_break:
                first = jnp.min(jnp.where(s == m, ridx, rows), axis=0, keepdims=True)
                hit = ridx == first
            else:
                hit = s == m
            ranks[i] = jnp.where(hit, r, ranks[i])
            chains[i] = jnp.where(hit, -jnp.inf, s)
            on_round(i, r, m)
    return ranks


def _extract_topk(chains, rounds, on_round, rank_ref):
    ranks = _extract_rounds(chains, rounds, on_round, tie_break=False)
    most = jnp.float32(0.0)
    for i, rank in enumerate(ranks):
        rank_ref[i] = rank
        most = jnp.maximum(most, jnp.max(jnp.sum((rank < rounds).astype(F32), axis=0, keepdims=True)))

    @pl.when(most > rounds)
    def _():
        for i, rank in enumerate(_extract_rounds(chains, rounds, on_round, tie_break=True)):
            rank_ref[i] = rank


def _topk_kernel(qp_ref, sk_ref, r2_o, w2_o, k1_o, w1_o, tv_scr, cand_scr, kj_scr, rank_scr, crank_scr):
    tt = qp_ref.shape[0]
    half = PEER_QDIM // 2
    q = qp_ref[...].astype(BF16)
    heads = range(TOPK_HEADS)

    scores = [_nt_dot(sk_ref[h, p], q[:, (2 * h + p) * half:(2 * h + p + 1) * half])
              for h in heads for p in range(2)]

    def keep(i, r, m):
        tv_scr[i, r:r + 1, :] = m

    _extract_topk(scores, PEER_TOPK, keep, rank_scr)

    for h in heads:
        tv1 = tv_scr[2 * h]
        tv2 = tv_scr[2 * h + 1]
        off = 0
        for j, cnt in enumerate(CAND_COUNTS):
            cand_scr[h, off:off + cnt, :] = tv1[j:j + 1, :] + tv2[0:cnt, :]
            off += cnt
        cand_scr[h, N_CAND:CAND_ROWS, :] = jnp.full((CAND_ROWS - N_CAND, tt), -jnp.inf, F32)
    cands = [cand_scr[h] for h in heads]
    _extract_topk(cands, PEER_TOPK, lambda i, r, m: None, crank_scr)

    def twice_bf16(x):
        hi = lax.bitcast_convert_type(x.astype(BF16).astype(F32), jnp.uint32)
        return lax.bitcast_convert_type(hi | (hi >> 16), F32)

    for h in heads:
        cand = cands[h]
        chosen = crank_scr[h] < PEER_TOPK
        cmax = cand[0:1, :]
        z = jnp.sum(jnp.where(chosen, jnp.exp(cand - cmax), 0.0), axis=0, keepdims=True)
        cnt_f = chosen.astype(F32)
        off = 0
        for j, cnt in enumerate(CAND_COUNTS):
            kj_scr[h, j:j + 1, :] = jnp.sum(cnt_f[off:off + cnt, :], axis=0, keepdims=True)
            off += cnt

        rank1 = rank_scr[2 * h]
        rank2 = rank_scr[2 * h + 1]
        k1 = jnp.zeros((N_KEYS, tt), F32)
        for j in range(PEER_TOPK):
            k1 = jnp.where(rank1 == j, kj_scr[h, j:j + 1, :], k1)
        k1_o[h] = twice_bf16(k1)
        w1_o[h] = twice_bf16(
            jnp.where(rank1 < PEER_TOPK, jnp.exp(scores[2 * h] - tv_scr[2 * h, 0:1, :]), 0.0) / z)
        r2_o[h] = rank2.astype(F32).astype(BF16)
        w2_o[h] = jnp.where(rank2 < PEER_TOPK, jnp.exp(scores[2 * h + 1] - tv_scr[2 * h + 1, 0:1, :]),
                            0.0).astype(BF16)


def _topk_call(qp, sub_keys):
    n = qp.shape[0]
    tt = ROW_BLOCK
    hp = TOPK_HEADS
    out = pl.BlockSpec((hp, N_KEYS, tt), lambda t, h: (h, 0, t))
    shape = jax.ShapeDtypeStruct((PEER_HEADS, N_KEYS, n), F32)
    shape_bf16 = jax.ShapeDtypeStruct((PEER_HEADS, N_KEYS, n), BF16)
    return pl.pallas_call(
        _topk_kernel,
        grid=(n // tt, PEER_HEADS // hp),
        in_specs=[
            pl.BlockSpec((tt, hp * PEER_QDIM), lambda t, h: (t, h)),
            pl.BlockSpec((hp, 2, N_KEYS, PEER_QDIM // 2), lambda t, h: (h, 0, 0, 0)),
        ],
        out_specs=[out, out, out, out],
        out_shape=[shape_bf16, shape_bf16, shape, shape],
        scratch_shapes=[
            pltpu.VMEM((2 * hp, PEER_TOPK, tt), F32),
            pltpu.VMEM((hp, CAND_ROWS, tt), F32),
            pltpu.VMEM((hp, PEER_TOPK, tt), F32),
            pltpu.VMEM((2 * hp, N_KEYS, tt), jnp.int32),
            pltpu.VMEM((hp, CAND_ROWS, tt), jnp.int32),
        ],
        compiler_params=_cparams(("arbitrary", "arbitrary")),
        name="peer_topk",
    )(qp, sub_keys)


_ERF_A = (-2.72614225801306e-10, 2.77068142495902e-08, -2.10102402082508e-06, -5.69250639462346e-05,
          -7.34990630326855e-04, -2.95459980854025e-03, -1.60960333262415e-02)
_ERF_B = (-1.45660718464996e-05, -2.13374055278905e-04, -1.68282697438203e-03, -7.37332916720468e-03,
          -1.42647390514189e-02)


def _gelu(x):
    dt = x.dtype
    lim = 4.0 * math.sqrt(2.0)
    xc = jnp.clip(x, -lim, lim)
    t = xc * xc
    deg_a = len(_ERF_A) - 1
    deg_b = len(_ERF_B) - 1
    a2 = [c * (0.5 / math.sqrt(2.0)) / 2.0 ** (deg_a - i) for i, c in enumerate(_ERF_A)]
    b2 = [c / 2.0 ** (deg_b - i) for i, c in enumerate(_ERF_B)]
    a = jnp.asarray(a2[0], dt)
    for c in a2[1:]:
        a = a * t + jnp.asarray(c, dt)
    b = jnp.asarray(b2[0], dt)
    for c in b2[1:]:
        b = b * t + jnp.asarray(c, dt)
    return 0.5 * x + (x * xc) * (a / b)


def _dense_kernel(hq_ref, u_ref, vt_ref, r2_ref, w2_ref, k1_ref, w1_ref, o_ref, h_scr, a_scr, *, n_eb):
    s = pl.program_id(0)
    eb = u_ref.shape[0]
    tt = hq_ref.shape[0]
    n_i1 = eb // N_KEYS

    @pl.when(s == 0)
    def _():
        h_scr[...] = jnp.zeros(h_scr.shape, BF16)
        a_scr[...] = jnp.zeros(a_scr.shape, BF16)

    @pl.when(jnp.maximum(s - 2, 0) % n_eb == 0)
    def _():
        o_ref[...] = jnp.zeros(o_ref.shape, F32)

    def rows16(ref, h, ii, cs):
        row = ref[h, ii:ii + 1, cs]
        return pltpu.bitcast(jnp.broadcast_to(row, (8, PEER_LANES)), BF16)

    for c in range(tt // PEER_LANES):
        cs = slice(c * PEER_LANES, (c + 1) * PEER_LANES)

        o_ref[:, cs] += jnp.dot(vt_ref[...], a_scr[:, cs], preferred_element_type=F32)

        for ii in range(n_i1):
            rows = slice(ii * N_KEYS, (ii + 1) * N_KEYS)
            g = jnp.zeros((N_KEYS, PEER_LANES), BF16)
            for h in range(PEER_HEADS):
                k1b = jnp.concatenate([rows16(k1_ref, h, ii, cs)] * (N_KEYS // 16), axis=0)
                w1b = jnp.concatenate([rows16(w1_ref, h, ii, cs)] * (N_KEYS // 16), axis=0)
                keep = r2_ref[h, :, cs] < k1b
                g = g + jnp.where(keep, w2_ref[h, :, cs] * w1b, 0.0)
            a_scr[rows, cs] = _gelu(h_scr[rows, cs]) * g

        h_scr[:, cs] = _nt_dot(u_ref[...], hq_ref[cs, :]).astype(BF16)


def _dense_call(hq, u_tab, vt_tab, r2, w2, k1, w1):
    n = hq.shape[0]
    n_exp = u_tab.shape[0]
    tt = PEER_TOKENS
    eb = PEER_EXPERTS
    n_eb = n_exp // eb
    steps = (n // tt) * n_eb

    def blk(s, lag):
        return jnp.clip(s - lag, 0, steps - 1)

    tab = pl.BlockSpec((PEER_HEADS, N_KEYS, tt), lambda s: (0, 0, blk(s, 1) // n_eb))
    row = pl.BlockSpec((PEER_HEADS, eb // N_KEYS, tt), lambda s: (0, blk(s, 1) % n_eb, blk(s, 1) // n_eb))
    return pl.pallas_call(
        functools.partial(_dense_kernel, n_eb=n_eb),
        grid=(steps + 2,),
        in_specs=[
            pl.BlockSpec((tt, D_MODEL), lambda s: (blk(s, 0) // n_eb, 0)),
            pl.BlockSpec((eb, D_MODEL), lambda s: (blk(s, 0) % n_eb, 0)),
            pl.BlockSpec((D_MODEL, eb), lambda s: (0, blk(s, 2) % n_eb)),
            tab, tab, row, row,
        ],
        out_specs=pl.BlockSpec((D_MODEL, tt), lambda s: (0, blk(s, 2) // n_eb)),
        out_shape=jax.ShapeDtypeStruct((D_MODEL, n), F32),
        scratch_shapes=[pltpu.VMEM((eb, tt), BF16), pltpu.VMEM((eb, tt), BF16)],
        compiler_params=_cparams(("arbitrary",)),
        name="peer_dense",
    )(hq, u_tab, vt_tab, r2, w2, k1, w1)


def _ln2_kernel(x1_ref, ft_ref, mod_ref, lg_ref, lb_ref, o_ref):
    g2 = mod_ref[0, 5:6, :]
    f = ft_ref[...].T
    o_ref[...] = _layer_norm(DEEPNORM_ALPHA * x1_ref[...] + g2 * f, lg_ref[...], lb_ref[...])


def _ln2_call(x1, ft, mods, ln_g, ln_b, ctx_blocks):
    n = x1.shape[0]
    tb = ROW_BLOCK
    return pl.pallas_call(
        _ln2_kernel,
        grid=(n // tb,),
        in_specs=[
            pl.BlockSpec((tb, D_MODEL), lambda i: (i, 0)),
            pl.BlockSpec((D_MODEL, tb), lambda i: (0, i)),
            pl.BlockSpec((1, N_MOD, D_MODEL), lambda i: (jnp.where(i < ctx_blocks, 0, 1), 0, 0)),
            pl.BlockSpec((1, D_MODEL), lambda i: (0, 0)),
            pl.BlockSpec((1, D_MODEL), lambda i: (0, 0)),
        ],
        out_specs=pl.BlockSpec((tb, D_MODEL), lambda i: (i, 0)),
        out_shape=jax.ShapeDtypeStruct((n, D_MODEL), F32),
        compiler_params=_cparams(("arbitrary",)),
        name="ln2",
    )(x1, ft, mods, ln_g, ln_b)


def _rope_tables(ctx_len, s_len):
    t = np.arange(s_len)
    pos = np.stack([t // GRID_W, t % GRID_W], axis=-1).astype(np.float64)
    quarter = HEAD_DIM // 4
    inv_freq = ROPE_THETA ** (-np.arange(quarter, dtype=np.float64) / quarter)
    ang = pos[:, :, None] * inv_freq
    cos = np.repeat(np.cos(ang), 2, axis=1).reshape(s_len, HEAD_DIM)
    sin = np.repeat(np.sin(ang), 2, axis=1).reshape(s_len, HEAD_DIM)
    first_half = (np.arange(HEAD_DIM) % 32) < 16
    sa = np.where(first_half, -sin, 0.0)
    sb = np.where(first_half, 0.0, sin)

    def full(tab, fill):
        tab = np.concatenate([np.full((ctx_len, HEAD_DIM), fill), tab], axis=0)
        return jnp.asarray(np.tile(tab, (1, 2)), F32)

    return full(cos, 1.0), full(sa, 0.0), full(sb, 0.0)


def kernel(x, c, ctx, c_ctx, w_mod, b_mod, w_in, q_norm, k_norm, conv_w, sink, w_branch, w_out,
           ln1_g, ln1_b, w_pq, sub_keys, u_tab, v_tab, ln2_g, ln2_b):
    bsz, s_len, d = x.shape
    ctx_len = ctx.shape[1]
    depth = w_mod.shape[0]
    assert bsz == 1 and d == D_MODEL and depth == DEPTH
    assert ctx_len % ROW_BLOCK == 0
    assert (ctx_len + s_len) % PEER_TOKENS == 0 and (ctx_len + s_len) % INPROJ_ROWS == 0
    ctx_blocks = ctx_len // ROW_BLOCK

    x_all = jnp.concatenate([ctx[0], x[0]], axis=0)
    cos_t, sa_t, sb_t = _rope_tables(ctx_len, s_len)

    cc = jnp.zeros((8, D_MODEL), F32).at[0].set(c_ctx).at[1].set(c[0])
    mod_all = _mod_call(cc, w_mod.astype(BF16), b_mod[:, None, :])

    for layer in range(depth):
        mods = mod_all[layer, 0:2].reshape(2, N_MOD, D_MODEL)
        proj = _inproj_call(x_all, mods, w_in[layer].astype(BF16), ctx_len)
        qg = jnp.tile(q_norm[layer], 2)[None, :]
        kg = jnp.tile(k_norm[layer], 2)[None, :]
        qa, ka, va, qc, kc, vc = _prep_call(proj, cos_t, sa_t, sb_t, qg, kg)
        ya = _attn_a_call(qa, ka, va, ctx_len)
        yc = _attn_c_call(sink[layer], qc, kc, vc, ctx_len)
        x1, hq, qp = _merge_call(x_all, mods, ya, yc, proj, conv_w[layer],
                                 w_branch[layer].astype(BF16), w_out[layer].astype(BF16),
                                 ln1_g[layer][None, :], ln1_b[layer][None, :],
                                 w_pq[layer].astype(BF16), ctx_blocks)
        r2, w2, k1, w1 = _topk_call(qp, sub_keys[layer].astype(BF16))
        ft = _dense_call(hq, u_tab[layer].astype(BF16), v_tab[layer].T.astype(BF16), r2, w2, k1, w1)
        x_all = _ln2_call(x1, ft, mods, ln2_g[layer][None, :], ln2_b[layer][None, :], ctx_blocks)

    return x_all[ctx_len:][None]
```

```python
import functools
import math

import jax
import jax.numpy as jnp
import numpy as np
from jax import lax
from jax.experimental import pallas as pl
from jax.experimental.pallas import tpu as pltpu

F32 = jnp.float32
BF16 = jnp.bfloat16

D_MODEL = 1024
HEAD_DIM = 64
N_HEADS = 8
N_KV = 2
GROUP = N_HEADS // N_KV
GRID_W = 64
WINDOW = 128
CONV_W = 3
B_WIDTH = 512
N_MOD = 6
ROPE_THETA = 10000.0
PEER_HEADS = 8
N_KEYS = 128
PEER_TOPK = 16
PEER_QDIM = 256
DEPTH = 2
IN_COLS = 6144
DEEPNORM_ALPHA = (2 * DEPTH) ** 0.25
LN_EPS = 1e-6
RMS_EPS = 1e-6
NEG_INF = -1e30
ATTN_SCALE = HEAD_DIM ** -0.5

ROW_BLOCK = 256
INPROJ_ROWS = 1280
KEY_CHUNK = 1024
BOUND_SLACK = 1.0 + 2.0 ** -6
SAFE_ROW_SUM = 1e-30
PEER_TOKENS = 1280
PEER_LANES = 256
TOPK_HEADS = 4
PEER_EXPERTS = 1024
VMEM_LIMIT = 56 * 1024 * 1024

CAND_COUNTS = tuple(PEER_TOPK // (j + 1) for j in range(PEER_TOPK))
N_CAND = sum(CAND_COUNTS)
CAND_ROWS = 56


def _cparams(sem):
    return pltpu.CompilerParams(dimension_semantics=sem, vmem_limit_bytes=VMEM_LIMIT)


def _nt_dot(a, b):
    return lax.dot_general(a, b, (((1,), (1,)), ((), ())), preferred_element_type=F32)


def _layer_norm(r, g, b):
    mu = jnp.mean(r, axis=-1, keepdims=True)
    rc = r - mu
    var = jnp.mean(rc * rc, axis=-1, keepdims=True)
    return rc * lax.rsqrt(var + LN_EPS) * g + b


def _mod_kernel(c_ref, w_ref, b_ref, o_ref):
    c = c_ref[...]
    h = (c * jax.nn.sigmoid(c)).astype(BF16)
    o_ref[0] = jnp.dot(h, w_ref[0], preferred_element_type=F32) + b_ref[0]


def _mod_call(cc, w_mod, b_mod):
    depth = w_mod.shape[0]
    cols = w_mod.shape[2]
    tn = 1024
    return pl.pallas_call(
        _mod_kernel,
        grid=(depth, cols // tn),
        in_specs=[
            pl.BlockSpec((8, D_MODEL), lambda l, j: (0, 0)),
            pl.BlockSpec((1, D_MODEL, tn), lambda l, j: (l, 0, j)),
            pl.BlockSpec((1, 1, tn), lambda l, j: (l, 0, j)),
        ],
        out_specs=pl.BlockSpec((1, 8, tn), lambda l, j: (l, 0, j)),
        out_shape=jax.ShapeDtypeStruct((depth, 8, cols), F32),
        compiler_params=_cparams(("arbitrary", "arbitrary")),
        name="mod",
    )(cc, w_mod, b_mod)


def _inproj_kernel(x_ref, mod_ref, w_ref, o_ref, *, ctx_len):
    tb = x_ref.shape[0]
    row = pl.program_id(1) * tb + lax.broadcasted_iota(jnp.int32, (tb, 1), 0)
    is_ctx = row < ctx_len
    sh = jnp.where(is_ctx, mod_ref[0, 0:1, :], mod_ref[1, 0:1, :])
    sc = jnp.where(is_ctx, mod_ref[0, 1:2, :], mod_ref[1, 1:2, :])
    hm = (x_ref[...] * (1.0 + sc) + sh).astype(BF16)
    o_ref[...] = jnp.dot(hm, w_ref[...], preferred_element_type=F32)


def _inproj_call(x_all, mods, w_in, ctx_len):
    n = x_all.shape[0]
    tn = 1024
    tb = INPROJ_ROWS
    return pl.pallas_call(
        functools.partial(_inproj_kernel, ctx_len=ctx_len),
        grid=(IN_COLS // tn, n // tb),
        in_specs=[
            pl.BlockSpec((tb, D_MODEL), lambda j, i: (i, 0)),
            pl.BlockSpec((2, N_MOD, D_MODEL), lambda j, i: (0, 0, 0)),
            pl.BlockSpec((D_MODEL, tn), lambda j, i: (0, j)),
        ],
        out_specs=pl.BlockSpec((tb, tn), lambda j, i: (i, j)),
        out_shape=jax.ShapeDtypeStruct((n, IN_COLS), F32),
        compiler_params=_cparams(("arbitrary", "arbitrary")),
        name="inproj",
    )(x_all, mods, w_in)


def _prep_kernel(p_ref, cos_ref, sa_ref, sb_ref, qg_ref, kg_ref,
                 qa_o, ka_o, va_o, qc_o, kc_o, vc_o):
    x = p_ref[...]
    cos = cos_ref[...]
    sa = sa_ref[...]
    sb = sb_ref[...]

    def tile(t, width):
        reps = width // 128
        return t if reps == 1 else jnp.concatenate([t] * reps, axis=1)

    def rope(y):
        w = y.shape[1]
        return (y * tile(cos, w)
                + pltpu.roll(y, w - 16, 1) * tile(sa, w)
                + pltpu.roll(y, 16, 1) * tile(sb, w))

    def inv_rms(xh):
        return lax.rsqrt(jnp.mean(xh * xh, axis=-1, keepdims=True) + RMS_EPS)

    qa = x[:, 0:512]
    ka = x[:, 512:640]
    va = x[:, 640:768]
    qc = x[:, 768:1280]
    kc = x[:, 1280:1408]
    vc = x[:, 1408:1536]

    qa_r = rope(qa * tile(qg_ref[...], 512))
    ka_r = rope(ka * kg_ref[...])
    qc_r = rope(qc)
    kc_r = rope(kc)

    for h in range(N_HEADS):
        sl = slice(h * HEAD_DIM, (h + 1) * HEAD_DIM)
        qa_o[h, :, 0:HEAD_DIM] = (qa_r[:, sl] * (inv_rms(qa[:, sl]) * ATTN_SCALE)).astype(BF16)
        qa_o[h, :, HEAD_DIM:2 * HEAD_DIM] = jnp.zeros((x.shape[0], HEAD_DIM), BF16)
        qc_o[h] = (qc_r[:, sl] * ATTN_SCALE).astype(BF16)
    lane = lax.broadcasted_iota(jnp.int32, (x.shape[0], HEAD_DIM), 1)
    ones_col = jnp.where(lane == 0, 1.0, 0.0).astype(BF16)
    for h in range(N_KV):
        sl = slice(h * HEAD_DIM, (h + 1) * HEAD_DIM)
        ka_o[h, :, 0:HEAD_DIM] = (ka_r[:, sl] * inv_rms(ka[:, sl])).astype(BF16)
        ka_o[h, :, HEAD_DIM:2 * HEAD_DIM] = ones_col
        va_o[h, :, 0:HEAD_DIM] = va[:, sl].astype(BF16)
        va_o[h, :, HEAD_DIM:2 * HEAD_DIM] = ones_col
        kc_o[h] = kc_r[:, sl].astype(BF16)
        vc_o[h, :, 0:HEAD_DIM] = vc[:, sl].astype(BF16)
        vc_o[h, :, HEAD_DIM:2 * HEAD_DIM] = ones_col


def _prep_call(proj, cos_t, sa_t, sb_t, qg, kg):
    n = proj.shape[0]
    tb = ROW_BLOCK
    tab = pl.BlockSpec((tb, 128), lambda i: (i, 0))
    gain = pl.BlockSpec((1, 128), lambda i: (0, 0))

    def hm(nh, width=HEAD_DIM):
        return pl.BlockSpec((nh, tb, width), lambda i: (0, i, 0))

    def hshape(nh, width=HEAD_DIM):
        return jax.ShapeDtypeStruct((nh, n, width), BF16)

    return pl.pallas_call(
        _prep_kernel,
        grid=(n // tb,),
        in_specs=[pl.BlockSpec((tb, 1536), lambda i: (i, 0)), tab, tab, tab, gain, gain],
        out_specs=[hm(N_HEADS, 2 * HEAD_DIM), hm(N_KV, 2 * HEAD_DIM), hm(N_KV, 2 * HEAD_DIM),
                   hm(N_HEADS), hm(N_KV), hm(N_KV, 2 * HEAD_DIM)],
        out_shape=[hshape(N_HEADS, 2 * HEAD_DIM), hshape(N_KV, 2 * HEAD_DIM), hshape(N_KV, 2 * HEAD_DIM),
                   hshape(N_HEADS), hshape(N_KV), hshape(N_KV, 2 * HEAD_DIM)],
        compiler_params=_cparams(("arbitrary",)),
        name="prep",
    )(proj, cos_t, sa_t, sb_t, qg, kg)


def _store_heads(o_ref, o, tq):
    for g in range(GROUP):
        o_ref[:, g * HEAD_DIM:(g + 1) * HEAD_DIM] = o[g * tq:(g + 1) * tq, :].astype(BF16)


def _attn_a_kernel(q_ref, k_ref, v_ref, o_ref, s_scr, mx_scr, m_scr, acc_scr, kmax_scr, *,
                   ctx_len, ctx_blocks, lat_chunks, unroll):
    i = pl.program_id(1)
    tq = q_ref.shape[1]
    rows = GROUP * tq
    key_chunk = s_scr.shape[2]
    q0 = q_ref[...].reshape(rows, 2 * HEAD_DIM)

    @pl.when(i == 0)
    def _():
        kk = k_ref[0, :, 0:HEAD_DIM].astype(F32)
        k_norm2 = jnp.max(jnp.sum(kk * kk, axis=-1, keepdims=True), axis=0, keepdims=True)
        kmax_scr[...] = jnp.broadcast_to(jnp.sqrt(k_norm2), kmax_scr.shape)

    def chunk_off(c):
        return pl.multiple_of(ctx_len + c * key_chunk, 128)

    def run(q, online):
        def scores(slot, off, width):
            s = _nt_dot(q, k_ref[0, pl.ds(off, width), :])
            s_scr[slot, :, 0:width] = s
            if online:
                mx_scr[slot] = jnp.broadcast_to(jnp.max(s, axis=-1, keepdims=True), (rows, 128))

        def update(slot, off, width):
            s = s_scr[slot, :, 0:width]
            v = v_ref[0, pl.ds(off, width), :]
            if online:
                m_old = m_scr[...]
                m_new = jnp.maximum(m_old, mx_scr[slot])
                p = jnp.exp(s - jnp.concatenate([m_new] * (width // 128), axis=1))
                acc_scr[...] = jnp.exp(m_old - m_new) * acc_scr[...] + jnp.dot(
                    p.astype(BF16), v, preferred_element_type=F32)
                m_scr[...] = m_new
            else:
                acc_scr[...] += jnp.dot(jnp.exp(s).astype(BF16), v, preferred_element_type=F32)

        if online:
            m_scr[...] = jnp.full(m_scr.shape, -jnp.inf, F32)
        acc_scr[...] = jnp.zeros(acc_scr.shape, F32)
        scores(0, 0, ctx_len)
        update(0, 0, ctx_len)

        @pl.when(i >= ctx_blocks)
        def _():
            scores(0, chunk_off(0), key_chunk)

            def body(it, carry):
                for u in range(0, unroll, 2):
                    c = unroll * it + u
                    scores(1, chunk_off(c + 1), key_chunk)
                    update(0, chunk_off(c), key_chunk)
                    scores(0, chunk_off(jnp.minimum(c + 2, lat_chunks - 1)), key_chunk)
                    update(1, chunk_off(c + 1), key_chunk)
                return carry

            lax.fori_loop(0, lat_chunks // unroll, body, 0)

        acc = acc_scr[...]
        _store_heads(o_ref, acc[:, 0:HEAD_DIM] / acc[:, HEAD_DIM:HEAD_DIM + 1], tq)

    qf = q0.astype(F32)
    q_norm = jnp.sqrt(jnp.sum(qf * qf, axis=-1, keepdims=True))
    bound = (q_norm * kmax_scr[0:1, 0:1] * BOUND_SLACK).astype(BF16).astype(F32)
    shift_lane = lax.broadcasted_iota(jnp.int32, qf.shape, 1) == HEAD_DIM
    run(jnp.where(shift_lane, -bound, qf).astype(BF16), online=False)

    @pl.when(jnp.logical_not(jnp.min(acc_scr[:, HEAD_DIM:HEAD_DIM + 1]) > SAFE_ROW_SUM))
    def _():
        run(q0, online=True)


def _attn_a_call(q, k, v, ctx_len):
    n = q.shape[1]
    tq = ROW_BLOCK
    s_len = n - ctx_len
    key_chunk = KEY_CHUNK if s_len % (2 * KEY_CHUNK) == 0 else KEY_CHUNK // 2
    lat_chunks = s_len // key_chunk
    assert s_len % key_chunk == 0 and lat_chunks % 2 == 0
    unroll = 4 if lat_chunks % 4 == 0 else 2
    kern = functools.partial(_attn_a_kernel, ctx_len=ctx_len, ctx_blocks=ctx_len // tq, lat_chunks=lat_chunks,
                             unroll=unroll)
    return pl.pallas_call(
        kern,
        grid=(N_KV, n // tq),
        in_specs=[
            pl.BlockSpec((GROUP, tq, 2 * HEAD_DIM), lambda j, i: (j, i, 0)),
            pl.BlockSpec((1, n, 2 * HEAD_DIM), lambda j, i: (j, 0, 0)),
            pl.BlockSpec((1, n, 2 * HEAD_DIM), lambda j, i: (j, 0, 0)),
        ],
        out_specs=pl.BlockSpec((tq, GROUP * HEAD_DIM), lambda j, i: (i, j)),
        out_shape=jax.ShapeDtypeStruct((n, N_HEADS * HEAD_DIM), BF16),
        scratch_shapes=[
            pltpu.VMEM((2, GROUP * tq, key_chunk), F32),
            pltpu.VMEM((2, GROUP * tq, 128), F32),
            pltpu.VMEM((GROUP * tq, 128), F32),
            pltpu.VMEM((GROUP * tq, 2 * HEAD_DIM), F32),
            pltpu.VMEM((8, 128), F32),
        ],
        compiler_params=_cparams(("arbitrary", "arbitrary")),
        name="attn_global",
    )(q, k, v)


def _attn_c_kernel(sink_ref, q_ref, k_ref, v_ref, o_ref, *, ctx_len, ctx_blocks, s_len):
    j = pl.program_id(0)
    i = pl.program_id(1)
    tq = q_ref.shape[1]
    rows = GROUP * tq
    n_loc = tq + 2 * WINDOW
    q = q_ref[...].reshape(rows, HEAD_DIM)

    p0 = i * tq - ctx_len
    ws = pl.multiple_of(jnp.clip(p0 - WINDOW, 0, s_len - n_loc), 128)
    k_loc = k_ref[0, pl.ds(ctx_len + ws, n_loc), :]
    v_loc = v_ref[0, pl.ds(ctx_len + ws, n_loc), :]
    k_ctx = k_ref[0, 0:ctx_len, :]
    v_ctx = v_ref[0, 0:ctx_len, :]

    dist = (lax.broadcasted_iota(jnp.int32, (tq, n_loc), 0) - lax.broadcasted_iota(jnp.int32, (tq, n_loc), 1)
            + (p0 - ws))
    valid = (jnp.abs(dist) <= WINDOW) & (i >= ctx_blocks)
    s_loc = jnp.where(valid[None], _nt_dot(q, k_loc).reshape(GROUP, tq, n_loc), NEG_INF).reshape(rows, n_loc)
    s_ctx = _nt_dot(q, k_ctx)

    grp = lax.broadcasted_iota(jnp.int32, (rows, 1), 0) // tq
    sink = jnp.full((rows, 1), sink_ref[j * GROUP + GROUP - 1], F32)
    for g in range(GROUP - 1):
        sink = jnp.where(grp == g, sink_ref[j * GROUP + g], sink)

    m = jnp.maximum(jnp.maximum(jnp.max(s_loc, axis=-1, keepdims=True),
                                jnp.max(s_ctx, axis=-1, keepdims=True)), sink)
    p_loc = jnp.exp(s_loc - m)
    p_ctx = jnp.exp(s_ctx - m)
    acc = (jnp.dot(p_loc.astype(BF16), v_loc, preferred_element_type=F32)
           + jnp.dot(p_ctx.astype(BF16), v_ctx, preferred_element_type=F32))
    o = acc[:, 0:HEAD_DIM] / (acc[:, HEAD_DIM:HEAD_DIM + 1] + jnp.exp(sink - m))
    _store_heads(o_ref, o, tq)


def _attn_c_call(sink, q, k, v, ctx_len):
    n = q.shape[1]
    tq = ROW_BLOCK
    kern = functools.partial(_attn_c_kernel, ctx_len=ctx_len, ctx_blocks=ctx_len // tq, s_len=n - ctx_len)
    return pl.pallas_call(
        kern,
        grid=(N_KV, n // tq),
        in_specs=[
            pl.BlockSpec(memory_space=pltpu.SMEM),
            pl.BlockSpec((GROUP, tq, HEAD_DIM), lambda j, i: (j, i, 0)),
            pl.BlockSpec((1, n, HEAD_DIM), lambda j, i: (j, 0, 0)),
            pl.BlockSpec((1, n, 2 * HEAD_DIM), lambda j, i: (j, 0, 0)),
        ],
        out_specs=pl.BlockSpec((tq, GROUP * HEAD_DIM), lambda j, i: (i, j)),
        out_shape=jax.ShapeDtypeStruct((n, N_HEADS * HEAD_DIM), BF16),
        compiler_params=_cparams(("arbitrary", "arbitrary")),
        name="attn_window",
    )(sink, q, k, v)


def _merge_kernel(x_ref, mod_ref, ya_ref, yc_ref, xb_ref, gb_ref, gc_ref,
                  xbp_ref, gcp_ref, xbn_ref, gcn_ref, gl_ref, cw_ref, wb_ref, wo_ref,
                  lg_ref, lb_ref, wpq_ref, x1_ref, hq_ref, qp_ref, *, n_blocks, ctx_blocks):
    i = pl.program_id(0)
    tb = x_ref.shape[0]

    z = gc_ref[...] * xb_ref[...]
    z_before = gcp_ref[7:8, :] * xbp_ref[7:8, :]
    z_after = gcn_ref[0:1, :] * xbn_ref[0:1, :]
    seq_start = (i == 0) | (i == ctx_blocks)
    seq_end = (i == ctx_blocks - 1) | (i == n_blocks - 1)
    z_before = jnp.where(seq_start, 0.0, z_before)
    z_after = jnp.where(seq_end, 0.0, z_after)
    rid = lax.broadcasted_iota(jnp.int32, z.shape, 0)
    z_prev = jnp.where(rid == 0, z_before, pltpu.roll(z, 1, 0))
    z_next = jnp.where(rid == tb - 1, z_after, pltpu.roll(z, tb - 1, 0))
    cw = cw_ref[...]
    yb = gb_ref[...] * (cw[0:1, :] * z_prev + cw[1:2, :] * z + cw[2:3, :] * z_next)

    pa = jnp.dot(ya_ref[...], wb_ref[0], preferred_element_type=F32)
    pb = jnp.dot(yb.astype(BF16), wb_ref[1], preferred_element_type=F32)
    pc = jnp.dot(yc_ref[...], wb_ref[2], preferred_element_type=F32)
    gates = jax.nn.sigmoid(gl_ref[...])
    m = (gates[:, 0:D_MODEL] * pa + gates[:, D_MODEL:2 * D_MODEL] * pb
         + gates[:, 2 * D_MODEL:3 * D_MODEL] * pc)
    y = jnp.dot(m.astype(BF16), wo_ref[...], preferred_element_type=F32)

    g1 = mod_ref[0, 2:3, :]
    sh2 = mod_ref[0, 3:4, :]
    sc2 = mod_ref[0, 4:5, :]
    x1 = _layer_norm(DEEPNORM_ALPHA * x_ref[...] + g1 * y, lg_ref[...], lb_ref[...])
    x1_ref[...] = x1
    hq = (x1 * (1.0 + sc2) + sh2).astype(BF16)
    hq_ref[...] = hq
    qp_ref[...] = jnp.dot(hq, wpq_ref[...], preferred_element_type=F32)


def _merge_call(x_all, mods, ya, yc, proj, conv_w, w_branch, w_out, ln_g, ln_b, w_pq, ctx_blocks):
    n = x_all.shape[0]
    tb = ROW_BLOCK
    nb = n // tb
    halo = tb // 8
    last8 = n // 8 - 1
    kern = functools.partial(_merge_kernel, n_blocks=nb, ctx_blocks=ctx_blocks)
    qcols = PEER_HEADS * PEER_QDIM

    def full(shape):
        return pl.BlockSpec(shape, lambda i: (0,) * len(shape))

    def cols512(c):
        return pl.BlockSpec((tb, B_WIDTH), lambda i: (i, c))

    def before(c):
        return pl.BlockSpec((8, B_WIDTH), lambda i: (jnp.maximum(i * halo - 1, 0), c))

    def after(c):
        return pl.BlockSpec((8, B_WIDTH), lambda i: (jnp.minimum((i + 1) * halo, last8), c))

    return pl.pallas_call(
        kern,
        grid=(nb,),
        in_specs=[
            pl.BlockSpec((tb, D_MODEL), lambda i: (i, 0)),
            pl.BlockSpec((1, N_MOD, D_MODEL), lambda i: (jnp.where(i < ctx_blocks, 0, 1), 0, 0)),
            pl.BlockSpec((tb, N_HEADS * HEAD_DIM), lambda i: (i, 0)),
            pl.BlockSpec((tb, N_HEADS * HEAD_DIM), lambda i: (i, 0)),
            cols512(3), cols512(4), cols512(5),
            before(3), before(5), after(3), after(5),
            pl.BlockSpec((tb, 3 * D_MODEL), lambda i: (i, 1)),
            full((CONV_W, B_WIDTH)),
            full((3, B_WIDTH, D_MODEL)),
            full((D_MODEL, D_MODEL)),
            full((1, D_MODEL)), full((1, D_MODEL)),
            full((D_MODEL, qcols)),
        ],
        out_specs=[
            pl.BlockSpec((tb, D_MODEL), lambda i: (i, 0)),
            pl.BlockSpec((tb, D_MODEL), lambda i: (i, 0)),
            pl.BlockSpec((tb, qcols), lambda i: (i, 0)),
        ],
        out_shape=[
            jax.ShapeDtypeStruct((n, D_MODEL), F32),
            jax.ShapeDtypeStruct((n, D_MODEL), BF16),
            jax.ShapeDtypeStruct((n, qcols), F32),
        ],
        compiler_params=_cparams(("arbitrary",)),
        name="merge",
    )(x_all, mods, ya, yc, proj, proj, proj, proj, proj, proj, proj, proj,
      conv_w, w_branch, w_out, ln_g, ln_b, w_pq)


def _extract_rounds(chains, rounds, on_round, tie_break):
    rows = chains[0].shape[0]
    ridx = lax.broadcasted_iota(jnp.int32, chains[0].shape, 0)
    chains = list(chains)
    ranks = [jnp.full(c.shape, rounds, jnp.int32) for c in chains]
    for r in range(rounds):
        for i, s in enumerate(chains):
            m = jnp.max(s, axis=0, keepdims=True)
            if tie_break:
                first = jnp.min(jnp.where(s == m, ridx, rows), axis=0, keepdims=True)
                hit = ridx == first
            else:
                hit = s == m
            ranks[i] = jnp.where(hit, r, ranks[i])
            chains[i] = jnp.where(hit, -jnp.inf, s)
            on_round(i, r, m)
    return ranks


def _extract_topk(chains, rounds, on_round, rank_ref):
    ranks = _extract_rounds(chains, rounds, on_round, tie_break=False)
    most = jnp.float32(0.0)
    for i, rank in enumerate(ranks):
        rank_ref[i] = rank
        most = jnp.maximum(most, jnp.max(jnp.sum((rank < rounds).astype(F32), axis=0, keepdims=True)))

    @pl.when(most > rounds)
    def _():
        for i, rank in enumerate(_extract_rounds(chains, rounds, on_round, tie_break=True)):
            rank_ref[i] = rank


def _topk_kernel(qp_ref, sk_ref, r2_o, w2_o, k1_o, w1_o, tv_scr, cand_scr, kj_scr, rank_scr, crank_scr):
    tt = qp_ref.shape[0]
    half = PEER_QDIM // 2
    q = qp_ref[...].astype(BF16)
    heads = range(TOPK_HEADS)

    scores = [_nt_dot(sk_ref[h, p], q[:, (2 * h + p) * half:(2 * h + p + 1) * half])
              for h in heads for p in range(2)]

    def keep(i, r, m):
        tv_scr[i, r:r + 1, :] = m

    _extract_topk(scores, PEER_TOPK, keep, rank_scr)

    for h in heads:
        tv1 = tv_scr[2 * h]
        tv2 = tv_scr[2 * h + 1]
        off = 0
        for j, cnt in enumerate(CAND_COUNTS):
            cand_scr[h, off:off + cnt, :] = tv1[j:j + 1, :] + tv2[0:cnt, :]
            off += cnt
        cand_scr[h, N_CAND:CAND_ROWS, :] = jnp.full((CAND_ROWS - N_CAND, tt), -jnp.inf, F32)
    cands = [cand_scr[h] for h in heads]
    _extract_topk(cands, PEER_TOPK, lambda i, r, m: None, crank_scr)

    def twice_bf16(x):
        hi = lax.bitcast_convert_type(x.astype(BF16).astype(F32), jnp.uint32)
        return lax.bitcast_convert_type(hi | (hi >> 16), F32)

    for h in heads:
        cand = cands[h]
        chosen = crank_scr[h] < PEER_TOPK
        cmax = cand[0:1, :]
        z = jnp.sum(jnp.where(chosen, jnp.exp(cand - cmax), 0.0), axis=0, keepdims=True)
        cnt_f = chosen.astype(F32)
        off = 0
        for j, cnt in enumerate(CAND_COUNTS):
            kj_scr[h, j:j + 1, :] = jnp.sum(cnt_f[off:off + cnt, :], axis=0, keepdims=True)
            off += cnt

        rank1 = rank_scr[2 * h]
        rank2 = rank_scr[2 * h + 1]
        k1 = jnp.zeros((N_KEYS, tt), F32)
        for j in range(PEER_TOPK):
            k1 = jnp.where(rank1 == j, kj_scr[h, j:j + 1, :], k1)
        k1_o[h] = twice_bf16(k1)
        w1_o[h] = twice_bf16(
            jnp.where(rank1 < PEER_TOPK, jnp.exp(scores[2 * h] - tv_scr[2 * h, 0:1, :]), 0.0) / z)
        r2_o[h] = rank2.astype(F32).astype(BF16)
        w2_o[h] = jnp.where(rank2 < PEER_TOPK, jnp.exp(scores[2 * h + 1] - tv_scr[2 * h + 1, 0:1, :]),
                            0.0).astype(BF16)


def _topk_call(qp, sub_keys):
    n = qp.shape[0]
    tt = ROW_BLOCK
    hp = TOPK_HEADS
    out = pl.BlockSpec((hp, N_KEYS, tt), lambda t, h: (h, 0, t))
    shape = jax.ShapeDtypeStruct((PEER_HEADS, N_KEYS, n), F32)
    shape_bf16 = jax.ShapeDtypeStruct((PEER_HEADS, N_KEYS, n), BF16)
    return pl.pallas_call(
        _topk_kernel,
        grid=(n // tt, PEER_HEADS // hp),
        in_specs=[
            pl.BlockSpec((tt, hp * PEER_QDIM), lambda t, h: (t, h)),
            pl.BlockSpec((hp, 2, N_KEYS, PEER_QDIM // 2), lambda t, h: (h, 0, 0, 0)),
        ],
        out_specs=[out, out, out, out],
        out_shape=[shape_bf16, shape_bf16, shape, shape],
        scratch_shapes=[
            pltpu.VMEM((2 * hp, PEER_TOPK, tt), F32),
            pltpu.VMEM((hp, CAND_ROWS, tt), F32),
            pltpu.VMEM((hp, PEER_TOPK, tt), F32),
            pltpu.VMEM((2 * hp, N_KEYS, tt), jnp.int32),
            pltpu.VMEM((hp, CAND_ROWS, tt), jnp.int32),
        ],
        compiler_params=_cparams(("arbitrary", "arbitrary")),
        name="peer_topk",
    )(qp, sub_keys)


_ERF_A = (-2.72614225801306e-10, 2.77068142495902e-08, -2.10102402082508e-06, -5.69250639462346e-05,
          -7.34990630326855e-04, -2.95459980854025e-03, -1.60960333262415e-02)
_ERF_B = (-1.45660718464996e-05, -2.13374055278905e-04, -1.68282697438203e-03, -7.37332916720468e-03,
          -1.42647390514189e-02)


def _gelu(x):
    dt = x.dtype
    lim = 4.0 * math.sqrt(2.0)
    xc = jnp.clip(x, -lim, lim)
    t = xc * xc
    deg_a = len(_ERF_A) - 1
    deg_b = len(_ERF_B) - 1
    a2 = [c * (0.5 / math.sqrt(2.0)) / 2.0 ** (deg_a - i) for i, c in enumerate(_ERF_A)]
    b2 = [c / 2.0 ** (deg_b - i) for i, c in enumerate(_ERF_B)]
    a2 = [c / b2[0] for c in a2]
    b2 = [c / b2[0] for c in b2]
    a = jnp.asarray(a2[0], dt)
    for c in a2[1:]:
        a = a * t + jnp.asarray(c, dt)
    b = t + jnp.asarray(b2[1], dt)
    for c in b2[2:]:
        b = b * t + jnp.asarray(c, dt)
    return x * (0.5 + xc * (a / b))


def _dense_kernel(hq_ref, u_ref, vt_ref, r2_ref, w2_ref, k1_ref, w1_ref, o_ref, h_scr, a_scr, *, n_eb):
    s = pl.program_id(0)
    eb = u_ref.shape[0]
    tt = hq_ref.shape[0]
    n_i1 = eb // N_KEYS

    @pl.when(s == 0)
    def _():
        h_scr[...] = jnp.zeros(h_scr.shape, BF16)
        a_scr[...] = jnp.zeros(a_scr.shape, BF16)

    @pl.when(jnp.maximum(s - 2, 0) % n_eb == 0)
    def _():
        o_ref[...] = jnp.zeros(o_ref.shape, F32)

    def rows16(ref, h, ii, cs):
        row = ref[h, ii:ii + 1, cs]
        return pltpu.bitcast(jnp.broadcast_to(row, (8, PEER_LANES)), BF16)

    for c in range(tt // PEER_LANES):
        cs = slice(c * PEER_LANES, (c + 1) * PEER_LANES)

        o_ref[:, cs] += jnp.dot(vt_ref[...], a_scr[:, cs], preferred_element_type=F32)

        for ii in range(n_i1):
            rows = slice(ii * N_KEYS, (ii + 1) * N_KEYS)
            g = None
            for h in range(PEER_HEADS):
                k1b = jnp.concatenate([rows16(k1_ref, h, ii, cs)] * (N_KEYS // 16), axis=0)
                w1b = jnp.concatenate([rows16(w1_ref, h, ii, cs)] * (N_KEYS // 16), axis=0)
                keep = r2_ref[h, :, cs] < k1b
                gate = jnp.where(keep, w2_ref[h, :, cs] * w1b, 0.0)
                g = gate if g is None else g + gate
            a_scr[rows, cs] = _gelu(h_scr[rows, cs]) * g

        h_scr[:, cs] = _nt_dot(u_ref[...], hq_ref[cs, :]).astype(BF16)


def _dense_call(hq, u_tab, vt_tab, r2, w2, k1, w1):
    n = hq.shape[0]
    n_exp = u_tab.shape[0]
    tt = PEER_TOKENS
    eb = PEER_EXPERTS
    n_eb = n_exp // eb
    steps = (n // tt) * n_eb

    def blk(s, lag):
        return jnp.clip(s - lag, 0, steps - 1)

    tab = pl.BlockSpec((PEER_HEADS, N_KEYS, tt), lambda s: (0, 0, blk(s, 1) // n_eb))
    row = pl.BlockSpec((PEER_HEADS, eb // N_KEYS, tt), lambda s: (0, blk(s, 1) % n_eb, blk(s, 1) // n_eb))
    return pl.pallas_call(
        functools.partial(_dense_kernel, n_eb=n_eb),
        grid=(steps + 2,),
        in_specs=[
            pl.BlockSpec((tt, D_MODEL), lambda s: (blk(s, 0) // n_eb, 0)),
            pl.BlockSpec((eb, D_MODEL), lambda s: (blk(s, 0) % n_eb, 0)),
            pl.BlockSpec((D_MODEL, eb), lambda s: (0, blk(s, 2) % n_eb)),
            tab, tab, row, row,
        ],
        out_specs=pl.BlockSpec((D_MODEL, tt), lambda s: (0, blk(s, 2) // n_eb)),
        out_shape=jax.ShapeDtypeStruct((D_MODEL, n), F32),
        scratch_shapes=[pltpu.VMEM((eb, tt), BF16), pltpu.VMEM((eb, tt), BF16)],
        compiler_params=_cparams(("arbitrary",)),
        name="peer_dense",
    )(hq, u_tab, vt_tab, r2, w2, k1, w1)


def _ln2_kernel(x1_ref, ft_ref, mod_ref, lg_ref, lb_ref, o_ref):
    g2 = mod_ref[0, 5:6, :]
    f = ft_ref[...].T
    o_ref[...] = _layer_norm(DEEPNORM_ALPHA * x1_ref[...] + g2 * f, lg_ref[...], lb_ref[...])


def _ln2_call(x1, ft, mods, ln_g, ln_b, ctx_blocks, latent_only):
    n = x1.shape[0]
    tb = ROW_BLOCK
    skip = ctx_blocks if latent_only else 0
    return pl.pallas_call(
        _ln2_kernel,
        grid=(n // tb,),
        in_specs=[
            pl.BlockSpec((tb, D_MODEL), lambda i: (i, 0)),
            pl.BlockSpec((D_MODEL, tb), lambda i: (0, i)),
            pl.BlockSpec((1, N_MOD, D_MODEL), lambda i: (jnp.where(i < ctx_blocks, 0, 1), 0, 0)),
            pl.BlockSpec((1, D_MODEL), lambda i: (0, 0)),
            pl.BlockSpec((1, D_MODEL), lambda i: (0, 0)),
        ],
        out_specs=pl.BlockSpec((tb, D_MODEL), lambda i: (jnp.maximum(i - skip, 0), 0)),
        out_shape=jax.ShapeDtypeStruct((n - skip * tb, D_MODEL), F32),
        compiler_params=_cparams(("arbitrary",)),
        name="ln2",
    )(x1, ft, mods, ln_g, ln_b)


def _rope_tables(ctx_len, s_len):
    t = np.arange(s_len)
    pos = np.stack([t // GRID_W, t % GRID_W], axis=-1).astype(np.float64)
    quarter = HEAD_DIM // 4
    inv_freq = ROPE_THETA ** (-np.arange(quarter, dtype=np.float64) / quarter)
    ang = pos[:, :, None] * inv_freq
    cos = np.repeat(np.cos(ang), 2, axis=1).reshape(s_len, HEAD_DIM)
    sin = np.repeat(np.sin(ang), 2, axis=1).reshape(s_len, HEAD_DIM)
    first_half = (np.arange(HEAD_DIM) % 32) < 16
    sa = np.where(first_half, -sin, 0.0)
    sb = np.where(first_half, 0.0, sin)

    def full(tab, fill):
        tab = np.concatenate([np.full((ctx_len, HEAD_DIM), fill), tab], axis=0)
        return jnp.asarray(np.tile(tab, (1, 2)), F32)

    return full(cos, 1.0), full(sa, 0.0), full(sb, 0.0)


def kernel(x, c, ctx, c_ctx, w_mod, b_mod, w_in, q_norm, k_norm, conv_w, sink, w_branch, w_out,
           ln1_g, ln1_b, w_pq, sub_keys, u_tab, v_tab, ln2_g, ln2_b):
    bsz, s_len, d = x.shape
    ctx_len = ctx.shape[1]
    depth = w_mod.shape[0]
    assert bsz == 1 and d == D_MODEL and depth == DEPTH
    assert ctx_len % ROW_BLOCK == 0
    assert (ctx_len + s_len) % PEER_TOKENS == 0 and (ctx_len + s_len) % INPROJ_ROWS == 0
    ctx_blocks = ctx_len // ROW_BLOCK

    x_all = jnp.concatenate([ctx[0], x[0]], axis=0)
    cos_t, sa_t, sb_t = _rope_tables(ctx_len, s_len)

    cc = jnp.zeros((8, D_MODEL), F32).at[0].set(c_ctx).at[1].set(c[0])
    mod_all = _mod_call(cc, w_mod.astype(BF16), b_mod[:, None, :])

    for layer in range(depth):
        mods = mod_all[layer, 0:2].reshape(2, N_MOD, D_MODEL)
        proj = _inproj_call(x_all, mods, w_in[layer].astype(BF16), ctx_len)
        qg = jnp.tile(q_norm[layer], 2)[None, :]
        kg = jnp.tile(k_norm[layer], 2)[None, :]
        qa, ka, va, qc, kc, vc = _prep_call(proj, cos_t, sa_t, sb_t, qg, kg)
        ya = _attn_a_call(qa, ka, va, ctx_len)
        yc = _attn_c_call(sink[layer], qc, kc, vc, ctx_len)
        x1, hq, qp = _merge_call(x_all, mods, ya, yc, proj, conv_w[layer],
                                 w_branch[layer].astype(BF16), w_out[layer].astype(BF16),
                                 ln1_g[layer][None, :], ln1_b[layer][None, :],
                                 w_pq[layer].astype(BF16), ctx_blocks)
        r2, w2, k1, w1 = _topk_call(qp, sub_keys[layer].astype(BF16))
        ft = _dense_call(hq, u_tab[layer].astype(BF16), v_tab[layer].T.astype(BF16), r2, w2, k1, w1)
        x_all = _ln2_call(x1, ft, mods, ln2_g[layer][None, :], ln2_b[layer][None, :], ctx_blocks,
                          latent_only=(layer == depth - 1))

    return x_all[None]
```

```python
import functools
import math

import jax
import jax.numpy as jnp
import numpy as np
from jax import lax
from jax.experimental import pallas as pl
from jax.experimental.pallas import tpu as pltpu

F32 = jnp.float32
BF16 = jnp.bfloat16

D_MODEL = 1024
HEAD_DIM = 64
N_HEADS = 8
N_KV = 2
GROUP = N_HEADS // N_KV
GRID_W = 64
WINDOW = 128
CONV_W = 3
B_WIDTH = 512
N_MOD = 6
ROPE_THETA = 10000.0
PEER_HEADS = 8
N_KEYS = 128
PEER_TOPK = 16
PEER_QDIM = 256
DEPTH = 2
IN_COLS = 6144
DEEPNORM_ALPHA = (2 * DEPTH) ** 0.25
LN_EPS = 1e-6
RMS_EPS = 1e-6
NEG_INF = -1e30
ATTN_SCALE = HEAD_DIM ** -0.5

ROW_BLOCK = 256
INPROJ_ROWS = 1280
KEY_CHUNK = 1024
BOUND_SLACK = 1.0 + 2.0 ** -6
SAFE_ROW_SUM = 1e-30
PEER_TOKENS = 1280
PEER_LANES = 256
TOPK_HEADS = 4
PEER_EXPERTS = 1024
VMEM_LIMIT = 56 * 1024 * 1024

CAND_COUNTS = tuple(PEER_TOPK // (j + 1) for j in range(PEER_TOPK))
N_CAND = sum(CAND_COUNTS)
CAND_ROWS = 56


def _cparams(sem):
    return pltpu.CompilerParams(dimension_semantics=sem, vmem_limit_bytes=VMEM_LIMIT)


def _nt_dot(a, b):
    return lax.dot_general(a, b, (((1,), (1,)), ((), ())), preferred_element_type=F32)


def _layer_norm(r, g, b):
    mu = jnp.mean(r, axis=-1, keepdims=True)
    rc = r - mu
    var = jnp.mean(rc * rc, axis=-1, keepdims=True)
    return rc * lax.rsqrt(var + LN_EPS) * g + b


def _mod_kernel(c_ref, w_ref, b_ref, o_ref):
    c = c_ref[...]
    h = (c * jax.nn.sigmoid(c)).astype(BF16)
    o_ref[0] = jnp.dot(h, w_ref[0], preferred_element_type=F32) + b_ref[0]


def _mod_call(cc, w_mod, b_mod):
    depth = w_mod.shape[0]
    cols = w_mod.shape[2]
    tn = 1024
    return pl.pallas_call(
        _mod_kernel,
        grid=(depth, cols // tn),
        in_specs=[
            pl.BlockSpec((8, D_MODEL), lambda l, j: (0, 0)),
            pl.BlockSpec((1, D_MODEL, tn), lambda l, j: (l, 0, j)),
            pl.BlockSpec((1, 1, tn), lambda l, j: (l, 0, j)),
        ],
        out_specs=pl.BlockSpec((1, 8, tn), lambda l, j: (l, 0, j)),
        out_shape=jax.ShapeDtypeStruct((depth, 8, cols), F32),
        compiler_params=_cparams(("arbitrary", "arbitrary")),
        name="mod",
    )(cc, w_mod, b_mod)


def _inproj_kernel(x_ref, mod_ref, w_ref, o_ref, hm_scr, *, ctx_len):
    tb = x_ref.shape[0]

    @pl.when(pl.program_id(1) == 0)
    def _():
        row = pl.program_id(0) * tb + lax.broadcasted_iota(jnp.int32, (tb, 1), 0)
        is_ctx = row < ctx_len
        sh = jnp.where(is_ctx, mod_ref[0, 0:1, :], mod_ref[1, 0:1, :])
        sc = jnp.where(is_ctx, mod_ref[0, 1:2, :], mod_ref[1, 1:2, :])
        hm_scr[...] = (x_ref[...] * (1.0 + sc) + sh).astype(BF16)

    o_ref[...] = jnp.dot(hm_scr[...], w_ref[...], preferred_element_type=F32)


def _inproj_call(x_all, mods, w_in, ctx_len):
    n = x_all.shape[0]
    tn = 1024
    tb = INPROJ_ROWS
    return pl.pallas_call(
        functools.partial(_inproj_kernel, ctx_len=ctx_len),
        grid=(n // tb, IN_COLS // tn),
        in_specs=[
            pl.BlockSpec((tb, D_MODEL), lambda i, j: (i, 0)),
            pl.BlockSpec((2, N_MOD, D_MODEL), lambda i, j: (0, 0, 0)),
            pl.BlockSpec((D_MODEL, tn), lambda i, j: (0, j)),
        ],
        out_specs=pl.BlockSpec((tb, tn), lambda i, j: (i, j)),
        out_shape=jax.ShapeDtypeStruct((n, IN_COLS), F32),
        scratch_shapes=[pltpu.VMEM((tb, D_MODEL), BF16)],
        compiler_params=_cparams(("arbitrary", "arbitrary")),
        name="inproj",
    )(x_all, mods, w_in)


def _prep_kernel(p_ref, cos_ref, sa_ref, sb_ref, qg_ref, kg_ref,
                 qa_o, ka_o, va_o, qc_o, kc_o, vc_o):
    x = p_ref[...]
    cos = cos_ref[...]
    sa = sa_ref[...]
    sb = sb_ref[...]

    def tile(t, width):
        reps = width // 128
        return t if reps == 1 else jnp.concatenate([t] * reps, axis=1)

    def rope(y):
        w = y.shape[1]
        return (y * tile(cos, w)
                + pltpu.roll(y, w - 16, 1) * tile(sa, w)
                + pltpu.roll(y, 16, 1) * tile(sb, w))

    def inv_rms(xh):
        return lax.rsqrt(jnp.mean(xh * xh, axis=-1, keepdims=True) + RMS_EPS)

    qa = x[:, 0:512]
    ka = x[:, 512:640]
    va = x[:, 640:768]
    qc = x[:, 768:1280]
    kc = x[:, 1280:1408]
    vc = x[:, 1408:1536]

    qa_r = rope(qa * tile(qg_ref[...], 512))
    ka_r = rope(ka * kg_ref[...])
    qc_r = rope(qc)
    kc_r = rope(kc)

    for h in range(N_HEADS):
        sl = slice(h * HEAD_DIM, (h + 1) * HEAD_DIM)
        qa_o[h, :, 0:HEAD_DIM] = (qa_r[:, sl] * (inv_rms(qa[:, sl]) * ATTN_SCALE)).astype(BF16)
        qa_o[h, :, HEAD_DIM:2 * HEAD_DIM] = jnp.zeros((x.shape[0], HEAD_DIM), BF16)
        qc_o[h] = (qc_r[:, sl] * ATTN_SCALE).astype(BF16)
    lane = lax.broadcasted_iota(jnp.int32, (x.shape[0], HEAD_DIM), 1)
    ones_col = jnp.where(lane == 0, 1.0, 0.0).astype(BF16)
    for h in range(N_KV):
        sl = slice(h * HEAD_DIM, (h + 1) * HEAD_DIM)
        ka_o[h, :, 0:HEAD_DIM] = (ka_r[:, sl] * inv_rms(ka[:, sl])).astype(BF16)
        ka_o[h, :, HEAD_DIM:2 * HEAD_DIM] = ones_col
        va_o[h, :, 0:HEAD_DIM] = va[:, sl].astype(BF16)
        va_o[h, :, HEAD_DIM:2 * HEAD_DIM] = ones_col
        kc_o[h] = kc_r[:, sl].astype(BF16)
        vc_o[h, :, 0:HEAD_DIM] = vc[:, sl].astype(BF16)
        vc_o[h, :, HEAD_DIM:2 * HEAD_DIM] = ones_col


def _prep_call(proj, cos_t, sa_t, sb_t, qg, kg):
    n = proj.shape[0]
    tb = ROW_BLOCK
    tab = pl.BlockSpec((tb, 128), lambda i: (i, 0))
    gain = pl.BlockSpec((1, 128), lambda i: (0, 0))

    def hm(nh, width=HEAD_DIM):
        return pl.BlockSpec((nh, tb, width), lambda i: (0, i, 0))

    def hshape(nh, width=HEAD_DIM):
        return jax.ShapeDtypeStruct((nh, n, width), BF16)

    return pl.pallas_call(
        _prep_kernel,
        grid=(n // tb,),
        in_specs=[pl.BlockSpec((tb, 1536), lambda i: (i, 0)), tab, tab, tab, gain, gain],
        out_specs=[hm(N_HEADS, 2 * HEAD_DIM), hm(N_KV, 2 * HEAD_DIM), hm(N_KV, 2 * HEAD_DIM),
                   hm(N_HEADS), hm(N_KV), hm(N_KV, 2 * HEAD_DIM)],
        out_shape=[hshape(N_HEADS, 2 * HEAD_DIM), hshape(N_KV, 2 * HEAD_DIM), hshape(N_KV, 2 * HEAD_DIM),
                   hshape(N_HEADS), hshape(N_KV), hshape(N_KV, 2 * HEAD_DIM)],
        compiler_params=_cparams(("arbitrary",)),
        name="prep",
    )(proj, cos_t, sa_t, sb_t, qg, kg)


def _store_heads(o_ref, o, tq):
    for g in range(GROUP):
        o_ref[:, g * HEAD_DIM:(g + 1) * HEAD_DIM] = o[g * tq:(g + 1) * tq, :].astype(BF16)


def _attn_a_kernel(q_ref, k_ref, v_ref, o_ref, s_scr, mx_scr, m_scr, acc_scr, kmax_scr, *,
                   ctx_len, ctx_blocks, lat_chunks, unroll):
    i = pl.program_id(1)
    tq = q_ref.shape[1]
    rows = GROUP * tq
    key_chunk = s_scr.shape[2]
    q0 = q_ref[...].reshape(rows, 2 * HEAD_DIM)

    @pl.when(i == 0)
    def _():
        kk = k_ref[0, :, 0:HEAD_DIM].astype(F32)
        k_norm2 = jnp.max(jnp.sum(kk * kk, axis=-1, keepdims=True), axis=0, keepdims=True)
        kmax_scr[...] = jnp.broadcast_to(jnp.sqrt(k_norm2), kmax_scr.shape)

    def chunk_off(c):
        return pl.multiple_of(ctx_len + c * key_chunk, 128)

    def run(q, online):
        def scores(slot, off, width):
            s = _nt_dot(q, k_ref[0, pl.ds(off, width), :])
            s_scr[slot, :, 0:width] = s
            if online:
                mx_scr[slot] = jnp.broadcast_to(jnp.max(s, axis=-1, keepdims=True), (rows, 128))

        def update(slot, off, width):
            s = s_scr[slot, :, 0:width]
            v = v_ref[0, pl.ds(off, width), :]
            if online:
                m_old = m_scr[...]
                m_new = jnp.maximum(m_old, mx_scr[slot])
                p = jnp.exp(s - jnp.concatenate([m_new] * (width // 128), axis=1))
                acc_scr[...] = jnp.exp(m_old - m_new) * acc_scr[...] + jnp.dot(
                    p.astype(BF16), v, preferred_element_type=F32)
                m_scr[...] = m_new
            else:
                acc_scr[...] += jnp.dot(jnp.exp(s).astype(BF16), v, preferred_element_type=F32)

        if online:
            m_scr[...] = jnp.full(m_scr.shape, -jnp.inf, F32)
        acc_scr[...] = jnp.zeros(acc_scr.shape, F32)
        scores(0, 0, ctx_len)
        update(0, 0, ctx_len)

        @pl.when(i >= ctx_blocks)
        def _():
            scores(0, chunk_off(0), key_chunk)

            def body(it, carry):
                for u in range(0, unroll, 2):
                    c = unroll * it + u
                    scores(1, chunk_off(c + 1), key_chunk)
                    update(0, chunk_off(c), key_chunk)
                    scores(0, chunk_off(jnp.minimum(c + 2, lat_chunks - 1)), key_chunk)
                    update(1, chunk_off(c + 1), key_chunk)
                return carry

            lax.fori_loop(0, lat_chunks // unroll, body, 0)

        acc = acc_scr[...]
        _store_heads(o_ref, acc[:, 0:HEAD_DIM] / acc[:, HEAD_DIM:HEAD_DIM + 1], tq)

    qf = q0.astype(F32)
    q_norm = jnp.sqrt(jnp.sum(qf * qf, axis=-1, keepdims=True))
    bound = (q_norm * kmax_scr[0:1, 0:1] * BOUND_SLACK).astype(BF16).astype(F32)
    shift_lane = lax.broadcasted_iota(jnp.int32, qf.shape, 1) == HEAD_DIM
    run(jnp.where(shift_lane, -bound, qf).astype(BF16), online=False)

    @pl.when(jnp.logical_not(jnp.min(acc_scr[:, HEAD_DIM:HEAD_DIM + 1]) > SAFE_ROW_SUM))
    def _():
        run(q0, online=True)


def _attn_a_call(q, k, v, ctx_len):
    n = q.shape[1]
    tq = ROW_BLOCK
    s_len = n - ctx_len
    key_chunk = KEY_CHUNK if s_len % (2 * KEY_CHUNK) == 0 else KEY_CHUNK // 2
    lat_chunks = s_len // key_chunk
    assert s_len % key_chunk == 0 and lat_chunks % 2 == 0
    unroll = 4 if lat_chunks % 4 == 0 else 2
    kern = functools.partial(_attn_a_kernel, ctx_len=ctx_len, ctx_blocks=ctx_len // tq, lat_chunks=lat_chunks,
                             unroll=unroll)
    return pl.pallas_call(
        kern,
        grid=(N_KV, n // tq),
        in_specs=[
            pl.BlockSpec((GROUP, tq, 2 * HEAD_DIM), lambda j, i: (j, i, 0)),
            pl.BlockSpec((1, n, 2 * HEAD_DIM), lambda j, i: (j, 0, 0)),
            pl.BlockSpec((1, n, 2 * HEAD_DIM), lambda j, i: (j, 0, 0)),
        ],
        out_specs=pl.BlockSpec((tq, GROUP * HEAD_DIM), lambda j, i: (i, j)),
        out_shape=jax.ShapeDtypeStruct((n, N_HEADS * HEAD_DIM), BF16),
        scratch_shapes=[
            pltpu.VMEM((2, GROUP * tq, key_chunk), F32),
            pltpu.VMEM((2, GROUP * tq, 128), F32),
            pltpu.VMEM((GROUP * tq, 128), F32),
            pltpu.VMEM((GROUP * tq, 2 * HEAD_DIM), F32),
            pltpu.VMEM((8, 128), F32),
        ],
        compiler_params=_cparams(("arbitrary", "arbitrary")),
        name="attn_global",
    )(q, k, v)


def _attn_c_kernel(sink_ref, q_ref, k_ref, v_ref, o_ref, *, ctx_len, ctx_blocks, s_len):
    j = pl.program_id(0)
    i = pl.program_id(1)
    tq = q_ref.shape[1]
    n_loc = tq + 2 * WINDOW

    p0 = i * tq - ctx_len
    ws = pl.multiple_of(jnp.clip(p0 - WINDOW, 0, s_len - n_loc), 128)
    k_loc = k_ref[0, pl.ds(ctx_len + ws, n_loc), :]
    v_loc = v_ref[0, pl.ds(ctx_len + ws, n_loc), :]
    k_ctx = k_ref[0, 0:ctx_len, :]
    v_ctx = v_ref[0, 0:ctx_len, :]

    dist = (lax.broadcasted_iota(jnp.int32, (tq, n_loc), 0) - lax.broadcasted_iota(jnp.int32, (tq, n_loc), 1)
            + (p0 - ws))
    valid = (jnp.abs(dist) <= WINDOW) & (i >= ctx_blocks)

    for g in range(GROUP):
        q = q_ref[g]
        sink = sink_ref[j * GROUP + g]
        s_loc = jnp.where(valid, _nt_dot(q, k_loc), NEG_INF)
        s_ctx = _nt_dot(q, k_ctx)
        m = jnp.maximum(jnp.maximum(jnp.max(s_loc, axis=-1, keepdims=True),
                                    jnp.max(s_ctx, axis=-1, keepdims=True)), sink)
        acc = (jnp.dot(jnp.exp(s_loc - m).astype(BF16), v_loc, preferred_element_type=F32)
               + jnp.dot(jnp.exp(s_ctx - m).astype(BF16), v_ctx, preferred_element_type=F32))
        o = acc[:, 0:HEAD_DIM] / (acc[:, HEAD_DIM:HEAD_DIM + 1] + jnp.exp(sink - m))
        o_ref[:, g * HEAD_DIM:(g + 1) * HEAD_DIM] = o.astype(BF16)


def _attn_c_call(sink, q, k, v, ctx_len):
    n = q.shape[1]
    tq = ROW_BLOCK
    kern = functools.partial(_attn_c_kernel, ctx_len=ctx_len, ctx_blocks=ctx_len // tq, s_len=n - ctx_len)
    return pl.pallas_call(
        kern,
        grid=(N_KV, n // tq),
        in_specs=[
            pl.BlockSpec(memory_space=pltpu.SMEM),
            pl.BlockSpec((GROUP, tq, HEAD_DIM), lambda j, i: (j, i, 0)),
            pl.BlockSpec((1, n, HEAD_DIM), lambda j, i: (j, 0, 0)),
            pl.BlockSpec((1, n, 2 * HEAD_DIM), lambda j, i: (j, 0, 0)),
        ],
        out_specs=pl.BlockSpec((tq, GROUP * HEAD_DIM), lambda j, i: (i, j)),
        out_shape=jax.ShapeDtypeStruct((n, N_HEADS * HEAD_DIM), BF16),
        compiler_params=_cparams(("arbitrary", "arbitrary")),
        name="attn_window",
    )(sink, q, k, v)


def _merge_kernel(x_ref, mod_ref, ya_ref, yc_ref, xb_ref, gb_ref, gc_ref,
                  xbp_ref, gcp_ref, xbn_ref, gcn_ref, gl_ref, cw_ref, wb_ref, wo_ref,
                  lg_ref, lb_ref, wpq_ref, x1_ref, hq_ref, qp_ref, *, n_blocks, ctx_blocks):
    i = pl.program_id(0)
    tb = x_ref.shape[0]

    z = gc_ref[...] * xb_ref[...]
    z_before = gcp_ref[7:8, :] * xbp_ref[7:8, :]
    z_after = gcn_ref[0:1, :] * xbn_ref[0:1, :]
    seq_start = (i == 0) | (i == ctx_blocks)
    seq_end = (i == ctx_blocks - 1) | (i == n_blocks - 1)
    z_before = jnp.where(seq_start, 0.0, z_before)
    z_after = jnp.where(seq_end, 0.0, z_after)
    rid = lax.broadcasted_iota(jnp.int32, z.shape, 0)
    z_prev = jnp.where(rid == 0, z_before, pltpu.roll(z, 1, 0))
    z_next = jnp.where(rid == tb - 1, z_after, pltpu.roll(z, tb - 1, 0))
    cw = cw_ref[...]
    yb = gb_ref[...] * (cw[0:1, :] * z_prev + cw[1:2, :] * z + cw[2:3, :] * z_next)

    pa = jnp.dot(ya_ref[...], wb_ref[0], preferred_element_type=F32)
    pb = jnp.dot(yb.astype(BF16), wb_ref[1], preferred_element_type=F32)
    pc = jnp.dot(yc_ref[...], wb_ref[2], preferred_element_type=F32)
    gates = jax.nn.sigmoid(gl_ref[...])
    m = (gates[:, 0:D_MODEL] * pa + gates[:, D_MODEL:2 * D_MODEL] * pb
         + gates[:, 2 * D_MODEL:3 * D_MODEL] * pc)
    y = jnp.dot(m.astype(BF16), wo_ref[...], preferred_element_type=F32)

    g1 = mod_ref[0, 2:3, :]
    sh2 = mod_ref[0, 3:4, :]
    sc2 = mod_ref[0, 4:5, :]
    x1 = _layer_norm(DEEPNORM_ALPHA * x_ref[...] + g1 * y, lg_ref[...], lb_ref[...])
    x1_ref[...] = x1
    hq = (x1 * (1.0 + sc2) + sh2).astype(BF16)
    hq_ref[...] = hq
    qp_ref[...] = jnp.dot(hq, wpq_ref[...], preferred_element_type=F32)


def _merge_call(x_all, mods, ya, yc, proj, conv_w, w_branch, w_out, ln_g, ln_b, w_pq, ctx_blocks):
    n = x_all.shape[0]
    tb = ROW_BLOCK
    nb = n // tb
    halo = tb // 8
    last8 = n // 8 - 1
    kern = functools.partial(_merge_kernel, n_blocks=nb, ctx_blocks=ctx_blocks)
    qcols = PEER_HEADS * PEER_QDIM

    def full(shape):
        return pl.BlockSpec(shape, lambda i: (0,) * len(shape))

    def cols512(c):
        return pl.BlockSpec((tb, B_WIDTH), lambda i: (i, c))

    def before(c):
        return pl.BlockSpec((8, B_WIDTH), lambda i: (jnp.maximum(i * halo - 1, 0), c))

    def after(c):
        return pl.BlockSpec((8, B_WIDTH), lambda i: (jnp.minimum((i + 1) * halo, last8), c))

    return pl.pallas_call(
        kern,
        grid=(nb,),
        in_specs=[
            pl.BlockSpec((tb, D_MODEL), lambda i: (i, 0)),
            pl.BlockSpec((1, N_MOD, D_MODEL), lambda i: (jnp.where(i < ctx_blocks, 0, 1), 0, 0)),
            pl.BlockSpec((tb, N_HEADS * HEAD_DIM), lambda i: (i, 0)),
            pl.BlockSpec((tb, N_HEADS * HEAD_DIM), lambda i: (i, 0)),
            cols512(3), cols512(4), cols512(5),
            before(3), before(5), after(3), after(5),
            pl.BlockSpec((tb, 3 * D_MODEL), lambda i: (i, 1)),
            full((CONV_W, B_WIDTH)),
            full((3, B_WIDTH, D_MODEL)),
            full((D_MODEL, D_MODEL)),
            full((1, D_MODEL)), full((1, D_MODEL)),
            full((D_MODEL, qcols)),
        ],
        out_specs=[
            pl.BlockSpec((tb, D_MODEL), lambda i: (i, 0)),
            pl.BlockSpec((tb, D_MODEL), lambda i: (i, 0)),
            pl.BlockSpec((tb, qcols), lambda i: (i, 0)),
        ],
        out_shape=[
            jax.ShapeDtypeStruct((n, D_MODEL), F32),
            jax.ShapeDtypeStruct((n, D_MODEL), BF16),
            jax.ShapeDtypeStruct((n, qcols), F32),
        ],
        compiler_params=_cparams(("arbitrary",)),
        name="merge",
    )(x_all, mods, ya, yc, proj, proj, proj, proj, proj, proj, proj, proj,
      conv_w, w_branch, w_out, ln_g, ln_b, w_pq)


def _extract_rounds(chains, rounds, on_round, tie_break):
    rows = chains[0].shape[0]
    ridx = lax.broadcasted_iota(jnp.int32, chains[0].shape, 0)
    chains = list(chains)
    ranks = [jnp.full(c.shape, rounds, jnp.int32) for c in chains]
    for r in range(rounds):
        for i, s in enumerate(chains):
            m = jnp.max(s, axis=0, keepdims=True)
            if tie_break:
                first = jnp.min(jnp.where(s == m, ridx, rows), axis=0, keepdims=True)
                hit = ridx == first
            else:
                hit = s == m
            ranks[i] = jnp.where(hit, r, ranks[i])
            chains[i] = jnp.where(hit, -jnp.inf, s)
            on_round(i, r, m)
    return ranks


def _extract_topk(chains, rounds, on_round, rank_ref):
    ranks = _extract_rounds(chains, rounds, on_round, tie_break=False)
    most = jnp.float32(0.0)
    for i, rank in enumerate(ranks):
        rank_ref[i] = rank
        most = jnp.maximum(most, jnp.max(jnp.sum((rank < rounds).astype(F32), axis=0, keepdims=True)))

    @pl.when(most > rounds)
    def _():
        for i, rank in enumerate(_extract_rounds(chains, rounds, on_round, tie_break=True)):
            rank_ref[i] = rank


def _topk_kernel(qp_ref, sk_ref, r2_o, w2_o, k1_o, w1_o, tv_scr, cand_scr, kj_scr, rank_scr, crank_scr):
    tt = qp_ref.shape[0]
    half = PEER_QDIM // 2
    q = qp_ref[...].astype(BF16)
    heads = range(TOPK_HEADS)

    scores = [_nt_dot(sk_ref[h, p], q[:, (2 * h + p) * half:(2 * h + p + 1) * half])
              for h in heads for p in range(2)]

    def keep(i, r, m):
        tv_scr[i, r:r + 1, :] = m

    _extract_topk(scores, PEER_TOPK, keep, rank_scr)

    for h in heads:
        tv1 = tv_scr[2 * h]
        tv2 = tv_scr[2 * h + 1]
        off = 0
        for j, cnt in enumerate(CAND_COUNTS):
            cand_scr[h, off:off + cnt, :] = tv1[j:j + 1, :] + tv2[0:cnt, :]
            off += cnt
        cand_scr[h, N_CAND:CAND_ROWS, :] = jnp.full((CAND_ROWS - N_CAND, tt), -jnp.inf, F32)
    cands = [cand_scr[h] for h in heads]
    _extract_topk(cands, PEER_TOPK, lambda i, r, m: None, crank_scr)

    def twice_bf16(x):
        hi = lax.bitcast_convert_type(x.astype(BF16).astype(F32), jnp.uint32)
        return lax.bitcast_convert_type(hi | (hi >> 16), F32)

    for h in heads:
        cand = cands[h]
        chosen = crank_scr[h] < PEER_TOPK
        cmax = cand[0:1, :]
        z = jnp.sum(jnp.where(chosen, jnp.exp(cand - cmax), 0.0), axis=0, keepdims=True)
        cnt_f = chosen.astype(F32)
        off = 0
        for j, cnt in enumerate(CAND_COUNTS):
            kj_scr[h, j:j + 1, :] = jnp.sum(cnt_f[off:off + cnt, :], axis=0, keepdims=True)
            off += cnt

        rank1 = rank_scr[2 * h]
        rank2 = rank_scr[2 * h + 1]
        k1 = jnp.zeros((N_KEYS, tt), F32)
        for j in range(PEER_TOPK):
            k1 = jnp.where(rank1 == j, kj_scr[h, j:j + 1, :], k1)
        k1_o[h] = twice_bf16(k1)
        w1_o[h] = twice_bf16(
            jnp.where(rank1 < PEER_TOPK, jnp.exp(scores[2 * h] - tv_scr[2 * h, 0:1, :]), 0.0) / z)
        r2_o[h] = rank2.astype(F32).astype(BF16)
        w2_o[h] = jnp.where(rank2 < PEER_TOPK, jnp.exp(scores[2 * h + 1] - tv_scr[2 * h + 1, 0:1, :]),
                            0.0).astype(BF16)


def _topk_call(qp, sub_keys):
    n = qp.shape[0]
    tt = ROW_BLOCK
    hp = TOPK_HEADS
    out = pl.BlockSpec((hp, N_KEYS, tt), lambda t, h: (h, 0, t))
    shape = jax.ShapeDtypeStruct((PEER_HEADS, N_KEYS, n), F32)
    shape_bf16 = jax.ShapeDtypeStruct((PEER_HEADS, N_KEYS, n), BF16)
    return pl.pallas_call(
        _topk_kernel,
        grid=(n // tt, PEER_HEADS // hp),
        in_specs=[
            pl.BlockSpec((tt, hp * PEER_QDIM), lambda t, h: (t, h)),
            pl.BlockSpec((hp, 2, N_KEYS, PEER_QDIM // 2), lambda t, h: (h, 0, 0, 0)),
        ],
        out_specs=[out, out, out, out],
        out_shape=[shape_bf16, shape_bf16, shape, shape],
        scratch_shapes=[
            pltpu.VMEM((2 * hp, PEER_TOPK, tt), F32),
            pltpu.VMEM((hp, CAND_ROWS, tt), F32),
            pltpu.VMEM((hp, PEER_TOPK, tt), F32),
            pltpu.VMEM((2 * hp, N_KEYS, tt), jnp.int32),
            pltpu.VMEM((hp, CAND_ROWS, tt), jnp.int32),
        ],
        compiler_params=_cparams(("arbitrary", "arbitrary")),
        name="peer_topk",
    )(qp, sub_keys)


_ERF_A = (-2.72614225801306e-10, 2.77068142495902e-08, -2.10102402082508e-06, -5.69250639462346e-05,
          -7.34990630326855e-04, -2.95459980854025e-03, -1.60960333262415e-02)
_ERF_B = (-1.45660718464996e-05, -2.13374055278905e-04, -1.68282697438203e-03, -7.37332916720468e-03,
          -1.42647390514189e-02)


def _gelu(x):
    dt = x.dtype
    lim = 4.0 * math.sqrt(2.0)
    xc = jnp.clip(x, -lim, lim)
    t = xc * xc
    deg_a = len(_ERF_A) - 1
    deg_b = len(_ERF_B) - 1
    a2 = [c * (0.5 / math.sqrt(2.0)) / 2.0 ** (deg_a - i) for i, c in enumerate(_ERF_A)]
    b2 = [c / 2.0 ** (deg_b - i) for i, c in enumerate(_ERF_B)]
    a2 = [c / b2[0] for c in a2]
    b2 = [c / b2[0] for c in b2]
    a = jnp.asarray(a2[0], dt)
    for c in a2[1:]:
        a = a * t + jnp.asarray(c, dt)
    b = t + jnp.asarray(b2[1], dt)
    for c in b2[2:]:
        b = b * t + jnp.asarray(c, dt)
    return x * (0.5 + xc * (a / b))


def _dense_kernel(hq_ref, u_ref, vt_ref, r2_ref, w2_ref, k1_ref, w1_ref, o_ref, h_scr, a_scr, *, n_eb):
    s = pl.program_id(0)
    eb = u_ref.shape[0]
    tt = hq_ref.shape[0]
    n_i1 = eb // N_KEYS

    @pl.when(s == 0)
    def _():
        h_scr[...] = jnp.zeros(h_scr.shape, BF16)
        a_scr[...] = jnp.zeros(a_scr.shape, BF16)

    @pl.when(jnp.maximum(s - 2, 0) % n_eb == 0)
    def _():
        o_ref[...] = jnp.zeros(o_ref.shape, F32)

    def rows16(ref, h, ii, cs):
        row = ref[h, ii:ii + 1, cs]
        return pltpu.bitcast(jnp.broadcast_to(row, (8, PEER_LANES)), BF16)

    for c in range(tt // PEER_LANES):
        cs = slice(c * PEER_LANES, (c + 1) * PEER_LANES)

        o_ref[:, cs] += jnp.dot(vt_ref[...], a_scr[:, cs], preferred_element_type=F32)

        for ii in range(n_i1):
            rows = slice(ii * N_KEYS, (ii + 1) * N_KEYS)
            g = None
            for h in range(PEER_HEADS):
                k1b = jnp.concatenate([rows16(k1_ref, h, ii, cs)] * (N_KEYS // 16), axis=0)
                w1b = jnp.concatenate([rows16(w1_ref, h, ii, cs)] * (N_KEYS // 16), axis=0)
                keep = r2_ref[h, :, cs] < k1b
                gate = jnp.where(keep, w2_ref[h, :, cs] * w1b, 0.0)
                g = gate if g is None else g + gate
            a_scr[rows, cs] = _gelu(h_scr[rows, cs]) * g

        h_scr[:, cs] = _nt_dot(u_ref[...], hq_ref[cs, :]).astype(BF16)


def _dense_call(hq, u_tab, vt_tab, layer, r2, w2, k1, w1):
    n = hq.shape[0]
    n_exp = u_tab.shape[1]
    tt = PEER_TOKENS
    eb = PEER_EXPERTS
    n_eb = n_exp // eb
    steps = (n // tt) * n_eb

    def blk(s, lag):
        return jnp.clip(s - lag, 0, steps - 1)

    tab = pl.BlockSpec((PEER_HEADS, N_KEYS, tt), lambda s: (0, 0, blk(s, 1) // n_eb))
    row = pl.BlockSpec((PEER_HEADS, eb // N_KEYS, tt), lambda s: (0, blk(s, 1) % n_eb, blk(s, 1) // n_eb))
    return pl.pallas_call(
        functools.partial(_dense_kernel, n_eb=n_eb),
        grid=(steps + 2,),
        in_specs=[
            pl.BlockSpec((tt, D_MODEL), lambda s: (blk(s, 0) // n_eb, 0)),
            pl.BlockSpec((None, eb, D_MODEL), lambda s: (layer, blk(s, 0) % n_eb, 0)),
            pl.BlockSpec((None, D_MODEL, eb), lambda s: (layer, 0, blk(s, 2) % n_eb)),
            tab, tab, row, row,
        ],
        out_specs=pl.BlockSpec((D_MODEL, tt), lambda s: (0, blk(s, 2) // n_eb)),
        out_shape=jax.ShapeDtypeStruct((D_MODEL, n), F32),
        scratch_shapes=[pltpu.VMEM((eb, tt), BF16), pltpu.VMEM((eb, tt), BF16)],
        compiler_params=_cparams(("arbitrary",)),
        name="peer_dense",
    )(hq, u_tab, vt_tab, r2, w2, k1, w1)


def _ln2_kernel(x1_ref, ft_ref, mod_ref, lg_ref, lb_ref, o_ref):
    g2 = mod_ref[0, 5:6, :]
    f = ft_ref[...].T
    o_ref[...] = _layer_norm(DEEPNORM_ALPHA * x1_ref[...] + g2 * f, lg_ref[...], lb_ref[...])


def _ln2_call(x1, ft, mods, ln_g, ln_b, ctx_blocks, latent_only):
    n = x1.shape[0]
    tb = ROW_BLOCK
    skip = ctx_blocks if latent_only else 0
    return pl.pallas_call(
        _ln2_kernel,
        grid=(n // tb,),
        in_specs=[
            pl.BlockSpec((tb, D_MODEL), lambda i: (i, 0)),
            pl.BlockSpec((D_MODEL, tb), lambda i: (0, i)),
            pl.BlockSpec((1, N_MOD, D_MODEL), lambda i: (jnp.where(i < ctx_blocks, 0, 1), 0, 0)),
            pl.BlockSpec((1, D_MODEL), lambda i: (0, 0)),
            pl.BlockSpec((1, D_MODEL), lambda i: (0, 0)),
        ],
        out_specs=pl.BlockSpec((tb, D_MODEL), lambda i: (jnp.maximum(i - skip, 0), 0)),
        out_shape=jax.ShapeDtypeStruct((n - skip * tb, D_MODEL), F32),
        compiler_params=_cparams(("arbitrary",)),
        name="ln2",
    )(x1, ft, mods, ln_g, ln_b)


def _rope_tables(ctx_len, s_len):
    t = np.arange(s_len)
    pos = np.stack([t // GRID_W, t % GRID_W], axis=-1).astype(np.float64)
    quarter = HEAD_DIM // 4
    inv_freq = ROPE_THETA ** (-np.arange(quarter, dtype=np.float64) / quarter)
    ang = pos[:, :, None] * inv_freq
    cos = np.repeat(np.cos(ang), 2, axis=1).reshape(s_len, HEAD_DIM)
    sin = np.repeat(np.sin(ang), 2, axis=1).reshape(s_len, HEAD_DIM)
    first_half = (np.arange(HEAD_DIM) % 32) < 16
    sa = np.where(first_half, -sin, 0.0)
    sb = np.where(first_half, 0.0, sin)

    def full(tab, fill):
        tab = np.concatenate([np.full((ctx_len, HEAD_DIM), fill), tab], axis=0)
        return jnp.asarray(np.tile(tab, (1, 2)), F32)

    return full(cos, 1.0), full(sa, 0.0), full(sb, 0.0)


def kernel(x, c, ctx, c_ctx, w_mod, b_mod, w_in, q_norm, k_norm, conv_w, sink, w_branch, w_out,
           ln1_g, ln1_b, w_pq, sub_keys, u_tab, v_tab, ln2_g, ln2_b):
    bsz, s_len, d = x.shape
    ctx_len = ctx.shape[1]
    depth = w_mod.shape[0]
    assert bsz == 1 and d == D_MODEL and depth == DEPTH
    assert ctx_len % ROW_BLOCK == 0
    assert (ctx_len + s_len) % PEER_TOKENS == 0 and (ctx_len + s_len) % INPROJ_ROWS == 0
    ctx_blocks = ctx_len // ROW_BLOCK

    x_all = jnp.concatenate([ctx[0], x[0]], axis=0)
    cos_t, sa_t, sb_t = _rope_tables(ctx_len, s_len)

    cc = jnp.zeros((8, D_MODEL), F32).at[0].set(c_ctx).at[1].set(c[0])
    mod_all = _mod_call(cc, w_mod.astype(BF16), b_mod[:, None, :])
    u_all = u_tab.astype(BF16)
    vt_all = jnp.swapaxes(v_tab, 1, 2).astype(BF16)

    for layer in range(depth):
        mods = mod_all[layer, 0:2].reshape(2, N_MOD, D_MODEL)
        proj = _inproj_call(x_all, mods, w_in[layer].astype(BF16), ctx_len)
        qg = jnp.tile(q_norm[layer], 2)[None, :]
        kg = jnp.tile(k_norm[layer], 2)[None, :]
        qa, ka, va, qc, kc, vc = _prep_call(proj, cos_t, sa_t, sb_t, qg, kg)
        ya = _attn_a_call(qa, ka, va, ctx_len)
        yc = _attn_c_call(sink[layer], qc, kc, vc, ctx_len)
        x1, hq, qp = _merge_call(x_all, mods, ya, yc, proj, conv_w[layer],
                                 w_branch[layer].astype(BF16), w_out[layer].astype(BF16),
                                 ln1_g[layer][None, :], ln1_b[layer][None, :],
                                 w_pq[layer].astype(BF16), ctx_blocks)
        r2, w2, k1, w1 = _topk_call(qp, sub_keys[layer].astype(BF16))
        ft = _dense_call(hq, u_all, vt_all, layer, r2, w2, k1, w1)
        x_all = _ln2_call(x1, ft, mods, ln2_g[layer][None, :], ln2_b[layer][None, :], ctx_blocks,
                          latent_only=(layer == depth - 1))

    return x_all[None]
```

```python
import functools
import math

import jax
import jax.numpy as jnp
import numpy as np
from jax import lax
from jax.experimental import pallas as pl
from jax.experimental.pallas import tpu as pltpu

F32 = jnp.float32
BF16 = jnp.bfloat16

D_MODEL = 1024
HEAD_DIM = 64
N_HEADS = 8
N_KV = 2
GROUP = N_HEADS // N_KV
GRID_W = 64
WINDOW = 128
CONV_W = 3
B_WIDTH = 512
N_MOD = 6
ROPE_THETA = 10000.0
PEER_HEADS = 8
N_KEYS = 128
PEER_TOPK = 16
PEER_QDIM = 256
DEPTH = 2
IN_COLS = 6144
DEEPNORM_ALPHA = (2 * DEPTH) ** 0.25
LN_EPS = 1e-6
RMS_EPS = 1e-6
NEG_INF = -1e30
ATTN_SCALE = HEAD_DIM ** -0.5

ROW_BLOCK = 256
INPROJ_ROWS = 1280
KEY_CHUNK = 1024
BOUND_SLACK = 1.0 + 2.0 ** -6
SAFE_ROW_SUM = 1e-30
PEER_TOKENS = 1280
PEER_LANES = 256
TOPK_HEADS = 4
PEER_EXPERTS = 1024
VMEM_LIMIT = 56 * 1024 * 1024

CAND_COUNTS = tuple(PEER_TOPK // (j + 1) for j in range(PEER_TOPK))
N_CAND = sum(CAND_COUNTS)
CAND_ROWS = 56


def _cparams(sem):
    return pltpu.CompilerParams(dimension_semantics=sem, vmem_limit_bytes=VMEM_LIMIT)


def _nt_dot(a, b):
    return lax.dot_general(a, b, (((1,), (1,)), ((), ())), preferred_element_type=F32)


def _layer_norm(r, g, b):
    mu = jnp.mean(r, axis=-1, keepdims=True)
    rc = r - mu
    var = jnp.mean(rc * rc, axis=-1, keepdims=True)
    return rc * lax.rsqrt(var + LN_EPS) * g + b


def _mod_kernel(c_ref, w_ref, b_ref, o_ref):
    c = c_ref[...]
    h = (c * jax.nn.sigmoid(c)).astype(BF16)
    o_ref[0] = jnp.dot(h, w_ref[0], preferred_element_type=F32) + b_ref[0]


def _mod_call(cc, w_mod, b_mod):
    depth = w_mod.shape[0]
    cols = w_mod.shape[2]
    tn = 1024
    return pl.pallas_call(
        _mod_kernel,
        grid=(depth, cols // tn),
        in_specs=[
            pl.BlockSpec((8, D_MODEL), lambda l, j: (0, 0)),
            pl.BlockSpec((1, D_MODEL, tn), lambda l, j: (l, 0, j)),
            pl.BlockSpec((1, 1, tn), lambda l, j: (l, 0, j)),
        ],
        out_specs=pl.BlockSpec((1, 8, tn), lambda l, j: (l, 0, j)),
        out_shape=jax.ShapeDtypeStruct((depth, 8, cols), F32),
        compiler_params=_cparams(("arbitrary", "arbitrary")),
        name="mod",
    )(cc, w_mod, b_mod)


def _inproj_kernel(x_ref, mod_ref, w_ref, o_ref, hm_scr, *, ctx_len):
    tb = x_ref.shape[0]

    @pl.when(pl.program_id(1) == 0)
    def _():
        row = pl.program_id(0) * tb + lax.broadcasted_iota(jnp.int32, (tb, 1), 0)
        is_ctx = row < ctx_len
        sh = jnp.where(is_ctx, mod_ref[0, 0:1, :], mod_ref[1, 0:1, :])
        sc = jnp.where(is_ctx, mod_ref[0, 1:2, :], mod_ref[1, 1:2, :])
        hm_scr[...] = (x_ref[...] * (1.0 + sc) + sh).astype(BF16)

    o_ref[...] = jnp.dot(hm_scr[...], w_ref[...], preferred_element_type=F32)


def _inproj_call(x_all, mods, w_in, ctx_len):
    n = x_all.shape[0]
    tn = 1024
    tb = INPROJ_ROWS
    return pl.pallas_call(
        functools.partial(_inproj_kernel, ctx_len=ctx_len),
        grid=(n // tb, IN_COLS // tn),
        in_specs=[
            pl.BlockSpec((tb, D_MODEL), lambda i, j: (i, 0)),
            pl.BlockSpec((2, N_MOD, D_MODEL), lambda i, j: (0, 0, 0)),
            pl.BlockSpec((D_MODEL, tn), lambda i, j: (0, j)),
        ],
        out_specs=pl.BlockSpec((tb, tn), lambda i, j: (i, j)),
        out_shape=jax.ShapeDtypeStruct((n, IN_COLS), F32),
        scratch_shapes=[pltpu.VMEM((tb, D_MODEL), BF16)],
        compiler_params=_cparams(("arbitrary", "arbitrary")),
        name="inproj",
    )(x_all, mods, w_in)


def _prep_kernel(p_ref, cos_ref, sa_ref, sb_ref, qg_ref, kg_ref,
                 qa_o, ka_o, va_o, qc_o, kc_o, vc_o):
    x = p_ref[...]
    cos = cos_ref[...]
    sa = sa_ref[...]
    sb = sb_ref[...]

    def tile(t, width):
        reps = width // 128
        return t if reps == 1 else jnp.concatenate([t] * reps, axis=1)

    def rope(y):
        w = y.shape[1]
        return (y * tile(cos, w)
                + pltpu.roll(y, w - 16, 1) * tile(sa, w)
                + pltpu.roll(y, 16, 1) * tile(sb, w))

    def inv_rms(xh):
        return lax.rsqrt(jnp.mean(xh * xh, axis=-1, keepdims=True) + RMS_EPS)

    qa = x[:, 0:512]
    ka = x[:, 512:640]
    va = x[:, 640:768]
    qc = x[:, 768:1280]
    kc = x[:, 1280:1408]
    vc = x[:, 1408:1536]

    qa_r = rope(qa * tile(qg_ref[...], 512))
    ka_r = rope(ka * kg_ref[...])
    qc_r = rope(qc)
    kc_r = rope(kc)

    for h in range(N_HEADS):
        sl = slice(h * HEAD_DIM, (h + 1) * HEAD_DIM)
        qa_o[h, :, 0:HEAD_DIM] = (qa_r[:, sl] * (inv_rms(qa[:, sl]) * ATTN_SCALE)).astype(BF16)
        qa_o[h, :, HEAD_DIM:2 * HEAD_DIM] = jnp.zeros((x.shape[0], HEAD_DIM), BF16)
        qc_o[h] = (qc_r[:, sl] * ATTN_SCALE).astype(BF16)
    lane = lax.broadcasted_iota(jnp.int32, (x.shape[0], HEAD_DIM), 1)
    ones_col = jnp.where(lane == 0, 1.0, 0.0).astype(BF16)
    for h in range(N_KV):
        sl = slice(h * HEAD_DIM, (h + 1) * HEAD_DIM)
        ka_o[h, :, 0:HEAD_DIM] = (ka_r[:, sl] * inv_rms(ka[:, sl])).astype(BF16)
        ka_o[h, :, HEAD_DIM:2 * HEAD_DIM] = ones_col
        va_o[h, :, 0:HEAD_DIM] = va[:, sl].astype(BF16)
        va_o[h, :, HEAD_DIM:2 * HEAD_DIM] = ones_col
        kc_o[h] = kc_r[:, sl].astype(BF16)
        vc_o[h, :, 0:HEAD_DIM] = vc[:, sl].astype(BF16)
        vc_o[h, :, HEAD_DIM:2 * HEAD_DIM] = ones_col


def _prep_call(proj, cos_t, sa_t, sb_t, qg, kg):
    n = proj.shape[0]
    tb = ROW_BLOCK
    tab = pl.BlockSpec((tb, 128), lambda i: (i, 0))
    gain = pl.BlockSpec((1, 128), lambda i: (0, 0))

    def hm(nh, width=HEAD_DIM):
        return pl.BlockSpec((nh, tb, width), lambda i: (0, i, 0))

    def hshape(nh, width=HEAD_DIM):
        return jax.ShapeDtypeStruct((nh, n, width), BF16)

    return pl.pallas_call(
        _prep_kernel,
        grid=(n // tb,),
        in_specs=[pl.BlockSpec((tb, 1536), lambda i: (i, 0)), tab, tab, tab, gain, gain],
        out_specs=[hm(N_HEADS, 2 * HEAD_DIM), hm(N_KV, 2 * HEAD_DIM), hm(N_KV, 2 * HEAD_DIM),
                   hm(N_HEADS), hm(N_KV), hm(N_KV, 2 * HEAD_DIM)],
        out_shape=[hshape(N_HEADS, 2 * HEAD_DIM), hshape(N_KV, 2 * HEAD_DIM), hshape(N_KV, 2 * HEAD_DIM),
                   hshape(N_HEADS), hshape(N_KV), hshape(N_KV, 2 * HEAD_DIM)],
        compiler_params=_cparams(("arbitrary",)),
        name="prep",
    )(proj, cos_t, sa_t, sb_t, qg, kg)


def _store_heads(o_ref, o, tq):
    for g in range(GROUP):
        o_ref[:, g * HEAD_DIM:(g + 1) * HEAD_DIM] = o[g * tq:(g + 1) * tq, :].astype(BF16)


def _attn_a_kernel(q_ref, k_ref, v_ref, o_ref, s_scr, mx_scr, m_scr, acc_scr, kmax_scr, *,
                   ctx_len, ctx_blocks, lat_chunks, unroll):
    i = pl.program_id(1)
    tq = q_ref.shape[1]
    rows = GROUP * tq
    key_chunk = s_scr.shape[2]
    q0 = q_ref[...].reshape(rows, 2 * HEAD_DIM)

    @pl.when(i == 0)
    def _():
        kk = k_ref[0, :, 0:HEAD_DIM].astype(F32)
        k_norm2 = jnp.max(jnp.sum(kk * kk, axis=-1, keepdims=True), axis=0, keepdims=True)
        kmax_scr[...] = jnp.broadcast_to(jnp.sqrt(k_norm2), kmax_scr.shape)

    def chunk_off(c):
        return pl.multiple_of(ctx_len + c * key_chunk, 128)

    def run(q, online):
        def scores(slot, off, width):
            s = _nt_dot(q, k_ref[0, pl.ds(off, width), :])
            s_scr[slot, :, 0:width] = s
            if online:
                mx_scr[slot] = jnp.broadcast_to(jnp.max(s, axis=-1, keepdims=True), (rows, 128))

        def update(slot, off, width):
            s = s_scr[slot, :, 0:width]
            v = v_ref[0, pl.ds(off, width), :]
            if online:
                m_old = m_scr[...]
                m_new = jnp.maximum(m_old, mx_scr[slot])
                p = jnp.exp(s - jnp.concatenate([m_new] * (width // 128), axis=1))
                acc_scr[...] = jnp.exp(m_old - m_new) * acc_scr[...] + jnp.dot(
                    p.astype(BF16), v, preferred_element_type=F32)
                m_scr[...] = m_new
            else:
                acc_scr[...] += jnp.dot(jnp.exp(s).astype(BF16), v, preferred_element_type=F32)

        if online:
            m_scr[...] = jnp.full(m_scr.shape, -jnp.inf, F32)
        acc_scr[...] = jnp.zeros(acc_scr.shape, F32)
        scores(0, 0, ctx_len)
        update(0, 0, ctx_len)

        @pl.when(i >= ctx_blocks)
        def _():
            scores(0, chunk_off(0), key_chunk)

            def body(it, carry):
                for u in range(0, unroll, 2):
                    c = unroll * it + u
                    scores(1, chunk_off(c + 1), key_chunk)
                    update(0, chunk_off(c), key_chunk)
                    scores(0, chunk_off(jnp.minimum(c + 2, lat_chunks - 1)), key_chunk)
                    update(1, chunk_off(c + 1), key_chunk)
                return carry

            lax.fori_loop(0, lat_chunks // unroll, body, 0)

        acc = acc_scr[...]
        _store_heads(o_ref, acc[:, 0:HEAD_DIM] / acc[:, HEAD_DIM:HEAD_DIM + 1], tq)

    qf = q0.astype(F32)
    q_norm = jnp.sqrt(jnp.sum(qf * qf, axis=-1, keepdims=True))
    bound = (q_norm * kmax_scr[0:1, 0:1] * BOUND_SLACK).astype(BF16).astype(F32)
    shift_lane = lax.broadcasted_iota(jnp.int32, qf.shape, 1) == HEAD_DIM
    run(jnp.where(shift_lane, -bound, qf).astype(BF16), online=False)

    @pl.when(jnp.logical_not(jnp.min(acc_scr[:, HEAD_DIM:HEAD_DIM + 1]) > SAFE_ROW_SUM))
    def _():
        run(q0, online=True)


def _attn_a_call(q, k, v, ctx_len):
    n = q.shape[1]
    tq = ROW_BLOCK
    s_len = n - ctx_len
    key_chunk = KEY_CHUNK if s_len % (2 * KEY_CHUNK) == 0 else KEY_CHUNK // 2
    lat_chunks = s_len // key_chunk
    assert s_len % key_chunk == 0 and lat_chunks % 2 == 0
    unroll = 4 if lat_chunks % 4 == 0 else 2
    kern = functools.partial(_attn_a_kernel, ctx_len=ctx_len, ctx_blocks=ctx_len // tq, lat_chunks=lat_chunks,
                             unroll=unroll)
    return pl.pallas_call(
        kern,
        grid=(N_KV, n // tq),
        in_specs=[
            pl.BlockSpec((GROUP, tq, 2 * HEAD_DIM), lambda j, i: (j, i, 0)),
            pl.BlockSpec((1, n, 2 * HEAD_DIM), lambda j, i: (j, 0, 0)),
            pl.BlockSpec((1, n, 2 * HEAD_DIM), lambda j, i: (j, 0, 0)),
        ],
        out_specs=pl.BlockSpec((tq, GROUP * HEAD_DIM), lambda j, i: (i, j)),
        out_shape=jax.ShapeDtypeStruct((n, N_HEADS * HEAD_DIM), BF16),
        scratch_shapes=[
            pltpu.VMEM((2, GROUP * tq, key_chunk), F32),
            pltpu.VMEM((2, GROUP * tq, 128), F32),
            pltpu.VMEM((GROUP * tq, 128), F32),
            pltpu.VMEM((GROUP * tq, 2 * HEAD_DIM), F32),
            pltpu.VMEM((8, 128), F32),
        ],
        compiler_params=_cparams(("arbitrary", "arbitrary")),
        name="attn_global",
    )(q, k, v)


def _attn_c_kernel(sink_ref, q_ref, k_ref, v_ref, o_ref, *, ctx_len, ctx_blocks, s_len):
    j = pl.program_id(0)
    i = pl.program_id(1)
    tq = q_ref.shape[1]
    rows = GROUP * tq
    n_loc = tq + 2 * WINDOW
    q = q_ref[...].reshape(rows, HEAD_DIM)

    p0 = i * tq - ctx_len
    ws = pl.multiple_of(jnp.clip(p0 - WINDOW, 0, s_len - n_loc), 128)
    k_loc = k_ref[0, pl.ds(ctx_len + ws, n_loc), :]
    v_loc = v_ref[0, pl.ds(ctx_len + ws, n_loc), :]
    k_ctx = k_ref[0, 0:ctx_len, :]
    v_ctx = v_ref[0, 0:ctx_len, :]

    dist = (lax.broadcasted_iota(jnp.int32, (tq, n_loc), 0) - lax.broadcasted_iota(jnp.int32, (tq, n_loc), 1)
            + (p0 - ws))
    valid = (jnp.abs(dist) <= WINDOW) & (i >= ctx_blocks)
    s_loc = jnp.where(valid[None], _nt_dot(q, k_loc).reshape(GROUP, tq, n_loc), NEG_INF).reshape(rows, n_loc)
    s_ctx = _nt_dot(q, k_ctx)

    grp = lax.broadcasted_iota(jnp.int32, (rows, 1), 0) // tq
    sink = jnp.full((rows, 1), sink_ref[j * GROUP + GROUP - 1], F32)
    for g in range(GROUP - 1):
        sink = jnp.where(grp == g, sink_ref[j * GROUP + g], sink)

    m = jnp.maximum(jnp.maximum(jnp.max(s_loc, axis=-1, keepdims=True),
                                jnp.max(s_ctx, axis=-1, keepdims=True)), sink)
    acc = (jnp.dot(jnp.exp(s_loc - m).astype(BF16), v_loc, preferred_element_type=F32)
           + jnp.dot(jnp.exp(s_ctx - m).astype(BF16), v_ctx, preferred_element_type=F32))
    o = acc[:, 0:HEAD_DIM] / (acc[:, HEAD_DIM:HEAD_DIM + 1] + jnp.exp(sink - m))
    _store_heads(o_ref, o, tq)


def _attn_c_call(sink, q, k, v, ctx_len):
    n = q.shape[1]
    tq = ROW_BLOCK
    kern = functools.partial(_attn_c_kernel, ctx_len=ctx_len, ctx_blocks=ctx_len // tq, s_len=n - ctx_len)
    return pl.pallas_call(
        kern,
        grid=(N_KV, n // tq),
        in_specs=[
            pl.BlockSpec(memory_space=pltpu.SMEM),
            pl.BlockSpec((GROUP, tq, HEAD_DIM), lambda j, i: (j, i, 0)),
            pl.BlockSpec((1, n, HEAD_DIM), lambda j, i: (j, 0, 0)),
            pl.BlockSpec((1, n, 2 * HEAD_DIM), lambda j, i: (j, 0, 0)),
        ],
        out_specs=pl.BlockSpec((tq, GROUP * HEAD_DIM), lambda j, i: (i, j)),
        out_shape=jax.ShapeDtypeStruct((n, N_HEADS * HEAD_DIM), BF16),
        compiler_params=_cparams(("arbitrary", "arbitrary")),
        name="attn_window",
    )(sink, q, k, v)


def _merge_kernel(x_ref, mod_ref, ya_ref, yc_ref, xb_ref, gb_ref, gc_ref,
                  xbp_ref, gcp_ref, xbn_ref, gcn_ref, gl_ref, cw_ref, wb_ref, wo_ref,
                  lg_ref, lb_ref, wpq_ref, x1_ref, hq_ref, qp_ref, *, n_blocks, ctx_blocks):
    i = pl.program_id(0)
    tb = x_ref.shape[0]

    z = gc_ref[...] * xb_ref[...]
    z_before = gcp_ref[7:8, :] * xbp_ref[7:8, :]
    z_after = gcn_ref[0:1, :] * xbn_ref[0:1, :]
    seq_start = (i == 0) | (i == ctx_blocks)
    seq_end = (i == ctx_blocks - 1) | (i == n_blocks - 1)
    z_before = jnp.where(seq_start, 0.0, z_before)
    z_after = jnp.where(seq_end, 0.0, z_after)
    rid = lax.broadcasted_iota(jnp.int32, z.shape, 0)
    z_prev = jnp.where(rid == 0, z_before, pltpu.roll(z, 1, 0))
    z_next = jnp.where(rid == tb - 1, z_after, pltpu.roll(z, tb - 1, 0))
    cw = cw_ref[...]
    yb = gb_ref[...] * (cw[0:1, :] * z_prev + cw[1:2, :] * z + cw[2:3, :] * z_next)

    pa = jnp.dot(ya_ref[...], wb_ref[0], preferred_element_type=F32)
    pb = jnp.dot(yb.astype(BF16), wb_ref[1], preferred_element_type=F32)
    pc = jnp.dot(yc_ref[...], wb_ref[2], preferred_element_type=F32)
    gates = jax.nn.sigmoid(gl_ref[...])
    m = (gates[:, 0:D_MODEL] * pa + gates[:, D_MODEL:2 * D_MODEL] * pb
         + gates[:, 2 * D_MODEL:3 * D_MODEL] * pc)
    y = jnp.dot(m.astype(BF16), wo_ref[...], preferred_element_type=F32)

    g1 = mod_ref[0, 2:3, :]
    sh2 = mod_ref[0, 3:4, :]
    sc2 = mod_ref[0, 4:5, :]
    x1 = _layer_norm(DEEPNORM_ALPHA * x_ref[...] + g1 * y, lg_ref[...], lb_ref[...])
    x1_ref[...] = x1
    hq = (x1 * (1.0 + sc2) + sh2).astype(BF16)
    hq_ref[...] = hq
    qp_ref[...] = jnp.dot(hq, wpq_ref[...], preferred_element_type=F32).astype(BF16)


def _merge_call(x_all, mods, ya, yc, proj, conv_w, w_branch, w_out, ln_g, ln_b, w_pq, ctx_blocks):
    n = x_all.shape[0]
    tb = ROW_BLOCK
    nb = n // tb
    halo = tb // 8
    last8 = n // 8 - 1
    kern = functools.partial(_merge_kernel, n_blocks=nb, ctx_blocks=ctx_blocks)
    qcols = PEER_HEADS * PEER_QDIM

    def full(shape):
        return pl.BlockSpec(shape, lambda i: (0,) * len(shape))

    def cols512(c):
        return pl.BlockSpec((tb, B_WIDTH), lambda i: (i, c))

    def before(c):
        return pl.BlockSpec((8, B_WIDTH), lambda i: (jnp.maximum(i * halo - 1, 0), c))

    def after(c):
        return pl.BlockSpec((8, B_WIDTH), lambda i: (jnp.minimum((i + 1) * halo, last8), c))

    return pl.pallas_call(
        kern,
        grid=(nb,),
        in_specs=[
            pl.BlockSpec((tb, D_MODEL), lambda i: (i, 0)),
            pl.BlockSpec((1, N_MOD, D_MODEL), lambda i: (jnp.where(i < ctx_blocks, 0, 1), 0, 0)),
            pl.BlockSpec((tb, N_HEADS * HEAD_DIM), lambda i: (i, 0)),
            pl.BlockSpec((tb, N_HEADS * HEAD_DIM), lambda i: (i, 0)),
            cols512(3), cols512(4), cols512(5),
            before(3), before(5), after(3), after(5),
            pl.BlockSpec((tb, 3 * D_MODEL), lambda i: (i, 1)),
            full((CONV_W, B_WIDTH)),
            full((3, B_WIDTH, D_MODEL)),
            full((D_MODEL, D_MODEL)),
            full((1, D_MODEL)), full((1, D_MODEL)),
            full((D_MODEL, qcols)),
        ],
        out_specs=[
            pl.BlockSpec((tb, D_MODEL), lambda i: (i, 0)),
            pl.BlockSpec((tb, D_MODEL), lambda i: (i, 0)),
            pl.BlockSpec((tb, qcols), lambda i: (i, 0)),
        ],
        out_shape=[
            jax.ShapeDtypeStruct((n, D_MODEL), F32),
            jax.ShapeDtypeStruct((n, D_MODEL), BF16),
            jax.ShapeDtypeStruct((n, qcols), BF16),
        ],
        compiler_params=_cparams(("arbitrary",)),
        name="merge",
    )(x_all, mods, ya, yc, proj, proj, proj, proj, proj, proj, proj, proj,
      conv_w, w_branch, w_out, ln_g, ln_b, w_pq)


def _extract_rounds(chains, rounds, on_round, tie_break):
    rows = chains[0].shape[0]
    ridx = lax.broadcasted_iota(jnp.int32, chains[0].shape, 0)
    chains = list(chains)
    ranks = [jnp.full(c.shape, rounds, jnp.int32) for c in chains]
    for r in range(rounds):
        for i, s in enumerate(chains):
            m = jnp.max(s, axis=0, keepdims=True)
            if tie_break:
                first = jnp.min(jnp.where(s == m, ridx, rows), axis=0, keepdims=True)
                hit = ridx == first
            else:
                hit = s == m
            ranks[i] = jnp.where(hit, r, ranks[i])
            chains[i] = jnp.where(hit, -jnp.inf, s)
            on_round(i, r, m)
    return ranks


def _extract_topk(chains, rounds, on_round, rank_ref):
    ranks = _extract_rounds(chains, rounds, on_round, tie_break=False)
    most = jnp.float32(0.0)
    for i, rank in enumerate(ranks):
        rank_ref[i] = rank
        most = jnp.maximum(most, jnp.max(jnp.sum((rank < rounds).astype(F32), axis=0, keepdims=True)))

    @pl.when(most > rounds)
    def _():
        for i, rank in enumerate(_extract_rounds(chains, rounds, on_round, tie_break=True)):
            rank_ref[i] = rank


def _topk_kernel(qp_ref, sk_ref, r2_o, w2_o, k1_o, w1_o, tv_scr, cand_scr, kj_scr, rank_scr, crank_scr):
    tt = qp_ref.shape[0]
    half = PEER_QDIM // 2
    q = qp_ref[...]
    heads = range(TOPK_HEADS)

    scores = [_nt_dot(sk_ref[h, p], q[:, (2 * h + p) * half:(2 * h + p + 1) * half])
              for h in heads for p in range(2)]

    def keep(i, r, m):
        tv_scr[i, r:r + 1, :] = m

    _extract_topk(scores, PEER_TOPK, keep, rank_scr)

    for h in heads:
        tv1 = tv_scr[2 * h]
        tv2 = tv_scr[2 * h + 1]
        off = 0
        for j, cnt in enumerate(CAND_COUNTS):
            cand_scr[h, off:off + cnt, :] = tv1[j:j + 1, :] + tv2[0:cnt, :]
            off += cnt
        cand_scr[h, N_CAND:CAND_ROWS, :] = jnp.full((CAND_ROWS - N_CAND, tt), -jnp.inf, F32)
    cands = [cand_scr[h] for h in heads]
    _extract_topk(cands, PEER_TOPK, lambda i, r, m: None, crank_scr)

    def twice_bf16(x):
        hi = lax.bitcast_convert_type(x.astype(BF16).astype(F32), jnp.uint32)
        return lax.bitcast_convert_type(hi | (hi >> 16), F32)

    for h in heads:
        cand = cands[h]
        chosen = crank_scr[h] < PEER_TOPK
        cmax = cand[0:1, :]
        z = jnp.sum(jnp.where(chosen, jnp.exp(cand - cmax), 0.0), axis=0, keepdims=True)
        cnt_f = chosen.astype(F32)
        off = 0
        for j, cnt in enumerate(CAND_COUNTS):
            kj_scr[h, j:j + 1, :] = jnp.sum(cnt_f[off:off + cnt, :], axis=0, keepdims=True)
            off += cnt

        rank1 = rank_scr[2 * h]
        rank2 = rank_scr[2 * h + 1]
        k1 = jnp.zeros((N_KEYS, tt), F32)
        for j in range(PEER_TOPK):
            k1 = jnp.where(rank1 == j, kj_scr[h, j:j + 1, :], k1)
        k1_o[h] = twice_bf16(k1)
        w1_o[h] = twice_bf16(
            jnp.where(rank1 < PEER_TOPK, jnp.exp(scores[2 * h] - tv_scr[2 * h, 0:1, :]), 0.0) / z)
        r2_o[h] = rank2.astype(F32).astype(BF16)
        w2_o[h] = jnp.where(rank2 < PEER_TOPK, jnp.exp(scores[2 * h + 1] - tv_scr[2 * h + 1, 0:1, :]),
                            0.0).astype(BF16)


def _topk_call(qp, sub_keys):
    n = qp.shape[0]
    tt = ROW_BLOCK
    hp = TOPK_HEADS
    out = pl.BlockSpec((hp, N_KEYS, tt), lambda t, h: (h, 0, t))
    shape = jax.ShapeDtypeStruct((PEER_HEADS, N_KEYS, n), F32)
    shape_bf16 = jax.ShapeDtypeStruct((PEER_HEADS, N_KEYS, n), BF16)
    return pl.pallas_call(
        _topk_kernel,
        grid=(n // tt, PEER_HEADS // hp),
        in_specs=[
            pl.BlockSpec((tt, hp * PEER_QDIM), lambda t, h: (t, h)),
            pl.BlockSpec((hp, 2, N_KEYS, PEER_QDIM // 2), lambda t, h: (h, 0, 0, 0)),
        ],
        out_specs=[out, out, out, out],
        out_shape=[shape_bf16, shape_bf16, shape, shape],
        scratch_shapes=[
            pltpu.VMEM((2 * hp, PEER_TOPK, tt), F32),
            pltpu.VMEM((hp, CAND_ROWS, tt), F32),
            pltpu.VMEM((hp, PEER_TOPK, tt), F32),
            pltpu.VMEM((2 * hp, N_KEYS, tt), jnp.int32),
            pltpu.VMEM((hp, CAND_ROWS, tt), jnp.int32),
        ],
        compiler_params=_cparams(("arbitrary", "arbitrary")),
        name="peer_topk",
    )(qp, sub_keys)


_ERF_A = (-2.72614225801306e-10, 2.77068142495902e-08, -2.10102402082508e-06, -5.69250639462346e-05,
          -7.34990630326855e-04, -2.95459980854025e-03, -1.60960333262415e-02)
_ERF_B = (-1.45660718464996e-05, -2.13374055278905e-04, -1.68282697438203e-03, -7.37332916720468e-03,
          -1.42647390514189e-02)


def _gelu(x):
    dt = x.dtype
    lim = 4.0 * math.sqrt(2.0)
    xc = jnp.clip(x, -lim, lim)
    t = xc * xc
    deg_a = len(_ERF_A) - 1
    deg_b = len(_ERF_B) - 1
    a2 = [c * (0.5 / math.sqrt(2.0)) / 2.0 ** (deg_a - i) for i, c in enumerate(_ERF_A)]
    b2 = [c / 2.0 ** (deg_b - i) for i, c in enumerate(_ERF_B)]
    a2 = [c / b2[0] for c in a2]
    b2 = [c / b2[0] for c in b2]
    a = jnp.asarray(a2[0], dt)
    for c in a2[1:]:
        a = a * t + jnp.asarray(c, dt)
    b = t + jnp.asarray(b2[1], dt)
    for c in b2[2:]:
        b = b * t + jnp.asarray(c, dt)
    return x * (0.5 + xc * (a / b))


def _dense_kernel(hq_ref, u_ref, vt_ref, r2_ref, w2_ref, k1_ref, w1_ref, o_ref, h_scr, a_scr, *, n_eb):
    s = pl.program_id(0)
    eb = u_ref.shape[0]
    tt = hq_ref.shape[0]
    n_i1 = eb // N_KEYS

    @pl.when(s == 0)
    def _():
        h_scr[...] = jnp.zeros(h_scr.shape, BF16)
        a_scr[...] = jnp.zeros(a_scr.shape, BF16)

    @pl.when(jnp.maximum(s - 2, 0) % n_eb == 0)
    def _():
        o_ref[...] = jnp.zeros(o_ref.shape, F32)

    def rows16(ref, h, ii, cs):
        row = ref[h, ii:ii + 1, cs]
        return pltpu.bitcast(jnp.broadcast_to(row, (8, PEER_LANES)), BF16)

    for c in range(tt // PEER_LANES):
        cs = slice(c * PEER_LANES, (c + 1) * PEER_LANES)

        o_ref[:, cs] += jnp.dot(vt_ref[...], a_scr[:, cs], preferred_element_type=F32)

        for ii in range(n_i1):
            rows = slice(ii * N_KEYS, (ii + 1) * N_KEYS)
            g = None
            for h in range(PEER_HEADS):
                k1b = jnp.concatenate([rows16(k1_ref, h, ii, cs)] * (N_KEYS // 16), axis=0)
                w1b = jnp.concatenate([rows16(w1_ref, h, ii, cs)] * (N_KEYS // 16), axis=0)
                keep = r2_ref[h, :, cs] < k1b
                gate = jnp.where(keep, w2_ref[h, :, cs] * w1b, 0.0)
                g = gate if g is None else g + gate
            a_scr[rows, cs] = _gelu(h_scr[rows, cs]) * g

        h_scr[:, cs] = _nt_dot(u_ref[...], hq_ref[cs, :]).astype(BF16)


def _dense_call(hq, u_tab, vt_tab, layer, r2, w2, k1, w1):
    n = hq.shape[0]
    n_exp = u_tab.shape[1]
    tt = PEER_TOKENS
    eb = PEER_EXPERTS
    n_eb = n_exp // eb
    steps = (n // tt) * n_eb

    def blk(s, lag):
        return jnp.clip(s - lag, 0, steps - 1)

    tab = pl.BlockSpec((PEER_HEADS, N_KEYS, tt), lambda s: (0, 0, blk(s, 1) // n_eb))
    row = pl.BlockSpec((PEER_HEADS, eb // N_KEYS, tt), lambda s: (0, blk(s, 1) % n_eb, blk(s, 1) // n_eb))
    return pl.pallas_call(
        functools.partial(_dense_kernel, n_eb=n_eb),
        grid=(steps + 2,),
        in_specs=[
            pl.BlockSpec((tt, D_MODEL), lambda s: (blk(s, 0) // n_eb, 0)),
            pl.BlockSpec((None, eb, D_MODEL), lambda s: (layer, blk(s, 0) % n_eb, 0)),
            pl.BlockSpec((None, D_MODEL, eb), lambda s: (layer, 0, blk(s, 2) % n_eb)),
            tab, tab, row, row,
        ],
        out_specs=pl.BlockSpec((D_MODEL, tt), lambda s: (0, blk(s, 2) // n_eb)),
        out_shape=jax.ShapeDtypeStruct((D_MODEL, n), F32),
        scratch_shapes=[pltpu.VMEM((eb, tt), BF16), pltpu.VMEM((eb, tt), BF16)],
        compiler_params=_cparams(("arbitrary",)),
        name="peer_dense",
    )(hq, u_tab, vt_tab, r2, w2, k1, w1)


def _ln2_kernel(x1_ref, ft_ref, mod_ref, lg_ref, lb_ref, o_ref):
    g2 = mod_ref[0, 5:6, :]
    f = ft_ref[...].T
    o_ref[...] = _layer_norm(DEEPNORM_ALPHA * x1_ref[...] + g2 * f, lg_ref[...], lb_ref[...])


def _ln2_call(x1, ft, mods, ln_g, ln_b, ctx_blocks, latent_only):
    n = x1.shape[0]
    tb = ROW_BLOCK
    skip = ctx_blocks if latent_only else 0
    return pl.pallas_call(
        _ln2_kernel,
        grid=(n // tb,),
        in_specs=[
            pl.BlockSpec((tb, D_MODEL), lambda i: (i, 0)),
            pl.BlockSpec((D_MODEL, tb), lambda i: (0, i)),
            pl.BlockSpec((1, N_MOD, D_MODEL), lambda i: (jnp.where(i < ctx_blocks, 0, 1), 0, 0)),
            pl.BlockSpec((1, D_MODEL), lambda i: (0, 0)),
            pl.BlockSpec((1, D_MODEL), lambda i: (0, 0)),
        ],
        out_specs=pl.BlockSpec((tb, D_MODEL), lambda i: (jnp.maximum(i - skip, 0), 0)),
        out_shape=jax.ShapeDtypeStruct((n - skip * tb, D_MODEL), F32),
        compiler_params=_cparams(("arbitrary",)),
        name="ln2",
    )(x1, ft, mods, ln_g, ln_b)


def _rope_tables(ctx_len, s_len):
    t = np.arange(s_len)
    pos = np.stack([t // GRID_W, t % GRID_W], axis=-1).astype(np.float64)
    quarter = HEAD_DIM // 4
    inv_freq = ROPE_THETA ** (-np.arange(quarter, dtype=np.float64) / quarter)
    ang = pos[:, :, None] * inv_freq
    cos = np.repeat(np.cos(ang), 2, axis=1).reshape(s_len, HEAD_DIM)
    sin = np.repeat(np.sin(ang), 2, axis=1).reshape(s_len, HEAD_DIM)
    first_half = (np.arange(HEAD_DIM) % 32) < 16
    sa = np.where(first_half, -sin, 0.0)
    sb = np.where(first_half, 0.0, sin)

    def full(tab, fill):
        tab = np.concatenate([np.full((ctx_len, HEAD_DIM), fill), tab], axis=0)
        return jnp.asarray(np.tile(tab, (1, 2)), F32)

    return full(cos, 1.0), full(sa, 0.0), full(sb, 0.0)


def kernel(x, c, ctx, c_ctx, w_mod, b_mod, w_in, q_norm, k_norm, conv_w, sink, w_branch, w_out,
           ln1_g, ln1_b, w_pq, sub_keys, u_tab, v_tab, ln2_g, ln2_b):
    bsz, s_len, d = x.shape
    ctx_len = ctx.shape[1]
    depth = w_mod.shape[0]
    assert bsz == 1 and d == D_MODEL and depth == DEPTH
    assert ctx_len % ROW_BLOCK == 0
    assert (ctx_len + s_len) % PEER_TOKENS == 0 and (ctx_len + s_len) % INPROJ_ROWS == 0
    ctx_blocks = ctx_len // ROW_BLOCK

    x_all = jnp.concatenate([ctx[0], x[0]], axis=0)
    cos_t, sa_t, sb_t = _rope_tables(ctx_len, s_len)

    cc = jnp.zeros((8, D_MODEL), F32).at[0].set(c_ctx).at[1].set(c[0])
    mod_all = _mod_call(cc, w_mod.astype(BF16), b_mod[:, None, :])
    u_all = u_tab.astype(BF16)
    vt_all = jnp.swapaxes(v_tab, 1, 2).astype(BF16)

    for layer in range(depth):
        mods = mod_all[layer, 0:2].reshape(2, N_MOD, D_MODEL)
        proj = _inproj_call(x_all, mods, w_in[layer].astype(BF16), ctx_len)
        qg = jnp.tile(q_norm[layer], 2)[None, :]
        kg = jnp.tile(k_norm[layer], 2)[None, :]
        qa, ka, va, qc, kc, vc = _prep_call(proj, cos_t, sa_t, sb_t, qg, kg)
        ya = _attn_a_call(qa, ka, va, ctx_len)
        yc = _attn_c_call(sink[layer], qc, kc, vc, ctx_len)
        x1, hq, qp = _merge_call(x_all, mods, ya, yc, proj, conv_w[layer],
                                 w_branch[layer].astype(BF16), w_out[layer].astype(BF16),
                                 ln1_g[layer][None, :], ln1_b[layer][None, :],
                                 w_pq[layer].astype(BF16), ctx_blocks)
        r2, w2, k1, w1 = _topk_call(qp, sub_keys[layer].astype(BF16))
        ft = _dense_call(hq, u_all, vt_all, layer, r2, w2, k1, w1)
        x_all = _ln2_call(x1, ft, mods, ln2_g[layer][None, :], ln2_b[layer][None, :], ctx_blocks,
                          latent_only=(layer == depth - 1))

    return x_all[None]
```

```python
import functools
import math

import jax
import jax.numpy as jnp
import numpy as np
from jax import lax
from jax.experimental import pallas as pl
from jax.experimental.pallas import tpu as pltpu

F32 = jnp.float32
BF16 = jnp.bfloat16

D_MODEL = 1024
HEAD_DIM = 64
N_HEADS = 8
N_KV = 2
GROUP = N_HEADS // N_KV
GRID_W = 64
WINDOW = 128
CONV_W = 3
B_WIDTH = 512
N_MOD = 6
ROPE_THETA = 10000.0
PEER_HEADS = 8
N_KEYS = 128
PEER_TOPK = 16
PEER_QDIM = 256
DEPTH = 2
IN_COLS = 6144
DEEPNORM_ALPHA = (2 * DEPTH) ** 0.25
LN_EPS = 1e-6
RMS_EPS = 1e-6
NEG_INF = -1e30
ATTN_SCALE = HEAD_DIM ** -0.5

ROW_BLOCK = 256
INPROJ_ROWS = 1280
KEY_CHUNK = 1024
BOUND_SLACK = 1.0 + 2.0 ** -6
SAFE_ROW_SUM = 1e-30
PEER_TOKENS = 1280
PEER_LANES = 256
TOPK_HEADS = 8
PEER_EXPERTS = 1024
VMEM_LIMIT = 56 * 1024 * 1024

CAND_COUNTS = tuple(PEER_TOPK // (j + 1) for j in range(PEER_TOPK))
N_CAND = sum(CAND_COUNTS)
CAND_ROWS = 56


def _cparams(sem):
    return pltpu.CompilerParams(dimension_semantics=sem, vmem_limit_bytes=VMEM_LIMIT)


def _nt_dot(a, b):
    return lax.dot_general(a, b, (((1,), (1,)), ((), ())), preferred_element_type=F32)


def _layer_norm(r, g, b):
    mu = jnp.mean(r, axis=-1, keepdims=True)
    rc = r - mu
    var = jnp.mean(rc * rc, axis=-1, keepdims=True)
    return rc * lax.rsqrt(var + LN_EPS) * g + b


def _mod_kernel(c_ref, w_ref, b_ref, o_ref):
    c = c_ref[...]
    h = (c * jax.nn.sigmoid(c)).astype(BF16)
    o_ref[0] = jnp.dot(h, w_ref[0], preferred_element_type=F32) + b_ref[0]


def _mod_call(cc, w_mod, b_mod):
    depth = w_mod.shape[0]
    cols = w_mod.shape[2]
    tn = 1024
    return pl.pallas_call(
        _mod_kernel,
        grid=(depth, cols // tn),
        in_specs=[
            pl.BlockSpec((8, D_MODEL), lambda l, j: (0, 0)),
            pl.BlockSpec((1, D_MODEL, tn), lambda l, j: (l, 0, j)),
            pl.BlockSpec((1, 1, tn), lambda l, j: (l, 0, j)),
        ],
        out_specs=pl.BlockSpec((1, 8, tn), lambda l, j: (l, 0, j)),
        out_shape=jax.ShapeDtypeStruct((depth, 8, cols), F32),
        compiler_params=_cparams(("arbitrary", "arbitrary")),
        name="mod",
    )(cc, w_mod, b_mod)


def _inproj_kernel(x_ref, mod_ref, w_ref, o_ref, hm_scr, *, ctx_len):
    tb = x_ref.shape[0]

    @pl.when(pl.program_id(1) == 0)
    def _():
        row = pl.program_id(0) * tb + lax.broadcasted_iota(jnp.int32, (tb, 1), 0)
        is_ctx = row < ctx_len
        sh = jnp.where(is_ctx, mod_ref[0, 0:1, :], mod_ref[1, 0:1, :])
        sc = jnp.where(is_ctx, mod_ref[0, 1:2, :], mod_ref[1, 1:2, :])
        hm_scr[...] = (x_ref[...] * (1.0 + sc) + sh).astype(BF16)

    o_ref[...] = jnp.dot(hm_scr[...], w_ref[...], preferred_element_type=F32)


def _inproj_call(x_all, mods, w_in, ctx_len):
    n = x_all.shape[0]
    tn = 1024
    tb = INPROJ_ROWS
    return pl.pallas_call(
        functools.partial(_inproj_kernel, ctx_len=ctx_len),
        grid=(n // tb, IN_COLS // tn),
        in_specs=[
            pl.BlockSpec((tb, D_MODEL), lambda i, j: (i, 0)),
            pl.BlockSpec((2, N_MOD, D_MODEL), lambda i, j: (0, 0, 0)),
            pl.BlockSpec((D_MODEL, tn), lambda i, j: (0, j)),
        ],
        out_specs=pl.BlockSpec((tb, tn), lambda i, j: (i, j)),
        out_shape=jax.ShapeDtypeStruct((n, IN_COLS), F32),
        scratch_shapes=[pltpu.VMEM((tb, D_MODEL), BF16)],
        compiler_params=_cparams(("arbitrary", "arbitrary")),
        name="inproj",
    )(x_all, mods, w_in)


def _prep_kernel(p_ref, cos_ref, sa_ref, sb_ref, qg_ref, kg_ref,
                 qa_o, ka_o, va_o, qc_o, kc_o, vc_o):
    x = p_ref[...]
    cos = cos_ref[...]
    sa = sa_ref[...]
    sb = sb_ref[...]

    def tile(t, width):
        reps = width // 128
        return t if reps == 1 else jnp.concatenate([t] * reps, axis=1)

    def rope(y):
        w = y.shape[1]
        return (y * tile(cos, w)
                + pltpu.roll(y, w - 16, 1) * tile(sa, w)
                + pltpu.roll(y, 16, 1) * tile(sb, w))

    def inv_rms(xh):
        return lax.rsqrt(jnp.mean(xh * xh, axis=-1, keepdims=True) + RMS_EPS)

    qa = x[:, 0:512]
    ka = x[:, 512:640]
    va = x[:, 640:768]
    qc = x[:, 768:1280]
    kc = x[:, 1280:1408]
    vc = x[:, 1408:1536]

    qa_r = rope(qa * tile(qg_ref[...], 512))
    ka_r = rope(ka * kg_ref[...])
    qc_r = rope(qc)
    kc_r = rope(kc)

    for h in range(N_HEADS):
        sl = slice(h * HEAD_DIM, (h + 1) * HEAD_DIM)
        qa_o[h, :, 0:HEAD_DIM] = (qa_r[:, sl] * (inv_rms(qa[:, sl]) * ATTN_SCALE)).astype(BF16)
        qa_o[h, :, HEAD_DIM:2 * HEAD_DIM] = jnp.zeros((x.shape[0], HEAD_DIM), BF16)
        qc_o[h] = (qc_r[:, sl] * ATTN_SCALE).astype(BF16)
    lane = lax.broadcasted_iota(jnp.int32, (x.shape[0], HEAD_DIM), 1)
    ones_col = jnp.where(lane == 0, 1.0, 0.0).astype(BF16)
    for h in range(N_KV):
        sl = slice(h * HEAD_DIM, (h + 1) * HEAD_DIM)
        ka_o[h, :, 0:HEAD_DIM] = (ka_r[:, sl] * inv_rms(ka[:, sl])).astype(BF16)
        ka_o[h, :, HEAD_DIM:2 * HEAD_DIM] = ones_col
        va_o[h, :, 0:HEAD_DIM] = va[:, sl].astype(BF16)
        va_o[h, :, HEAD_DIM:2 * HEAD_DIM] = ones_col
        kc_o[h] = kc_r[:, sl].astype(BF16)
        vc_o[h, :, 0:HEAD_DIM] = vc[:, sl].astype(BF16)
        vc_o[h, :, HEAD_DIM:2 * HEAD_DIM] = ones_col


def _prep_call(proj, cos_t, sa_t, sb_t, qg, kg):
    n = proj.shape[0]
    tb = ROW_BLOCK
    tab = pl.BlockSpec((tb, 128), lambda i: (i, 0))
    gain = pl.BlockSpec((1, 128), lambda i: (0, 0))

    def hm(nh, width=HEAD_DIM):
        return pl.BlockSpec((nh, tb, width), lambda i: (0, i, 0))

    def hshape(nh, width=HEAD_DIM):
        return jax.ShapeDtypeStruct((nh, n, width), BF16)

    return pl.pallas_call(
        _prep_kernel,
        grid=(n // tb,),
        in_specs=[pl.BlockSpec((tb, 1536), lambda i: (i, 0)), tab, tab, tab, gain, gain],
        out_specs=[hm(N_HEADS, 2 * HEAD_DIM), hm(N_KV, 2 * HEAD_DIM), hm(N_KV, 2 * HEAD_DIM),
                   hm(N_HEADS), hm(N_KV), hm(N_KV, 2 * HEAD_DIM)],
        out_shape=[hshape(N_HEADS, 2 * HEAD_DIM), hshape(N_KV, 2 * HEAD_DIM), hshape(N_KV, 2 * HEAD_DIM),
                   hshape(N_HEADS), hshape(N_KV), hshape(N_KV, 2 * HEAD_DIM)],
        compiler_params=_cparams(("arbitrary",)),
        name="prep",
    )(proj, cos_t, sa_t, sb_t, qg, kg)


def _store_heads(o_ref, o, tq):
    for g in range(GROUP):
        o_ref[:, g * HEAD_DIM:(g + 1) * HEAD_DIM] = o[g * tq:(g + 1) * tq, :].astype(BF16)


def _attn_a_kernel(q_ref, k_ref, v_ref, o_ref, s_scr, mx_scr, m_scr, acc_scr, kmax_scr, *,
                   ctx_len, ctx_blocks, lat_chunks, unroll):
    i = pl.program_id(1)
    tq = q_ref.shape[1]
    rows = GROUP * tq
    key_chunk = s_scr.shape[2]
    q0 = q_ref[...].reshape(rows, 2 * HEAD_DIM)

    @pl.when(i == 0)
    def _():
        kk = k_ref[0, :, 0:HEAD_DIM].astype(F32)
        k_norm2 = jnp.max(jnp.sum(kk * kk, axis=-1, keepdims=True), axis=0, keepdims=True)
        kmax_scr[...] = jnp.broadcast_to(jnp.sqrt(k_norm2), kmax_scr.shape)

    def chunk_off(c):
        return pl.multiple_of(ctx_len + c * key_chunk, 128)

    def run(q, online):
        def scores(slot, off, width):
            s = _nt_dot(q, k_ref[0, pl.ds(off, width), :])
            s_scr[slot, :, 0:width] = s
            if online:
                mx_scr[slot] = jnp.broadcast_to(jnp.max(s, axis=-1, keepdims=True), (rows, 128))

        def update(slot, off, width):
            s = s_scr[slot, :, 0:width]
            v = v_ref[0, pl.ds(off, width), :]
            if online:
                m_old = m_scr[...]
                m_new = jnp.maximum(m_old, mx_scr[slot])
                p = jnp.exp(s - jnp.concatenate([m_new] * (width // 128), axis=1))
                acc_scr[...] = jnp.exp(m_old - m_new) * acc_scr[...] + jnp.dot(
                    p.astype(BF16), v, preferred_element_type=F32)
                m_scr[...] = m_new
            else:
                acc_scr[...] += jnp.dot(jnp.exp(s).astype(BF16), v, preferred_element_type=F32)

        if online:
            m_scr[...] = jnp.full(m_scr.shape, -jnp.inf, F32)
        acc_scr[...] = jnp.zeros(acc_scr.shape, F32)
        scores(0, 0, ctx_len)
        update(0, 0, ctx_len)

        @pl.when(i >= ctx_blocks)
        def _():
            scores(0, chunk_off(0), key_chunk)

            def body(it, carry):
                for u in range(0, unroll, 2):
                    c = unroll * it + u
                    scores(1, chunk_off(c + 1), key_chunk)
                    update(0, chunk_off(c), key_chunk)
                    scores(0, chunk_off(jnp.minimum(c + 2, lat_chunks - 1)), key_chunk)
                    update(1, chunk_off(c + 1), key_chunk)
                return carry

            lax.fori_loop(0, lat_chunks // unroll, body, 0)

        acc = acc_scr[...]
        _store_heads(o_ref, acc[:, 0:HEAD_DIM] / acc[:, HEAD_DIM:HEAD_DIM + 1], tq)

    qf = q0.astype(F32)
    q_norm = jnp.sqrt(jnp.sum(qf * qf, axis=-1, keepdims=True))
    bound = (q_norm * kmax_scr[0:1, 0:1] * BOUND_SLACK).astype(BF16).astype(F32)
    shift_lane = lax.broadcasted_iota(jnp.int32, qf.shape, 1) == HEAD_DIM
    run(jnp.where(shift_lane, -bound, qf).astype(BF16), online=False)

    @pl.when(jnp.logical_not(jnp.min(acc_scr[:, HEAD_DIM:HEAD_DIM + 1]) > SAFE_ROW_SUM))
    def _():
        run(q0, online=True)


def _attn_a_call(q, k, v, ctx_len):
    n = q.shape[1]
    tq = ROW_BLOCK
    s_len = n - ctx_len
    key_chunk = KEY_CHUNK if s_len % (2 * KEY_CHUNK) == 0 else KEY_CHUNK // 2
    lat_chunks = s_len // key_chunk
    assert s_len % key_chunk == 0 and lat_chunks % 2 == 0
    unroll = max(u for u in (8, 4, 2) if lat_chunks % u == 0)
    kern = functools.partial(_attn_a_kernel, ctx_len=ctx_len, ctx_blocks=ctx_len // tq, lat_chunks=lat_chunks,
                             unroll=unroll)
    return pl.pallas_call(
        kern,
        grid=(N_KV, n // tq),
        in_specs=[
            pl.BlockSpec((GROUP, tq, 2 * HEAD_DIM), lambda j, i: (j, i, 0)),
            pl.BlockSpec((1, n, 2 * HEAD_DIM), lambda j, i: (j, 0, 0)),
            pl.BlockSpec((1, n, 2 * HEAD_DIM), lambda j, i: (j, 0, 0)),
        ],
        out_specs=pl.BlockSpec((tq, GROUP * HEAD_DIM), lambda j, i: (i, j)),
        out_shape=jax.ShapeDtypeStruct((n, N_HEADS * HEAD_DIM), BF16),
        scratch_shapes=[
            pltpu.VMEM((2, GROUP * tq, key_chunk), F32),
            pltpu.VMEM((2, GROUP * tq, 128), F32),
            pltpu.VMEM((GROUP * tq, 128), F32),
            pltpu.VMEM((GROUP * tq, 2 * HEAD_DIM), F32),
            pltpu.VMEM((8, 128), F32),
        ],
        compiler_params=_cparams(("arbitrary", "arbitrary")),
        name="attn_global",
    )(q, k, v)


def _attn_c_kernel(sink_ref, q_ref, k_ref, v_ref, o_ref, *, ctx_len, ctx_blocks, s_len):
    j = pl.program_id(0)
    i = pl.program_id(1)
    tq = q_ref.shape[1]
    rows = GROUP * tq
    n_loc = tq + 2 * WINDOW
    q = q_ref[...].reshape(rows, HEAD_DIM)

    p0 = i * tq - ctx_len
    ws = pl.multiple_of(jnp.clip(p0 - WINDOW, 0, s_len - n_loc), 128)
    k_loc = k_ref[0, pl.ds(ctx_len + ws, n_loc), :]
    v_loc = v_ref[0, pl.ds(ctx_len + ws, n_loc), :]
    k_ctx = k_ref[0, 0:ctx_len, :]
    v_ctx = v_ref[0, 0:ctx_len, :]

    dist = (lax.broadcasted_iota(jnp.int32, (tq, n_loc), 0) - lax.broadcasted_iota(jnp.int32, (tq, n_loc), 1)
            + (p0 - ws))
    valid = (jnp.abs(dist) <= WINDOW) & (i >= ctx_blocks)
    s_loc = jnp.where(valid[None], _nt_dot(q, k_loc).reshape(GROUP, tq, n_loc), NEG_INF).reshape(rows, n_loc)
    s_ctx = _nt_dot(q, k_ctx)

    grp = lax.broadcasted_iota(jnp.int32, (rows, 1), 0) // tq
    sink = jnp.full((rows, 1), sink_ref[j * GROUP + GROUP - 1], F32)
    for g in range(GROUP - 1):
        sink = jnp.where(grp == g, sink_ref[j * GROUP + g], sink)

    m = jnp.maximum(jnp.maximum(jnp.max(s_loc, axis=-1, keepdims=True),
                                jnp.max(s_ctx, axis=-1, keepdims=True)), sink)
    acc = (jnp.dot(jnp.exp(s_loc - m).astype(BF16), v_loc, preferred_element_type=F32)
           + jnp.dot(jnp.exp(s_ctx - m).astype(BF16), v_ctx, preferred_element_type=F32))
    o = acc[:, 0:HEAD_DIM] / (acc[:, HEAD_DIM:HEAD_DIM + 1] + jnp.exp(sink - m))
    _store_heads(o_ref, o, tq)


def _attn_c_call(sink, q, k, v, ctx_len):
    n = q.shape[1]
    tq = ROW_BLOCK
    kern = functools.partial(_attn_c_kernel, ctx_len=ctx_len, ctx_blocks=ctx_len // tq, s_len=n - ctx_len)
    return pl.pallas_call(
        kern,
        grid=(N_KV, n // tq),
        in_specs=[
            pl.BlockSpec(memory_space=pltpu.SMEM),
            pl.BlockSpec((GROUP, tq, HEAD_DIM), lambda j, i: (j, i, 0)),
            pl.BlockSpec((1, n, HEAD_DIM), lambda j, i: (j, 0, 0)),
            pl.BlockSpec((1, n, 2 * HEAD_DIM), lambda j, i: (j, 0, 0)),
        ],
        out_specs=pl.BlockSpec((tq, GROUP * HEAD_DIM), lambda j, i: (i, j)),
        out_shape=jax.ShapeDtypeStruct((n, N_HEADS * HEAD_DIM), BF16),
        compiler_params=_cparams(("arbitrary", "arbitrary")),
        name="attn_window",
    )(sink, q, k, v)


def _merge_kernel(x_ref, mod_ref, ya_ref, yc_ref, xb_ref, gb_ref, gc_ref,
                  xbp_ref, gcp_ref, xbn_ref, gcn_ref, gl_ref, cw_ref, wb_ref, wo_ref,
                  lg_ref, lb_ref, wpq_ref, x1_ref, hq_ref, qp_ref, *, n_blocks, ctx_blocks):
    i = pl.program_id(0)
    tb = x_ref.shape[0]

    z = gc_ref[...] * xb_ref[...]
    z_before = gcp_ref[7:8, :] * xbp_ref[7:8, :]
    z_after = gcn_ref[0:1, :] * xbn_ref[0:1, :]
    seq_start = (i == 0) | (i == ctx_blocks)
    seq_end = (i == ctx_blocks - 1) | (i == n_blocks - 1)
    z_before = jnp.where(seq_start, 0.0, z_before)
    z_after = jnp.where(seq_end, 0.0, z_after)
    rid = lax.broadcasted_iota(jnp.int32, z.shape, 0)
    z_prev = jnp.where(rid == 0, z_before, pltpu.roll(z, 1, 0))
    z_next = jnp.where(rid == tb - 1, z_after, pltpu.roll(z, tb - 1, 0))
    cw = cw_ref[...]
    yb = gb_ref[...] * (cw[0:1, :] * z_prev + cw[1:2, :] * z + cw[2:3, :] * z_next)

    pa = jnp.dot(ya_ref[...], wb_ref[0], preferred_element_type=F32)
    pb = jnp.dot(yb.astype(BF16), wb_ref[1], preferred_element_type=F32)
    pc = jnp.dot(yc_ref[...], wb_ref[2], preferred_element_type=F32)
    gates = jax.nn.sigmoid(gl_ref[...])
    m = (gates[:, 0:D_MODEL] * pa + gates[:, D_MODEL:2 * D_MODEL] * pb
         + gates[:, 2 * D_MODEL:3 * D_MODEL] * pc)
    y = jnp.dot(m.astype(BF16), wo_ref[...], preferred_element_type=F32)

    g1 = mod_ref[0, 2:3, :]
    sh2 = mod_ref[0, 3:4, :]
    sc2 = mod_ref[0, 4:5, :]
    x1 = _layer_norm(DEEPNORM_ALPHA * x_ref[...] + g1 * y, lg_ref[...], lb_ref[...])
    x1_ref[...] = x1
    hq = (x1 * (1.0 + sc2) + sh2).astype(BF16)
    hq_ref[...] = hq
    qp_ref[...] = jnp.dot(hq, wpq_ref[...], preferred_element_type=F32).astype(BF16)


def _merge_call(x_all, mods, ya, yc, proj, conv_w, w_branch, w_out, ln_g, ln_b, w_pq, ctx_blocks):
    n = x_all.shape[0]
    tb = ROW_BLOCK
    nb = n // tb
    halo = tb // 8
    last8 = n // 8 - 1
    kern = functools.partial(_merge_kernel, n_blocks=nb, ctx_blocks=ctx_blocks)
    qcols = PEER_HEADS * PEER_QDIM

    def full(shape):
        return pl.BlockSpec(shape, lambda i: (0,) * len(shape))

    def cols512(c):
        return pl.BlockSpec((tb, B_WIDTH), lambda i: (i, c))

    def before(c):
        return pl.BlockSpec((8, B_WIDTH), lambda i: (jnp.maximum(i * halo - 1, 0), c))

    def after(c):
        return pl.BlockSpec((8, B_WIDTH), lambda i: (jnp.minimum((i + 1) * halo, last8), c))

    return pl.pallas_call(
        kern,
        grid=(nb,),
        in_specs=[
            pl.BlockSpec((tb, D_MODEL), lambda i: (i, 0)),
            pl.BlockSpec((1, N_MOD, D_MODEL), lambda i: (jnp.where(i < ctx_blocks, 0, 1), 0, 0)),
            pl.BlockSpec((tb, N_HEADS * HEAD_DIM), lambda i: (i, 0)),
            pl.BlockSpec((tb, N_HEADS * HEAD_DIM), lambda i: (i, 0)),
            cols512(3), cols512(4), cols512(5),
            before(3), before(5), after(3), after(5),
            pl.BlockSpec((tb, 3 * D_MODEL), lambda i: (i, 1)),
            full((CONV_W, B_WIDTH)),
            full((3, B_WIDTH, D_MODEL)),
            full((D_MODEL, D_MODEL)),
            full((1, D_MODEL)), full((1, D_MODEL)),
            full((D_MODEL, qcols)),
        ],
        out_specs=[
            pl.BlockSpec((tb, D_MODEL), lambda i: (i, 0)),
            pl.BlockSpec((tb, D_MODEL), lambda i: (i, 0)),
            pl.BlockSpec((tb, qcols), lambda i: (i, 0)),
        ],
        out_shape=[
            jax.ShapeDtypeStruct((n, D_MODEL), F32),
            jax.ShapeDtypeStruct((n, D_MODEL), BF16),
            jax.ShapeDtypeStruct((n, qcols), BF16),
        ],
        compiler_params=_cparams(("arbitrary",)),
        name="merge",
    )(x_all, mods, ya, yc, proj, proj, proj, proj, proj, proj, proj, proj,
      conv_w, w_branch, w_out, ln_g, ln_b, w_pq)


def _extract_rounds(chains, rounds, on_round, tie_break):
    rows = chains[0].shape[0]
    ridx = lax.broadcasted_iota(jnp.int32, chains[0].shape, 0)
    chains = list(chains)
    ranks = [jnp.full(c.shape, rounds, jnp.int32) for c in chains]
    for r in range(rounds):
        for i, s in enumerate(chains):
            m = jnp.max(s, axis=0, keepdims=True)
            if tie_break:
                first = jnp.min(jnp.where(s == m, ridx, rows), axis=0, keepdims=True)
                hit = ridx == first
            else:
                hit = s == m
            ranks[i] = jnp.where(hit, r, ranks[i])
            chains[i] = jnp.where(hit, -jnp.inf, s)
            on_round(i, r, m)
    return ranks


def _extract_topk(chains, rounds, on_round, rank_ref):
    ranks = _extract_rounds(chains, rounds, on_round, tie_break=False)
    most = jnp.float32(0.0)
    for i, rank in enumerate(ranks):
        rank_ref[i] = rank
        most = jnp.maximum(most, jnp.max(jnp.sum((rank < rounds).astype(F32), axis=0, keepdims=True)))

    @pl.when(most > rounds)
    def _():
        for i, rank in enumerate(_extract_rounds(chains, rounds, on_round, tie_break=True)):
            rank_ref[i] = rank


def _topk_kernel(qp_ref, sk_ref, r2_o, w2_o, k1_o, w1_o, tv_scr, cand_scr, kj_scr, rank_scr, crank_scr):
    tt = qp_ref.shape[0]
    half = PEER_QDIM // 2
    q = qp_ref[...]
    heads = range(TOPK_HEADS)

    scores = [_nt_dot(sk_ref[h, p], q[:, (2 * h + p) * half:(2 * h + p + 1) * half])
              for h in heads for p in range(2)]

    def keep(i, r, m):
        tv_scr[i, r:r + 1, :] = m

    _extract_topk(scores, PEER_TOPK, keep, rank_scr)

    for h in heads:
        tv1 = tv_scr[2 * h]
        tv2 = tv_scr[2 * h + 1]
        off = 0
        for j, cnt in enumerate(CAND_COUNTS):
            cand_scr[h, off:off + cnt, :] = tv1[j:j + 1, :] + tv2[0:cnt, :]
            off += cnt
        cand_scr[h, N_CAND:CAND_ROWS, :] = jnp.full((CAND_ROWS - N_CAND, tt), -jnp.inf, F32)
    cands = [cand_scr[h] for h in heads]
    _extract_topk(cands, PEER_TOPK, lambda i, r, m: None, crank_scr)

    def twice_bf16(x):
        hi = lax.bitcast_convert_type(x.astype(BF16).astype(F32), jnp.uint32)
        return lax.bitcast_convert_type(hi | (hi >> 16), F32)

    for h in heads:
        cand = cands[h]
        chosen = crank_scr[h] < PEER_TOPK
        cmax = cand[0:1, :]
        z = jnp.sum(jnp.where(chosen, jnp.exp(cand - cmax), 0.0), axis=0, keepdims=True)
        cnt_f = chosen.astype(F32)
        off = 0
        for j, cnt in enumerate(CAND_COUNTS):
            kj_scr[h, j:j + 1, :] = jnp.sum(cnt_f[off:off + cnt, :], axis=0, keepdims=True)
            off += cnt

        rank1 = rank_scr[2 * h]
        rank2 = rank_scr[2 * h + 1]
        k1 = jnp.zeros((N_KEYS, tt), F32)
        for j in range(PEER_TOPK):
            k1 = jnp.where(rank1 == j, kj_scr[h, j:j + 1, :], k1)
        k1_o[h] = twice_bf16(k1)
        w1_o[h] = twice_bf16(
            jnp.where(rank1 < PEER_TOPK, jnp.exp(scores[2 * h] - tv_scr[2 * h, 0:1, :]), 0.0) / z)
        r2_o[h] = rank2.astype(F32).astype(BF16)
        w2_o[h] = jnp.where(rank2 < PEER_TOPK, jnp.exp(scores[2 * h + 1] - tv_scr[2 * h + 1, 0:1, :]),
                            0.0).astype(BF16)


def _topk_call(qp, sub_keys):
    n = qp.shape[0]
    tt = ROW_BLOCK
    hp = TOPK_HEADS
    out = pl.BlockSpec((hp, N_KEYS, tt), lambda t, h: (h, 0, t))
    shape = jax.ShapeDtypeStruct((PEER_HEADS, N_KEYS, n), F32)
    shape_bf16 = jax.ShapeDtypeStruct((PEER_HEADS, N_KEYS, n), BF16)
    return pl.pallas_call(
        _topk_kernel,
        grid=(n // tt, PEER_HEADS // hp),
        in_specs=[
            pl.BlockSpec((tt, hp * PEER_QDIM), lambda t, h: (t, h)),
            pl.BlockSpec((hp, 2, N_KEYS, PEER_QDIM // 2), lambda t, h: (h, 0, 0, 0)),
        ],
        out_specs=[out, out, out, out],
        out_shape=[shape_bf16, shape_bf16, shape, shape],
        scratch_shapes=[
            pltpu.VMEM((2 * hp, PEER_TOPK, tt), F32),
            pltpu.VMEM((hp, CAND_ROWS, tt), F32),
            pltpu.VMEM((hp, PEER_TOPK, tt), F32),
            pltpu.VMEM((2 * hp, N_KEYS, tt), jnp.int32),
            pltpu.VMEM((hp, CAND_ROWS, tt), jnp.int32),
        ],
        compiler_params=_cparams(("arbitrary", "arbitrary")),
        name="peer_topk",
    )(qp, sub_keys)


_ERF_A = (-2.72614225801306e-10, 2.77068142495902e-08, -2.10102402082508e-06, -5.69250639462346e-05,
          -7.34990630326855e-04, -2.95459980854025e-03, -1.60960333262415e-02)
_ERF_B = (-1.45660718464996e-05, -2.13374055278905e-04, -1.68282697438203e-03, -7.37332916720468e-03,
          -1.42647390514189e-02)


def _gelu(x):
    dt = x.dtype
    lim = 4.0 * math.sqrt(2.0)
    xc = jnp.clip(x, -lim, lim)
    t = xc * xc
    deg_a = len(_ERF_A) - 1
    deg_b = len(_ERF_B) - 1
    a2 = [c * (0.5 / math.sqrt(2.0)) / 2.0 ** (deg_a - i) for i, c in enumerate(_ERF_A)]
    b2 = [c / 2.0 ** (deg_b - i) for i, c in enumerate(_ERF_B)]
    a2 = [c / b2[0] for c in a2]
    b2 = [c / b2[0] for c in b2]
    a = jnp.asarray(a2[0], dt)
    for c in a2[1:]:
        a = a * t + jnp.asarray(c, dt)
    b = t + jnp.asarray(b2[1], dt)
    for c in b2[2:]:
        b = b * t + jnp.asarray(c, dt)
    return x * (0.5 + xc * (a / b))


def _dense_kernel(hq_ref, u_ref, vt_ref, r2_ref, w2_ref, k1_ref, w1_ref, o_ref, h_scr, a_scr, *, n_eb):
    s = pl.program_id(0)
    eb = u_ref.shape[0]
    tt = hq_ref.shape[0]
    n_i1 = eb // N_KEYS

    @pl.when(s == 0)
    def _():
        h_scr[...] = jnp.zeros(h_scr.shape, BF16)
        a_scr[...] = jnp.zeros(a_scr.shape, BF16)

    @pl.when(jnp.maximum(s - 2, 0) % n_eb == 0)
    def _():
        o_ref[...] = jnp.zeros(o_ref.shape, F32)

    def rows16(ref, h, ii, cs):
        row = ref[h, ii:ii + 1, cs]
        return pltpu.bitcast(jnp.tile(row, (8, 1)), BF16)

    for c in range(tt // PEER_LANES):
        cs = slice(c * PEER_LANES, (c + 1) * PEER_LANES)

        o_ref[:, cs] += jnp.dot(vt_ref[...], a_scr[:, cs], preferred_element_type=F32)

        for ii in range(n_i1):
            rows = slice(ii * N_KEYS, (ii + 1) * N_KEYS)
            g = None
            for h in range(PEER_HEADS):
                k1b = jnp.concatenate([rows16(k1_ref, h, ii, cs)] * (N_KEYS // 16), axis=0)
                w1b = jnp.concatenate([rows16(w1_ref, h, ii, cs)] * (N_KEYS // 16), axis=0)
                keep = r2_ref[h, :, cs] < k1b
                gate = jnp.where(keep, w2_ref[h, :, cs] * w1b, 0.0)
                g = gate if g is None else g + gate
            a_scr[rows, cs] = _gelu(h_scr[rows, cs]) * g

        h_scr[:, cs] = _nt_dot(u_ref[...], hq_ref[cs, :]).astype(BF16)


def _dense_call(hq, u_tab, vt_tab, layer, r2, w2, k1, w1):
    n = hq.shape[0]
    n_exp = u_tab.shape[1]
    tt = PEER_TOKENS
    eb = PEER_EXPERTS
    n_eb = n_exp // eb
    steps = (n // tt) * n_eb

    def blk(s, lag):
        return jnp.clip(s - lag, 0, steps - 1)

    tab = pl.BlockSpec((PEER_HEADS, N_KEYS, tt), lambda s: (0, 0, blk(s, 1) // n_eb))
    row = pl.BlockSpec((PEER_HEADS, eb // N_KEYS, tt), lambda s: (0, blk(s, 1) % n_eb, blk(s, 1) // n_eb))
    return pl.pallas_call(
        functools.partial(_dense_kernel, n_eb=n_eb),
        grid=(steps + 2,),
        in_specs=[
            pl.BlockSpec((tt, D_MODEL), lambda s: (blk(s, 0) // n_eb, 0)),
            pl.BlockSpec((None, eb, D_MODEL), lambda s: (layer, blk(s, 0) % n_eb, 0)),
            pl.BlockSpec((None, D_MODEL, eb), lambda s: (layer, 0, blk(s, 2) % n_eb)),
            tab, tab, row, row,
        ],
        out_specs=pl.BlockSpec((D_MODEL, tt), lambda s: (0, blk(s, 2) // n_eb)),
        out_shape=jax.ShapeDtypeStruct((D_MODEL, n), F32),
        scratch_shapes=[pltpu.VMEM((eb, tt), BF16), pltpu.VMEM((eb, tt), BF16)],
        compiler_params=_cparams(("arbitrary",)),
        name="peer_dense",
    )(hq, u_tab, vt_tab, r2, w2, k1, w1)


def _ln2_kernel(x1_ref, ft_ref, mod_ref, lg_ref, lb_ref, o_ref):
    g2 = mod_ref[0, 5:6, :]
    f = ft_ref[...].T
    o_ref[...] = _layer_norm(DEEPNORM_ALPHA * x1_ref[...] + g2 * f, lg_ref[...], lb_ref[...])


def _ln2_call(x1, ft, mods, ln_g, ln_b, ctx_blocks, latent_only):
    n = x1.shape[0]
    tb = ROW_BLOCK
    skip = ctx_blocks if latent_only else 0
    return pl.pallas_call(
        _ln2_kernel,
        grid=(n // tb,),
        in_specs=[
            pl.BlockSpec((tb, D_MODEL), lambda i: (i, 0)),
            pl.BlockSpec((D_MODEL, tb), lambda i: (0, i)),
            pl.BlockSpec((1, N_MOD, D_MODEL), lambda i: (jnp.where(i < ctx_blocks, 0, 1), 0, 0)),
            pl.BlockSpec((1, D_MODEL), lambda i: (0, 0)),
            pl.BlockSpec((1, D_MODEL), lambda i: (0, 0)),
        ],
        out_specs=pl.BlockSpec((tb, D_MODEL), lambda i: (jnp.maximum(i - skip, 0), 0)),
        out_shape=jax.ShapeDtypeStruct((n - skip * tb, D_MODEL), F32),
        compiler_params=_cparams(("arbitrary",)),
        name="ln2",
    )(x1, ft, mods, ln_g, ln_b)


def _rope_tables(ctx_len, s_len):
    t = np.arange(s_len)
    pos = np.stack([t // GRID_W, t % GRID_W], axis=-1).astype(np.float64)
    quarter = HEAD_DIM // 4
    inv_freq = ROPE_THETA ** (-np.arange(quarter, dtype=np.float64) / quarter)
    ang = pos[:, :, None] * inv_freq
    cos = np.repeat(np.cos(ang), 2, axis=1).reshape(s_len, HEAD_DIM)
    sin = np.repeat(np.sin(ang), 2, axis=1).reshape(s_len, HEAD_DIM)
    first_half = (np.arange(HEAD_DIM) % 32) < 16
    sa = np.where(first_half, -sin, 0.0)
    sb = np.where(first_half, 0.0, sin)

    def full(tab, fill):
        tab = np.concatenate([np.full((ctx_len, HEAD_DIM), fill), tab], axis=0)
        return jnp.asarray(np.tile(tab, (1, 2)), F32)

    return full(cos, 1.0), full(sa, 0.0), full(sb, 0.0)


def kernel(x, c, ctx, c_ctx, w_mod, b_mod, w_in, q_norm, k_norm, conv_w, sink, w_branch, w_out,
           ln1_g, ln1_b, w_pq, sub_keys, u_tab, v_tab, ln2_g, ln2_b):
    bsz, s_len, d = x.shape
    ctx_len = ctx.shape[1]
    depth = w_mod.shape[0]
    assert bsz == 1 and d == D_MODEL and depth == DEPTH
    assert ctx_len % ROW_BLOCK == 0
    assert (ctx_len + s_len) % PEER_TOKENS == 0 and (ctx_len + s_len) % INPROJ_ROWS == 0
    ctx_blocks = ctx_len // ROW_BLOCK

    x_all = jnp.concatenate([ctx[0], x[0]], axis=0)
    cos_t, sa_t, sb_t = _rope_tables(ctx_len, s_len)

    cc = jnp.zeros((8, D_MODEL), F32).at[0].set(c_ctx).at[1].set(c[0])
    mod_all = _mod_call(cc, w_mod.astype(BF16), b_mod[:, None, :])
    u_all = u_tab.astype(BF16)
    vt_all = jnp.swapaxes(v_tab, 1, 2).astype(BF16)

    for layer in range(depth):
        mods = mod_all[layer, 0:2].reshape(2, N_MOD, D_MODEL)
        proj = _inproj_call(x_all, mods, w_in[layer].astype(BF16), ctx_len)
        qg = jnp.tile(q_norm[layer], 2)[None, :]
        kg = jnp.tile(k_norm[layer], 2)[None, :]
        qa, ka, va, qc, kc, vc = _prep_call(proj, cos_t, sa_t, sb_t, qg, kg)
        ya = _attn_a_call(qa, ka, va, ctx_len)
        yc = _attn_c_call(sink[layer], qc, kc, vc, ctx_len)
        x1, hq, qp = _merge_call(x_all, mods, ya, yc, proj, conv_w[layer],
                                 w_branch[layer].astype(BF16), w_out[layer].astype(BF16),
                                 ln1_g[layer][None, :], ln1_b[layer][None, :],
                                 w_pq[layer].astype(BF16), ctx_blocks)
        r2, w2, k1, w1 = _topk_call(qp, sub_keys[layer].astype(BF16))
        ft = _dense_call(hq, u_all, vt_all, layer, r2, w2, k1, w1)
        x_all = _ln2_call(x1, ft, mods, ln2_g[layer][None, :], ln2_b[layer][None, :], ctx_blocks,
                          latent_only=(layer == depth - 1))

    return x_all[None]
```

```python
import functools
import math

import jax
import jax.numpy as jnp
import numpy as np
from jax import lax
from jax.experimental import pallas as pl
from jax.experimental.pallas import tpu as pltpu

F32 = jnp.float32
BF16 = jnp.bfloat16

D_MODEL = 1024
HEAD_DIM = 64
N_HEADS = 8
N_KV = 2
GROUP = N_HEADS // N_KV
GRID_W = 64
WINDOW = 128
CONV_W = 3
B_WIDTH = 512
N_MOD = 6
ROPE_THETA = 10000.0
PEER_HEADS = 8
N_KEYS = 128
PEER_TOPK = 16
PEER_QDIM = 256
DEPTH = 2
IN_COLS = 6144
DEEPNORM_ALPHA = (2 * DEPTH) ** 0.25
LN_EPS = 1e-6
RMS_EPS = 1e-6
NEG_INF = -1e30
ATTN_SCALE = HEAD_DIM ** -0.5

ROW_BLOCK = 256
INPROJ_ROWS = 1280
KEY_CHUNK = 1280
BOUND_SLACK = 1.0 + 2.0 ** -6
SAFE_ROW_SUM = 1e-30
PEER_TOKENS = 1280
PEER_LANES = 256
TOPK_HEADS = 8
PEER_EXPERTS = 1024
VMEM_LIMIT = 56 * 1024 * 1024

CAND_COUNTS = tuple(PEER_TOPK // (j + 1) for j in range(PEER_TOPK))
N_CAND = sum(CAND_COUNTS)
CAND_ROWS = 56


def _cparams(sem):
    return pltpu.CompilerParams(dimension_semantics=sem, vmem_limit_bytes=VMEM_LIMIT)


def _nt_dot(a, b):
    return lax.dot_general(a, b, (((1,), (1,)), ((), ())), preferred_element_type=F32)


def _layer_norm(r, g, b):
    mu = jnp.mean(r, axis=-1, keepdims=True)
    rc = r - mu
    var = jnp.mean(rc * rc, axis=-1, keepdims=True)
    return rc * lax.rsqrt(var + LN_EPS) * g + b


def _mod_kernel(c_ref, w_ref, b_ref, o_ref):
    c = c_ref[...]
    h = (c * jax.nn.sigmoid(c)).astype(BF16)
    o_ref[0] = jnp.dot(h, w_ref[0], preferred_element_type=F32) + b_ref[0]


def _mod_call(cc, w_mod, b_mod):
    depth = w_mod.shape[0]
    cols = w_mod.shape[2]
    tn = 1024
    return pl.pallas_call(
        _mod_kernel,
        grid=(depth, cols // tn),
        in_specs=[
            pl.BlockSpec((8, D_MODEL), lambda l, j: (0, 0)),
            pl.BlockSpec((1, D_MODEL, tn), lambda l, j: (l, 0, j)),
            pl.BlockSpec((1, 1, tn), lambda l, j: (l, 0, j)),
        ],
        out_specs=pl.BlockSpec((1, 8, tn), lambda l, j: (l, 0, j)),
        out_shape=jax.ShapeDtypeStruct((depth, 8, cols), F32),
        compiler_params=_cparams(("arbitrary", "arbitrary")),
        name="mod",
    )(cc, w_mod, b_mod)


def _inproj_kernel(x_ref, mod_ref, w_ref, o_ref, hm_scr, *, ctx_len):
    tb = x_ref.shape[0]

    @pl.when(pl.program_id(1) == 0)
    def _():
        row = pl.program_id(0) * tb + lax.broadcasted_iota(jnp.int32, (tb, 1), 0)
        is_ctx = row < ctx_len
        sh = jnp.where(is_ctx, mod_ref[0, 0:1, :], mod_ref[1, 0:1, :])
        sc = jnp.where(is_ctx, mod_ref[0, 1:2, :], mod_ref[1, 1:2, :])
        hm_scr[...] = (x_ref[...] * (1.0 + sc) + sh).astype(BF16)

    o_ref[...] = jnp.dot(hm_scr[...], w_ref[...], preferred_element_type=F32)


def _inproj_call(x_all, mods, w_in, ctx_len):
    n = x_all.shape[0]
    tn = 1024
    tb = INPROJ_ROWS
    return pl.pallas_call(
        functools.partial(_inproj_kernel, ctx_len=ctx_len),
        grid=(n // tb, IN_COLS // tn),
        in_specs=[
            pl.BlockSpec((tb, D_MODEL), lambda i, j: (i, 0)),
            pl.BlockSpec((2, N_MOD, D_MODEL), lambda i, j: (0, 0, 0)),
            pl.BlockSpec((D_MODEL, tn), lambda i, j: (0, j)),
        ],
        out_specs=pl.BlockSpec((tb, tn), lambda i, j: (i, j)),
        out_shape=jax.ShapeDtypeStruct((n, IN_COLS), F32),
        scratch_shapes=[pltpu.VMEM((tb, D_MODEL), BF16)],
        compiler_params=_cparams(("arbitrary", "arbitrary")),
        name="inproj",
    )(x_all, mods, w_in)


def _prep_kernel(p_ref, cos_ref, sa_ref, sb_ref, qg_ref, kg_ref,
                 qa_o, ka_o, va_o, qc_o, kc_o, vc_o):
    x = p_ref[...]
    cos = cos_ref[...]
    sa = sa_ref[...]
    sb = sb_ref[...]

    def tile(t, width):
        reps = width // 128
        return t if reps == 1 else jnp.concatenate([t] * reps, axis=1)

    def rope(y):
        w = y.shape[1]
        return (y * tile(cos, w)
                + pltpu.roll(y, w - 16, 1) * tile(sa, w)
                + pltpu.roll(y, 16, 1) * tile(sb, w))

    def inv_rms(xh):
        return lax.rsqrt(jnp.mean(xh * xh, axis=-1, keepdims=True) + RMS_EPS)

    qa = x[:, 0:512]
    ka = x[:, 512:640]
    va = x[:, 640:768]
    qc = x[:, 768:1280]
    kc = x[:, 1280:1408]
    vc = x[:, 1408:1536]

    qa_r = rope(qa * tile(qg_ref[...], 512))
    ka_r = rope(ka * kg_ref[...])
    qc_r = rope(qc)
    kc_r = rope(kc)

    for h in range(N_HEADS):
        sl = slice(h * HEAD_DIM, (h + 1) * HEAD_DIM)
        qa_o[h, :, 0:HEAD_DIM] = (qa_r[:, sl] * (inv_rms(qa[:, sl]) * ATTN_SCALE)).astype(BF16)
        qa_o[h, :, HEAD_DIM:2 * HEAD_DIM] = jnp.zeros((x.shape[0], HEAD_DIM), BF16)
        qc_o[h] = (qc_r[:, sl] * ATTN_SCALE).astype(BF16)
    lane = lax.broadcasted_iota(jnp.int32, (x.shape[0], HEAD_DIM), 1)
    ones_col = jnp.where(lane == 0, 1.0, 0.0).astype(BF16)
    for h in range(N_KV):
        sl = slice(h * HEAD_DIM, (h + 1) * HEAD_DIM)
        ka_o[h, :, 0:HEAD_DIM] = (ka_r[:, sl] * inv_rms(ka[:, sl])).astype(BF16)
        ka_o[h, :, HEAD_DIM:2 * HEAD_DIM] = ones_col
        va_o[h, :, 0:HEAD_DIM] = va[:, sl].astype(BF16)
        va_o[h, :, HEAD_DIM:2 * HEAD_DIM] = ones_col
        kc_o[h] = kc_r[:, sl].astype(BF16)
        vc_o[h, :, 0:HEAD_DIM] = vc[:, sl].astype(BF16)
        vc_o[h, :, HEAD_DIM:2 * HEAD_DIM] = ones_col


def _prep_call(proj, cos_t, sa_t, sb_t, qg, kg):
    n = proj.shape[0]
    tb = ROW_BLOCK
    tab = pl.BlockSpec((tb, 128), lambda i: (i, 0))
    gain = pl.BlockSpec((1, 128), lambda i: (0, 0))

    def hm(nh, width=HEAD_DIM):
        return pl.BlockSpec((nh, tb, width), lambda i: (0, i, 0))

    def hshape(nh, width=HEAD_DIM):
        return jax.ShapeDtypeStruct((nh, n, width), BF16)

    return pl.pallas_call(
        _prep_kernel,
        grid=(n // tb,),
        in_specs=[pl.BlockSpec((tb, 1536), lambda i: (i, 0)), tab, tab, tab, gain, gain],
        out_specs=[hm(N_HEADS, 2 * HEAD_DIM), hm(N_KV, 2 * HEAD_DIM), hm(N_KV, 2 * HEAD_DIM),
                   hm(N_HEADS), hm(N_KV), hm(N_KV, 2 * HEAD_DIM)],
        out_shape=[hshape(N_HEADS, 2 * HEAD_DIM), hshape(N_KV, 2 * HEAD_DIM), hshape(N_KV, 2 * HEAD_DIM),
                   hshape(N_HEADS), hshape(N_KV), hshape(N_KV, 2 * HEAD_DIM)],
        compiler_params=_cparams(("arbitrary",)),
        name="prep",
    )(proj, cos_t, sa_t, sb_t, qg, kg)


def _store_heads(o_ref, o, tq):
    for g in range(GROUP):
        o_ref[:, g * HEAD_DIM:(g + 1) * HEAD_DIM] = o[g * tq:(g + 1) * tq, :].astype(BF16)


def _attn_a_kernel(q_ref, k_ref, v_ref, o_ref, s_scr, mx_scr, m_scr, acc_scr, kmax_scr, *,
                   ctx_len, ctx_blocks, n_chunks):
    i = pl.program_id(1)
    tq = q_ref.shape[1]
    rows = GROUP * tq
    key_chunk = s_scr.shape[2]
    q0 = q_ref[...].reshape(rows, 2 * HEAD_DIM)

    @pl.when(i == 0)
    def _():
        kk = k_ref[0, :, 0:HEAD_DIM].astype(F32)
        k_norm2 = jnp.max(jnp.sum(kk * kk, axis=-1, keepdims=True), axis=0, keepdims=True)
        kmax_scr[...] = jnp.broadcast_to(jnp.sqrt(k_norm2), kmax_scr.shape)

    def run(q, online):
        def scores(slot, off, width):
            s = _nt_dot(q, k_ref[0, pl.ds(off, width), :])
            s_scr[slot, :, 0:width] = s
            if online:
                mx_scr[slot] = jnp.broadcast_to(jnp.max(s, axis=-1, keepdims=True), (rows, 128))

        def update(slot, off, width):
            s = s_scr[slot, :, 0:width]
            v = v_ref[0, pl.ds(off, width), :]
            if online:
                m_old = m_scr[...]
                m_new = jnp.maximum(m_old, mx_scr[slot])
                p = jnp.exp(s - jnp.concatenate([m_new] * (width // 128), axis=1))
                acc_scr[...] = jnp.exp(m_old - m_new) * acc_scr[...] + jnp.dot(
                    p.astype(BF16), v, preferred_element_type=F32)
                m_scr[...] = m_new
            else:
                acc_scr[...] += jnp.dot(jnp.exp(s).astype(BF16), v, preferred_element_type=F32)

        if online:
            m_scr[...] = jnp.full(m_scr.shape, -jnp.inf, F32)
        acc_scr[...] = jnp.zeros(acc_scr.shape, F32)

        @pl.when(i < ctx_blocks)
        def _():
            scores(0, 0, ctx_len)
            update(0, 0, ctx_len)

        @pl.when(i >= ctx_blocks)
        def _():
            scores(0, 0, key_chunk)
            for c in range(n_chunks):
                if c + 1 < n_chunks:
                    scores((c + 1) % 2, (c + 1) * key_chunk, key_chunk)
                update(c % 2, c * key_chunk, key_chunk)

        acc = acc_scr[...]
        _store_heads(o_ref, acc[:, 0:HEAD_DIM] / acc[:, HEAD_DIM:HEAD_DIM + 1], tq)

    qf = q0.astype(F32)
    q_norm = jnp.sqrt(jnp.sum(qf * qf, axis=-1, keepdims=True))
    bound = (q_norm * kmax_scr[0:1, 0:1] * BOUND_SLACK).astype(BF16).astype(F32)
    shift_lane = lax.broadcasted_iota(jnp.int32, qf.shape, 1) == HEAD_DIM
    run(jnp.where(shift_lane, -bound, qf).astype(BF16), online=False)

    @pl.when(jnp.logical_not(jnp.min(acc_scr[:, HEAD_DIM:HEAD_DIM + 1]) > SAFE_ROW_SUM))
    def _():
        run(q0, online=True)


def _attn_a_call(q, k, v, ctx_len):
    n = q.shape[1]
    tq = ROW_BLOCK
    key_chunk = KEY_CHUNK
    assert n % key_chunk == 0 and ctx_len <= key_chunk
    kern = functools.partial(_attn_a_kernel, ctx_len=ctx_len, ctx_blocks=ctx_len // tq, n_chunks=n // key_chunk)
    return pl.pallas_call(
        kern,
        grid=(N_KV, n // tq),
        in_specs=[
            pl.BlockSpec((GROUP, tq, 2 * HEAD_DIM), lambda j, i: (j, i, 0)),
            pl.BlockSpec((1, n, 2 * HEAD_DIM), lambda j, i: (j, 0, 0)),
            pl.BlockSpec((1, n, 2 * HEAD_DIM), lambda j, i: (j, 0, 0)),
        ],
        out_specs=pl.BlockSpec((tq, GROUP * HEAD_DIM), lambda j, i: (i, j)),
        out_shape=jax.ShapeDtypeStruct((n, N_HEADS * HEAD_DIM), BF16),
        scratch_shapes=[
            pltpu.VMEM((2, GROUP * tq, key_chunk), F32),
            pltpu.VMEM((2, GROUP * tq, 128), F32),
            pltpu.VMEM((GROUP * tq, 128), F32),
            pltpu.VMEM((GROUP * tq, 2 * HEAD_DIM), F32),
            pltpu.VMEM((8, 128), F32),
        ],
        compiler_params=_cparams(("arbitrary", "arbitrary")),
        name="attn_global",
    )(q, k, v)


def _attn_c_kernel(sink_ref, q_ref, k_ref, v_ref, o_ref, *, ctx_len, ctx_blocks, s_len):
    j = pl.program_id(0)
    i = pl.program_id(1)
    tq = q_ref.shape[1]
    rows = GROUP * tq
    n_loc = tq + 2 * WINDOW
    q = q_ref[...].reshape(rows, HEAD_DIM)

    p0 = i * tq - ctx_len
    ws = pl.multiple_of(jnp.clip(p0 - WINDOW, 0, s_len - n_loc), 128)
    k_loc = k_ref[0, pl.ds(ctx_len + ws, n_loc), :]
    v_loc = v_ref[0, pl.ds(ctx_len + ws, n_loc), :]
    k_ctx = k_ref[0, 0:ctx_len, :]
    v_ctx = v_ref[0, 0:ctx_len, :]

    dist = (lax.broadcasted_iota(jnp.int32, (tq, n_loc), 0) - lax.broadcasted_iota(jnp.int32, (tq, n_loc), 1)
            + (p0 - ws))
    valid = (jnp.abs(dist) <= WINDOW) & (i >= ctx_blocks)
    s_loc = jnp.where(valid[None], _nt_dot(q, k_loc).reshape(GROUP, tq, n_loc), NEG_INF).reshape(rows, n_loc)
    s_ctx = _nt_dot(q, k_ctx)

    grp = lax.broadcasted_iota(jnp.int32, (rows, 1), 0) // tq
    sink = jnp.full((rows, 1), sink_ref[j * GROUP + GROUP - 1], F32)
    for g in range(GROUP - 1):
        sink = jnp.where(grp == g, sink_ref[j * GROUP + g], sink)

    m = jnp.maximum(jnp.maximum(jnp.max(s_loc, axis=-1, keepdims=True),
                                jnp.max(s_ctx, axis=-1, keepdims=True)), sink)
    acc = (jnp.dot(jnp.exp(s_loc - m).astype(BF16), v_loc, preferred_element_type=F32)
           + jnp.dot(jnp.exp(s_ctx - m).astype(BF16), v_ctx, preferred_element_type=F32))
    o = acc[:, 0:HEAD_DIM] / (acc[:, HEAD_DIM:HEAD_DIM + 1] + jnp.exp(sink - m))
    _store_heads(o_ref, o, tq)


def _attn_c_call(sink, q, k, v, ctx_len):
    n = q.shape[1]
    tq = ROW_BLOCK
    kern = functools.partial(_attn_c_kernel, ctx_len=ctx_len, ctx_blocks=ctx_len // tq, s_len=n - ctx_len)
    return pl.pallas_call(
        kern,
        grid=(N_KV, n // tq),
        in_specs=[
            pl.BlockSpec(memory_space=pltpu.SMEM),
            pl.BlockSpec((GROUP, tq, HEAD_DIM), lambda j, i: (j, i, 0)),
            pl.BlockSpec((1, n, HEAD_DIM), lambda j, i: (j, 0, 0)),
            pl.BlockSpec((1, n, 2 * HEAD_DIM), lambda j, i: (j, 0, 0)),
        ],
        out_specs=pl.BlockSpec((tq, GROUP * HEAD_DIM), lambda j, i: (i, j)),
        out_shape=jax.ShapeDtypeStruct((n, N_HEADS * HEAD_DIM), BF16),
        compiler_params=_cparams(("arbitrary", "arbitrary")),
        name="attn_window",
    )(sink, q, k, v)


def _merge_kernel(x_ref, mod_ref, ya_ref, yc_ref, xb_ref, gb_ref, gc_ref,
                  xbp_ref, gcp_ref, xbn_ref, gcn_ref, gl_ref, cw_ref, wb_ref, wo_ref,
                  lg_ref, lb_ref, wpq_ref, x1_ref, hq_ref, qp_ref, *, n_blocks, ctx_blocks):
    i = pl.program_id(0)
    tb = x_ref.shape[0]

    z = gc_ref[...] * xb_ref[...]
    z_before = gcp_ref[7:8, :] * xbp_ref[7:8, :]
    z_after = gcn_ref[0:1, :] * xbn_ref[0:1, :]
    seq_start = (i == 0) | (i == ctx_blocks)
    seq_end = (i == ctx_blocks - 1) | (i == n_blocks - 1)
    z_before = jnp.where(seq_start, 0.0, z_before)
    z_after = jnp.where(seq_end, 0.0, z_after)
    rid = lax.broadcasted_iota(jnp.int32, z.shape, 0)
    z_prev = jnp.where(rid == 0, z_before, pltpu.roll(z, 1, 0))
    z_next = jnp.where(rid == tb - 1, z_after, pltpu.roll(z, tb - 1, 0))
    cw = cw_ref[...]
    yb = gb_ref[...] * (cw[0:1, :] * z_prev + cw[1:2, :] * z + cw[2:3, :] * z_next)

    pa = jnp.dot(ya_ref[...], wb_ref[0], preferred_element_type=F32)
    pb = jnp.dot(yb.astype(BF16), wb_ref[1], preferred_element_type=F32)
    pc = jnp.dot(yc_ref[...], wb_ref[2], preferred_element_type=F32)
    gates = jax.nn.sigmoid(gl_ref[...])
    m = (gates[:, 0:D_MODEL] * pa + gates[:, D_MODEL:2 * D_MODEL] * pb
         + gates[:, 2 * D_MODEL:3 * D_MODEL] * pc)
    y = jnp.dot(m.astype(BF16), wo_ref[...], preferred_element_type=F32)

    g1 = mod_ref[0, 2:3, :]
    sh2 = mod_ref[0, 3:4, :]
    sc2 = mod_ref[0, 4:5, :]
    x1 = _layer_norm(DEEPNORM_ALPHA * x_ref[...] + g1 * y, lg_ref[...], lb_ref[...])
    x1_ref[...] = x1
    hq = (x1 * (1.0 + sc2) + sh2).astype(BF16)
    hq_ref[...] = hq
    qp_ref[...] = jnp.dot(hq, wpq_ref[...], preferred_element_type=F32).astype(BF16)


def _merge_call(x_all, mods, ya, yc, proj, conv_w, w_branch, w_out, ln_g, ln_b, w_pq, ctx_blocks):
    n = x_all.shape[0]
    tb = ROW_BLOCK
    nb = n // tb
    halo = tb // 8
    last8 = n // 8 - 1
    kern = functools.partial(_merge_kernel, n_blocks=nb, ctx_blocks=ctx_blocks)
    qcols = PEER_HEADS * PEER_QDIM

    def full(shape):
        return pl.BlockSpec(shape, lambda i: (0,) * len(shape))

    def cols512(c):
        return pl.BlockSpec((tb, B_WIDTH), lambda i: (i, c))

    def before(c):
        return pl.BlockSpec((8, B_WIDTH), lambda i: (jnp.maximum(i * halo - 1, 0), c))

    def after(c):
        return pl.BlockSpec((8, B_WIDTH), lambda i: (jnp.minimum((i + 1) * halo, last8), c))

    return pl.pallas_call(
        kern,
        grid=(nb,),
        in_specs=[
            pl.BlockSpec((tb, D_MODEL), lambda i: (i, 0)),
            pl.BlockSpec((1, N_MOD, D_MODEL), lambda i: (jnp.where(i < ctx_blocks, 0, 1), 0, 0)),
            pl.BlockSpec((tb, N_HEADS * HEAD_DIM), lambda i: (i, 0)),
            pl.BlockSpec((tb, N_HEADS * HEAD_DIM), lambda i: (i, 0)),
            cols512(3), cols512(4), cols512(5),
            before(3), before(5), after(3), after(5),
            pl.BlockSpec((tb, 3 * D_MODEL), lambda i: (i, 1)),
            full((CONV_W, B_WIDTH)),
            full((3, B_WIDTH, D_MODEL)),
            full((D_MODEL, D_MODEL)),
            full((1, D_MODEL)), full((1, D_MODEL)),
            full((D_MODEL, qcols)),
        ],
        out_specs=[
            pl.BlockSpec((tb, D_MODEL), lambda i: (i, 0)),
            pl.BlockSpec((tb, D_MODEL), lambda i: (i, 0)),
            pl.BlockSpec((tb, qcols), lambda i: (i, 0)),
        ],
        out_shape=[
            jax.ShapeDtypeStruct((n, D_MODEL), F32),
            jax.ShapeDtypeStruct((n, D_MODEL), BF16),
            jax.ShapeDtypeStruct((n, qcols), BF16),
        ],
        compiler_params=_cparams(("arbitrary",)),
        name="merge",
    )(x_all, mods, ya, yc, proj, proj, proj, proj, proj, proj, proj, proj,
      conv_w, w_branch, w_out, ln_g, ln_b, w_pq)


def _extract_rounds(chains, rounds, on_round, tie_break):
    rows = chains[0].shape[0]
    ridx = lax.broadcasted_iota(jnp.int32, chains[0].shape, 0)
    chains = list(chains)
    ranks = [jnp.full(c.shape, rounds, jnp.int32) for c in chains]
    for r in range(rounds):
        for i, s in enumerate(chains):
            m = jnp.max(s, axis=0, keepdims=True)
            if tie_break:
                first = jnp.min(jnp.where(s == m, ridx, rows), axis=0, keepdims=True)
                hit = ridx == first
            else:
                hit = s == m
            ranks[i] = jnp.where(hit, r, ranks[i])
            chains[i] = jnp.where(hit, -jnp.inf, s)
            on_round(i, r, m)
    return ranks


def _extract_topk(chains, rounds, on_round, rank_ref):
    ranks = _extract_rounds(chains, rounds, on_round, tie_break=False)
    most = jnp.float32(0.0)
    for i, rank in enumerate(ranks):
        rank_ref[i] = rank
        most = jnp.maximum(most, jnp.max(jnp.sum((rank < rounds).astype(F32), axis=0, keepdims=True)))

    @pl.when(most > rounds)
    def _():
        for i, rank in enumerate(_extract_rounds(chains, rounds, on_round, tie_break=True)):
            rank_ref[i] = rank


def _topk_kernel(qp_ref, sk_ref, r2_o, w2_o, k1_o, w1_o, tv_scr, cand_scr, kj_scr, rank_scr, crank_scr):
    tt = qp_ref.shape[0]
    half = PEER_QDIM // 2
    q = qp_ref[...]
    heads = range(TOPK_HEADS)

    scores = [_nt_dot(sk_ref[h, p], q[:, (2 * h + p) * half:(2 * h + p + 1) * half])
              for h in heads for p in range(2)]

    def keep(i, r, m):
        tv_scr[i, r:r + 1, :] = m

    _extract_topk(scores, PEER_TOPK, keep, rank_scr)

    for h in heads:
        tv1 = tv_scr[2 * h]
        tv2 = tv_scr[2 * h + 1]
        off = 0
        for j, cnt in enumerate(CAND_COUNTS):
            cand_scr[h, off:off + cnt, :] = tv1[j:j + 1, :] + tv2[0:cnt, :]
            off += cnt
        cand_scr[h, N_CAND:CAND_ROWS, :] = jnp.full((CAND_ROWS - N_CAND, tt), -jnp.inf, F32)
    cands = [cand_scr[h] for h in heads]
    _extract_topk(cands, PEER_TOPK, lambda i, r, m: None, crank_scr)

    def twice_bf16(x):
        hi = lax.bitcast_convert_type(x.astype(BF16).astype(F32), jnp.uint32)
        return lax.bitcast_convert_type(hi | (hi >> 16), F32)

    for h in heads:
        cand = cands[h]
        chosen = crank_scr[h] < PEER_TOPK
        cmax = cand[0:1, :]
        z = jnp.sum(jnp.where(chosen, jnp.exp(cand - cmax), 0.0), axis=0, keepdims=True)
        cnt_f = chosen.astype(F32)
        off = 0
        for j, cnt in enumerate(CAND_COUNTS):
            kj_scr[h, j:j + 1, :] = jnp.sum(cnt_f[off:off + cnt, :], axis=0, keepdims=True)
            off += cnt

        rank1 = rank_scr[2 * h]
        rank2 = rank_scr[2 * h + 1]
        k1 = jnp.zeros((N_KEYS, tt), F32)
        for j in range(PEER_TOPK):
            k1 = jnp.where(rank1 == j, kj_scr[h, j:j + 1, :], k1)
        k1_o[h] = twice_bf16(k1)
        w1_o[h] = twice_bf16(
            jnp.where(rank1 < PEER_TOPK, jnp.exp(scores[2 * h] - tv_scr[2 * h, 0:1, :]), 0.0) / z)
        r2_o[h] = rank2.astype(F32).astype(BF16)
        w2_o[h] = jnp.where(rank2 < PEER_TOPK, jnp.exp(scores[2 * h + 1] - tv_scr[2 * h + 1, 0:1, :]),
                            0.0).astype(BF16)


def _topk_call(qp, sub_keys):
    n = qp.shape[0]
    tt = ROW_BLOCK
    hp = TOPK_HEADS
    out = pl.BlockSpec((hp, N_KEYS, tt), lambda t, h: (h, 0, t))
    shape = jax.ShapeDtypeStruct((PEER_HEADS, N_KEYS, n), F32)
    shape_bf16 = jax.ShapeDtypeStruct((PEER_HEADS, N_KEYS, n), BF16)
    return pl.pallas_call(
        _topk_kernel,
        grid=(n // tt, PEER_HEADS // hp),
        in_specs=[
            pl.BlockSpec((tt, hp * PEER_QDIM), lambda t, h: (t, h)),
            pl.BlockSpec((hp, 2, N_KEYS, PEER_QDIM // 2), lambda t, h: (h, 0, 0, 0)),
        ],
        out_specs=[out, out, out, out],
        out_shape=[shape_bf16, shape_bf16, shape, shape],
        scratch_shapes=[
            pltpu.VMEM((2 * hp, PEER_TOPK, tt), F32),
            pltpu.VMEM((hp, CAND_ROWS, tt), F32),
            pltpu.VMEM((hp, PEER_TOPK, tt), F32),
            pltpu.VMEM((2 * hp, N_KEYS, tt), jnp.int32),
            pltpu.VMEM((hp, CAND_ROWS, tt), jnp.int32),
        ],
        compiler_params=_cparams(("arbitrary", "arbitrary")),
        name="peer_topk",
    )(qp, sub_keys)


_ERF_A = (-2.72614225801306e-10, 2.77068142495902e-08, -2.10102402082508e-06, -5.69250639462346e-05,
          -7.34990630326855e-04, -2.95459980854025e-03, -1.60960333262415e-02)
_ERF_B = (-1.45660718464996e-05, -2.13374055278905e-04, -1.68282697438203e-03, -7.37332916720468e-03,
          -1.42647390514189e-02)


def _gelu(x):
    dt = x.dtype
    lim = 4.0 * math.sqrt(2.0)
    xc = jnp.clip(x, -lim, lim)
    t = xc * xc
    deg_a = len(_ERF_A) - 1
    deg_b = len(_ERF_B) - 1
    a2 = [c * (0.5 / math.sqrt(2.0)) / 2.0 ** (deg_a - i) for i, c in enumerate(_ERF_A)]
    b2 = [c / 2.0 ** (deg_b - i) for i, c in enumerate(_ERF_B)]
    a2 = [c / b2[0] for c in a2]
    b2 = [c / b2[0] for c in b2]
    a = jnp.asarray(a2[0], dt)
    for c in a2[1:]:
        a = a * t + jnp.asarray(c, dt)
    b = t + jnp.asarray(b2[1], dt)
    for c in b2[2:]:
        b = b * t + jnp.asarray(c, dt)
    return x * (0.5 + xc * (a / b))


def _dense_kernel(hq_ref, u_ref, vt_ref, r2_ref, w2_ref, k1_ref, w1_ref, o_ref, h_scr, a_scr, *, n_eb):
    s = pl.program_id(0)
    eb = u_ref.shape[0]
    tt = hq_ref.shape[0]
    n_i1 = eb // N_KEYS

    @pl.when(s == 0)
    def _():
        h_scr[...] = jnp.zeros(h_scr.shape, BF16)
        a_scr[...] = jnp.zeros(a_scr.shape, BF16)

    @pl.when(jnp.maximum(s - 2, 0) % n_eb == 0)
    def _():
        o_ref[...] = jnp.zeros(o_ref.shape, F32)

    def rows16(ref, h, ii, cs):
        row = ref[h, ii:ii + 1, cs]
        return pltpu.bitcast(jnp.tile(row, (8, 1)), BF16)

    for c in range(tt // PEER_LANES):
        cs = slice(c * PEER_LANES, (c + 1) * PEER_LANES)

        o_ref[:, cs] += jnp.dot(vt_ref[...], a_scr[:, cs], preferred_element_type=F32)

        for ii in range(n_i1):
            rows = slice(ii * N_KEYS, (ii + 1) * N_KEYS)
            g = None
            for h in range(PEER_HEADS):
                k1b = jnp.concatenate([rows16(k1_ref, h, ii, cs)] * (N_KEYS // 16), axis=0)
                w1b = jnp.concatenate([rows16(w1_ref, h, ii, cs)] * (N_KEYS // 16), axis=0)
                keep = r2_ref[h, :, cs] < k1b
                gate = jnp.where(keep, w2_ref[h, :, cs] * w1b, 0.0)
                g = gate if g is None else g + gate
            a_scr[rows, cs] = _gelu(h_scr[rows, cs]) * g

        h_scr[:, cs] = _nt_dot(u_ref[...], hq_ref[cs, :]).astype(BF16)


def _dense_call(hq, u_tab, vt_tab, layer, r2, w2, k1, w1):
    n = hq.shape[0]
    n_exp = u_tab.shape[1]
    tt = PEER_TOKENS
    eb = PEER_EXPERTS
    n_eb = n_exp // eb
    steps = (n // tt) * n_eb

    def blk(s, lag):
        return jnp.clip(s - lag, 0, steps - 1)

    tab = pl.BlockSpec((PEER_HEADS, N_KEYS, tt), lambda s: (0, 0, blk(s, 1) // n_eb))
    row = pl.BlockSpec((PEER_HEADS, eb // N_KEYS, tt), lambda s: (0, blk(s, 1) % n_eb, blk(s, 1) // n_eb))
    return pl.pallas_call(
        functools.partial(_dense_kernel, n_eb=n_eb),
        grid=(steps + 2,),
        in_specs=[
            pl.BlockSpec((tt, D_MODEL), lambda s: (blk(s, 0) // n_eb, 0)),
            pl.BlockSpec((None, eb, D_MODEL), lambda s: (layer, blk(s, 0) % n_eb, 0)),
            pl.BlockSpec((None, D_MODEL, eb), lambda s: (layer, 0, blk(s, 2) % n_eb)),
            tab, tab, row, row,
        ],
        out_specs=pl.BlockSpec((D_MODEL, tt), lambda s: (0, blk(s, 2) // n_eb)),
        out_shape=jax.ShapeDtypeStruct((D_MODEL, n), F32),
        scratch_shapes=[pltpu.VMEM((eb, tt), BF16), pltpu.VMEM((eb, tt), BF16)],
        compiler_params=_cparams(("arbitrary",)),
        name="peer_dense",
    )(hq, u_tab, vt_tab, r2, w2, k1, w1)


def _ln2_kernel(x1_ref, ft_ref, mod_ref, lg_ref, lb_ref, o_ref):
    g2 = mod_ref[0, 5:6, :]
    f = ft_ref[...].T
    o_ref[...] = _layer_norm(DEEPNORM_ALPHA * x1_ref[...] + g2 * f, lg_ref[...], lb_ref[...])


def _ln2_call(x1, ft, mods, ln_g, ln_b, ctx_blocks, latent_only):
    n = x1.shape[0]
    tb = ROW_BLOCK
    skip = ctx_blocks if latent_only else 0
    return pl.pallas_call(
        _ln2_kernel,
        grid=(n // tb,),
        in_specs=[
            pl.BlockSpec((tb, D_MODEL), lambda i: (i, 0)),
            pl.BlockSpec((D_MODEL, tb), lambda i: (0, i)),
            pl.BlockSpec((1, N_MOD, D_MODEL), lambda i: (jnp.where(i < ctx_blocks, 0, 1), 0, 0)),
            pl.BlockSpec((1, D_MODEL), lambda i: (0, 0)),
            pl.BlockSpec((1, D_MODEL), lambda i: (0, 0)),
        ],
        out_specs=pl.BlockSpec((tb, D_MODEL), lambda i: (jnp.maximum(i - skip, 0), 0)),
        out_shape=jax.ShapeDtypeStruct((n - skip * tb, D_MODEL), F32),
        compiler_params=_cparams(("arbitrary",)),
        name="ln2",
    )(x1, ft, mods, ln_g, ln_b)


def _rope_tables(ctx_len, s_len):
    t = np.arange(s_len)
    pos = np.stack([t // GRID_W, t % GRID_W], axis=-1).astype(np.float64)
    quarter = HEAD_DIM // 4
    inv_freq = ROPE_THETA ** (-np.arange(quarter, dtype=np.float64) / quarter)
    ang = pos[:, :, None] * inv_freq
    cos = np.repeat(np.cos(ang), 2, axis=1).reshape(s_len, HEAD_DIM)
    sin = np.repeat(np.sin(ang), 2, axis=1).reshape(s_len, HEAD_DIM)
    first_half = (np.arange(HEAD_DIM) % 32) < 16
    sa = np.where(first_half, -sin, 0.0)
    sb = np.where(first_half, 0.0, sin)

    def full(tab, fill):
        tab = np.concatenate([np.full((ctx_len, HEAD_DIM), fill), tab], axis=0)
        return jnp.asarray(np.tile(tab, (1, 2)), F32)

    return full(cos, 1.0), full(sa, 0.0), full(sb, 0.0)


def kernel(x, c, ctx, c_ctx, w_mod, b_mod, w_in, q_norm, k_norm, conv_w, sink, w_branch, w_out,
           ln1_g, ln1_b, w_pq, sub_keys, u_tab, v_tab, ln2_g, ln2_b):
    bsz, s_len, d = x.shape
    ctx_len = ctx.shape[1]
    depth = w_mod.shape[0]
    assert bsz == 1 and d == D_MODEL and depth == DEPTH
    assert ctx_len % ROW_BLOCK == 0
    assert (ctx_len + s_len) % PEER_TOKENS == 0 and (ctx_len + s_len) % INPROJ_ROWS == 0
    ctx_blocks = ctx_len // ROW_BLOCK

    x_all = jnp.concatenate([ctx[0], x[0]], axis=0)
    cos_t, sa_t, sb_t = _rope_tables(ctx_len, s_len)

    cc = jnp.zeros((8, D_MODEL), F32).at[0].set(c_ctx).at[1].set(c[0])
    mod_all = _mod_call(cc, w_mod.astype(BF16), b_mod[:, None, :])
    u_all = u_tab.astype(BF16)
    vt_all = jnp.swapaxes(v_tab, 1, 2).astype(BF16)

    for layer in range(depth):
        mods = mod_all[layer, 0:2].reshape(2, N_MOD, D_MODEL)
        proj = _inproj_call(x_all, mods, w_in[layer].astype(BF16), ctx_len)
        qg = jnp.tile(q_norm[layer], 2)[None, :]
        kg = jnp.tile(k_norm[layer], 2)[None, :]
        qa, ka, va, qc, kc, vc = _prep_call(proj, cos_t, sa_t, sb_t, qg, kg)
        ya = _attn_a_call(qa, ka, va, ctx_len)
        yc = _attn_c_call(sink[layer], qc, kc, vc, ctx_len)
        x1, hq, qp = _merge_call(x_all, mods, ya, yc, proj, conv_w[layer],
                                 w_branch[layer].astype(BF16), w_out[layer].astype(BF16),
                                 ln1_g[layer][None, :], ln1_b[layer][None, :],
                                 w_pq[layer].astype(BF16), ctx_blocks)
        r2, w2, k1, w1 = _topk_call(qp, sub_keys[layer].astype(BF16))
        ft = _dense_call(hq, u_all, vt_all, layer, r2, w2, k1, w1)
        x_all = _ln2_call(x1, ft, mods, ln2_g[layer][None, :], ln2_b[layer][None, :], ctx_blocks,
                          latent_only=(layer == depth - 1))

    return x_all[None]
```

```python
import functools
import math

import jax
import jax.numpy as jnp
import numpy as np
from jax import lax
from jax.experimental import pallas as pl
from jax.experimental.pallas import tpu as pltpu

F32 = jnp.float32
BF16 = jnp.bfloat16

D_MODEL = 1024
HEAD_DIM = 64
N_HEADS = 8
N_KV = 2
GROUP = N_HEADS // N_KV
GRID_W = 64
WINDOW = 128
CONV_W = 3
B_WIDTH = 512
N_MOD = 6
ROPE_THETA = 10000.0
PEER_HEADS = 8
N_KEYS = 128
PEER_TOPK = 16
PEER_QDIM = 256
DEPTH = 2
IN_COLS = 6144
DEEPNORM_ALPHA = (2 * DEPTH) ** 0.25
LN_EPS = 1e-6
RMS_EPS = 1e-6
NEG_INF = -1e30
ATTN_SCALE = HEAD_DIM ** -0.5

ROW_BLOCK = 256
INPROJ_ROWS = 1280
KEY_CHUNK = 1024
BOUND_SLACK = 1.0 + 2.0 ** -6
SAFE_ROW_SUM = 1e-30
PEER_TOKENS = 1280
PEER_LANES = 256
TOPK_HEADS = 8
PEER_EXPERTS = 1024
VMEM_LIMIT = 56 * 1024 * 1024

CAND_COUNTS = tuple(PEER_TOPK // (j + 1) for j in range(PEER_TOPK))
N_CAND = sum(CAND_COUNTS)
CAND_ROWS = 56


def _cparams(sem):
    return pltpu.CompilerParams(dimension_semantics=sem, vmem_limit_bytes=VMEM_LIMIT)


def _nt_dot(a, b):
    return lax.dot_general(a, b, (((1,), (1,)), ((), ())), preferred_element_type=F32)


def _layer_norm(r, g, b):
    mu = jnp.mean(r, axis=-1, keepdims=True)
    rc = r - mu
    var = jnp.mean(rc * rc, axis=-1, keepdims=True)
    return rc * lax.rsqrt(var + LN_EPS) * g + b


def _mod_kernel(c_ref, w_ref, b_ref, o_ref):
    c = c_ref[...]
    h = (c * jax.nn.sigmoid(c)).astype(BF16)
    o_ref[0] = jnp.dot(h, w_ref[0], preferred_element_type=F32) + b_ref[0]


def _mod_call(cc, w_mod, b_mod):
    depth = w_mod.shape[0]
    cols = w_mod.shape[2]
    tn = 1024
    return pl.pallas_call(
        _mod_kernel,
        grid=(depth, cols // tn),
        in_specs=[
            pl.BlockSpec((8, D_MODEL), lambda l, j: (0, 0)),
            pl.BlockSpec((1, D_MODEL, tn), lambda l, j: (l, 0, j)),
            pl.BlockSpec((1, 1, tn), lambda l, j: (l, 0, j)),
        ],
        out_specs=pl.BlockSpec((1, 8, tn), lambda l, j: (l, 0, j)),
        out_shape=jax.ShapeDtypeStruct((depth, 8, cols), F32),
        compiler_params=_cparams(("arbitrary", "arbitrary")),
        name="mod",
    )(cc, w_mod, b_mod)


def _inproj_kernel(x_ref, mod_ref, w_ref, o_ref, hm_scr, *, ctx_len):
    tb = x_ref.shape[0]

    @pl.when(pl.program_id(1) == 0)
    def _():
        row = pl.program_id(0) * tb + lax.broadcasted_iota(jnp.int32, (tb, 1), 0)
        is_ctx = row < ctx_len
        sh = jnp.where(is_ctx, mod_ref[0, 0:1, :], mod_ref[1, 0:1, :])
        sc = jnp.where(is_ctx, mod_ref[0, 1:2, :], mod_ref[1, 1:2, :])
        hm_scr[...] = (x_ref[...] * (1.0 + sc) + sh).astype(BF16)

    o_ref[...] = jnp.dot(hm_scr[...], w_ref[...], preferred_element_type=F32)


def _inproj_call(x_all, mods, w_in, ctx_len):
    n = x_all.shape[0]
    tn = 1024
    tb = INPROJ_ROWS
    return pl.pallas_call(
        functools.partial(_inproj_kernel, ctx_len=ctx_len),
        grid=(n // tb, IN_COLS // tn),
        in_specs=[
            pl.BlockSpec((tb, D_MODEL), lambda i, j: (i, 0)),
            pl.BlockSpec((2, N_MOD, D_MODEL), lambda i, j: (0, 0, 0)),
            pl.BlockSpec((D_MODEL, tn), lambda i, j: (0, j)),
        ],
        out_specs=pl.BlockSpec((tb, tn), lambda i, j: (i, j)),
        out_shape=jax.ShapeDtypeStruct((n, IN_COLS), F32),
        scratch_shapes=[pltpu.VMEM((tb, D_MODEL), BF16)],
        compiler_params=_cparams(("arbitrary", "arbitrary")),
        name="inproj",
    )(x_all, mods, w_in)


def _prep_kernel(p_ref, cos_ref, sa_ref, sb_ref, qg_ref, kg_ref,
                 qa_o, ka_o, va_o, qc_o, kc_o, vc_o):
    x = p_ref[...]
    cos = cos_ref[...]
    sa = sa_ref[...]
    sb = sb_ref[...]

    def tile(t, width):
        reps = width // 128
        return t if reps == 1 else jnp.concatenate([t] * reps, axis=1)

    def rope(y):
        w = y.shape[1]
        return (y * tile(cos, w)
                + pltpu.roll(y, w - 16, 1) * tile(sa, w)
                + pltpu.roll(y, 16, 1) * tile(sb, w))

    def inv_rms(xh):
        return lax.rsqrt(jnp.mean(xh * xh, axis=-1, keepdims=True) + RMS_EPS)

    qa = x[:, 0:512]
    ka = x[:, 512:640]
    va = x[:, 640:768]
    qc = x[:, 768:1280]
    kc = x[:, 1280:1408]
    vc = x[:, 1408:1536]

    qa_r = rope(qa * tile(qg_ref[...], 512))
    ka_r = rope(ka * kg_ref[...])
    qc_r = rope(qc)
    kc_r = rope(kc)

    for h in range(N_HEADS):
        sl = slice(h * HEAD_DIM, (h + 1) * HEAD_DIM)
        qa_o[h, :, 0:HEAD_DIM] = (qa_r[:, sl] * (inv_rms(qa[:, sl]) * ATTN_SCALE)).astype(BF16)
        qa_o[h, :, HEAD_DIM:2 * HEAD_DIM] = jnp.zeros((x.shape[0], HEAD_DIM), BF16)
        qc_o[h] = (qc_r[:, sl] * ATTN_SCALE).astype(BF16)
    lane = lax.broadcasted_iota(jnp.int32, (x.shape[0], HEAD_DIM), 1)
    ones_col = jnp.where(lane == 0, 1.0, 0.0).astype(BF16)
    for h in range(N_KV):
        sl = slice(h * HEAD_DIM, (h + 1) * HEAD_DIM)
        ka_o[h, :, 0:HEAD_DIM] = (ka_r[:, sl] * inv_rms(ka[:, sl])).astype(BF16)
        ka_o[h, :, HEAD_DIM:2 * HEAD_DIM] = ones_col
        va_o[h, :, 0:HEAD_DIM] = va[:, sl].astype(BF16)
        va_o[h, :, HEAD_DIM:2 * HEAD_DIM] = ones_col
        kc_o[h] = kc_r[:, sl].astype(BF16)
        vc_o[h, :, 0:HEAD_DIM] = vc[:, sl].astype(BF16)
        vc_o[h, :, HEAD_DIM:2 * HEAD_DIM] = ones_col


def _prep_call(proj, cos_t, sa_t, sb_t, qg, kg):
    n = proj.shape[0]
    tb = ROW_BLOCK
    tab = pl.BlockSpec((tb, 128), lambda i: (i, 0))
    gain = pl.BlockSpec((1, 128), lambda i: (0, 0))

    def hm(nh, width=HEAD_DIM):
        return pl.BlockSpec((nh, tb, width), lambda i: (0, i, 0))

    def hshape(nh, width=HEAD_DIM):
        return jax.ShapeDtypeStruct((nh, n, width), BF16)

    return pl.pallas_call(
        _prep_kernel,
        grid=(n // tb,),
        in_specs=[pl.BlockSpec((tb, 1536), lambda i: (i, 0)), tab, tab, tab, gain, gain],
        out_specs=[hm(N_HEADS, 2 * HEAD_DIM), hm(N_KV, 2 * HEAD_DIM), hm(N_KV, 2 * HEAD_DIM),
                   hm(N_HEADS), hm(N_KV), hm(N_KV, 2 * HEAD_DIM)],
        out_shape=[hshape(N_HEADS, 2 * HEAD_DIM), hshape(N_KV, 2 * HEAD_DIM), hshape(N_KV, 2 * HEAD_DIM),
                   hshape(N_HEADS), hshape(N_KV), hshape(N_KV, 2 * HEAD_DIM)],
        compiler_params=_cparams(("arbitrary",)),
        name="prep",
    )(proj, cos_t, sa_t, sb_t, qg, kg)


def _store_heads(o_ref, o, tq):
    for g in range(GROUP):
        o_ref[:, g * HEAD_DIM:(g + 1) * HEAD_DIM] = o[g * tq:(g + 1) * tq, :].astype(BF16)


def _attn_a_kernel(q_ref, k_ref, v_ref, o_ref, s_scr, mx_scr, m_scr, acc_scr, kmax_scr, *,
                   ctx_len, ctx_blocks, lat_chunks, unroll):
    i = pl.program_id(1)
    tq = q_ref.shape[1]
    rows = GROUP * tq
    key_chunk = s_scr.shape[2]
    q0 = q_ref[...].reshape(rows, 2 * HEAD_DIM)

    @pl.when(i == 0)
    def _():
        kk = k_ref[0, :, 0:HEAD_DIM].astype(F32)
        k_norm2 = jnp.max(jnp.sum(kk * kk, axis=-1, keepdims=True), axis=0, keepdims=True)
        kmax_scr[...] = jnp.broadcast_to(jnp.sqrt(k_norm2), kmax_scr.shape)

    def chunk_off(c):
        return pl.multiple_of(ctx_len + c * key_chunk, 128)

    def run(q, online):
        def scores(slot, off, width):
            s = _nt_dot(q, k_ref[0, pl.ds(off, width), :])
            s_scr[slot, :, 0:width] = s
            if online:
                mx_scr[slot] = jnp.broadcast_to(jnp.max(s, axis=-1, keepdims=True), (rows, 128))

        def update(slot, off, width):
            s = s_scr[slot, :, 0:width]
            v = v_ref[0, pl.ds(off, width), :]
            if online:
                m_old = m_scr[...]
                m_new = jnp.maximum(m_old, mx_scr[slot])
                p = jnp.exp(s - jnp.concatenate([m_new] * (width // 128), axis=1))
                acc_scr[...] = jnp.exp(m_old - m_new) * acc_scr[...] + jnp.dot(
                    p.astype(BF16), v, preferred_element_type=F32)
                m_scr[...] = m_new
            else:
                acc_scr[...] += jnp.dot(jnp.exp(s).astype(BF16), v, preferred_element_type=F32)

        if online:
            m_scr[...] = jnp.full(m_scr.shape, -jnp.inf, F32)
        acc_scr[...] = jnp.zeros(acc_scr.shape, F32)
        scores(0, 0, ctx_len)
        update(0, 0, ctx_len)

        @pl.when(i >= ctx_blocks)
        def _():
            scores(0, chunk_off(0), key_chunk)

            def body(it, carry):
                for u in range(0, unroll, 2):
                    c = unroll * it + u
                    scores(1, chunk_off(c + 1), key_chunk)
                    update(0, chunk_off(c), key_chunk)
                    scores(0, chunk_off(jnp.minimum(c + 2, lat_chunks - 1)), key_chunk)
                    update(1, chunk_off(c + 1), key_chunk)
                return carry

            lax.fori_loop(0, lat_chunks // unroll, body, 0)

        acc = acc_scr[...]
        _store_heads(o_ref, acc[:, 0:HEAD_DIM] / acc[:, HEAD_DIM:HEAD_DIM + 1], tq)

    qf = q0.astype(F32)
    q_norm = jnp.sqrt(jnp.sum(qf * qf, axis=-1, keepdims=True))
    bound = (q_norm * kmax_scr[0:1, 0:1] * BOUND_SLACK).astype(BF16).astype(F32)
    shift_lane = lax.broadcasted_iota(jnp.int32, qf.shape, 1) == HEAD_DIM
    run(jnp.where(shift_lane, -bound, qf).astype(BF16), online=False)

    @pl.when(jnp.logical_not(jnp.min(acc_scr[:, HEAD_DIM:HEAD_DIM + 1]) > SAFE_ROW_SUM))
    def _():
        run(q0, online=True)


def _attn_a_call(q, k, v, ctx_len):
    n = q.shape[1]
    tq = ROW_BLOCK
    s_len = n - ctx_len
    key_chunk = KEY_CHUNK if s_len % (2 * KEY_CHUNK) == 0 else KEY_CHUNK // 2
    lat_chunks = s_len // key_chunk
    assert s_len % key_chunk == 0 and lat_chunks % 2 == 0
    unroll = max(u for u in (8, 4, 2) if lat_chunks % u == 0)
    kern = functools.partial(_attn_a_kernel, ctx_len=ctx_len, ctx_blocks=ctx_len // tq, lat_chunks=lat_chunks,
                             unroll=unroll)
    return pl.pallas_call(
        kern,
        grid=(N_KV, n // tq),
        in_specs=[
            pl.BlockSpec((GROUP, tq, 2 * HEAD_DIM), lambda j, i: (j, i, 0)),
            pl.BlockSpec((1, n, 2 * HEAD_DIM), lambda j, i: (j, 0, 0)),
            pl.BlockSpec((1, n, 2 * HEAD_DIM), lambda j, i: (j, 0, 0)),
        ],
        out_specs=pl.BlockSpec((tq, GROUP * HEAD_DIM), lambda j, i: (i, j)),
        out_shape=jax.ShapeDtypeStruct((n, N_HEADS * HEAD_DIM), BF16),
        scratch_shapes=[
            pltpu.VMEM((2, GROUP * tq, key_chunk), F32),
            pltpu.VMEM((2, GROUP * tq, 128), F32),
            pltpu.VMEM((GROUP * tq, 128), F32),
            pltpu.VMEM((GROUP * tq, 2 * HEAD_DIM), F32),
            pltpu.VMEM((8, 128), F32),
        ],
        compiler_params=_cparams(("arbitrary", "arbitrary")),
        name="attn_global",
    )(q, k, v)


def _attn_c_kernel(sink_ref, q_ref, k_ref, v_ref, o_ref, *, ctx_len, ctx_blocks, s_len):
    j = pl.program_id(0)
    i = pl.program_id(1)
    tq = q_ref.shape[1]
    rows = GROUP * tq
    n_loc = tq + 2 * WINDOW
    q = q_ref[...].reshape(rows, HEAD_DIM)

    p0 = i * tq - ctx_len
    ws = pl.multiple_of(jnp.clip(p0 - WINDOW, 0, s_len - n_loc), 128)
    k_loc = k_ref[0, pl.ds(ctx_len + ws, n_loc), :]
    v_loc = v_ref[0, pl.ds(ctx_len + ws, n_loc), :]
    k_ctx = k_ref[0, 0:ctx_len, :]
    v_ctx = v_ref[0, 0:ctx_len, :]

    dist = (lax.broadcasted_iota(jnp.int32, (tq, n_loc), 0) - lax.broadcasted_iota(jnp.int32, (tq, n_loc), 1)
            + (p0 - ws))
    valid = (jnp.abs(dist) <= WINDOW) & (i >= ctx_blocks)
    s_loc = jnp.where(valid[None], _nt_dot(q, k_loc).reshape(GROUP, tq, n_loc), NEG_INF).reshape(rows, n_loc)
    s_ctx = _nt_dot(q, k_ctx)

    grp = lax.broadcasted_iota(jnp.int32, (rows, 1), 0) // tq
    sink = jnp.full((rows, 1), sink_ref[j * GROUP + GROUP - 1], F32)
    for g in range(GROUP - 1):
        sink = jnp.where(grp == g, sink_ref[j * GROUP + g], sink)

    m = jnp.maximum(jnp.maximum(jnp.max(s_loc, axis=-1, keepdims=True),
                                jnp.max(s_ctx, axis=-1, keepdims=True)), sink)
    acc = (jnp.dot(jnp.exp(s_loc - m).astype(BF16), v_loc, preferred_element_type=F32)
           + jnp.dot(jnp.exp(s_ctx - m).astype(BF16), v_ctx, preferred_element_type=F32))
    o = acc[:, 0:HEAD_DIM] / (acc[:, HEAD_DIM:HEAD_DIM + 1] + jnp.exp(sink - m))
    _store_heads(o_ref, o, tq)


def _attn_c_call(sink, q, k, v, ctx_len):
    n = q.shape[1]
    tq = ROW_BLOCK
    kern = functools.partial(_attn_c_kernel, ctx_len=ctx_len, ctx_blocks=ctx_len // tq, s_len=n - ctx_len)
    return pl.pallas_call(
        kern,
        grid=(N_KV, n // tq),
        in_specs=[
            pl.BlockSpec(memory_space=pltpu.SMEM),
            pl.BlockSpec((GROUP, tq, HEAD_DIM), lambda j, i: (j, i, 0)),
            pl.BlockSpec((1, n, HEAD_DIM), lambda j, i: (j, 0, 0)),
            pl.BlockSpec((1, n, 2 * HEAD_DIM), lambda j, i: (j, 0, 0)),
        ],
        out_specs=pl.BlockSpec((tq, GROUP * HEAD_DIM), lambda j, i: (i, j)),
        out_shape=jax.ShapeDtypeStruct((n, N_HEADS * HEAD_DIM), BF16),
        compiler_params=_cparams(("arbitrary", "arbitrary")),
        name="attn_window",
    )(sink, q, k, v)


def _merge_kernel(x_ref, mod_ref, ya_ref, yc_ref, xb_ref, gb_ref, gc_ref,
                  xbp_ref, gcp_ref, xbn_ref, gcn_ref, gl_ref, cw_ref, wb_ref, wo_ref,
                  lg_ref, lb_ref, wpq_ref, x1_ref, hq_ref, qp_ref, *, n_blocks, ctx_blocks):
    i = pl.program_id(0)
    tb = x_ref.shape[0]

    z = gc_ref[...] * xb_ref[...]
    z_before = gcp_ref[7:8, :] * xbp_ref[7:8, :]
    z_after = gcn_ref[0:1, :] * xbn_ref[0:1, :]
    seq_start = (i == 0) | (i == ctx_blocks)
    seq_end = (i == ctx_blocks - 1) | (i == n_blocks - 1)
    z_before = jnp.where(seq_start, 0.0, z_before)
    z_after = jnp.where(seq_end, 0.0, z_after)
    rid = lax.broadcasted_iota(jnp.int32, z.shape, 0)
    z_prev = jnp.where(rid == 0, z_before, pltpu.roll(z, 1, 0))
    z_next = jnp.where(rid == tb - 1, z_after, pltpu.roll(z, tb - 1, 0))
    cw = cw_ref[...]
    yb = gb_ref[...] * (cw[0:1, :] * z_prev + cw[1:2, :] * z + cw[2:3, :] * z_next)

    pa = jnp.dot(ya_ref[...], wb_ref[0], preferred_element_type=F32)
    pb = jnp.dot(yb.astype(BF16), wb_ref[1], preferred_element_type=F32)
    pc = jnp.dot(yc_ref[...], wb_ref[2], preferred_element_type=F32)
    gates = jax.nn.sigmoid(gl_ref[...])
    m = (gates[:, 0:D_MODEL] * pa + gates[:, D_MODEL:2 * D_MODEL] * pb
         + gates[:, 2 * D_MODEL:3 * D_MODEL] * pc)
    y = jnp.dot(m.astype(BF16), wo_ref[...], preferred_element_type=F32)

    g1 = mod_ref[0, 2:3, :]
    sh2 = mod_ref[0, 3:4, :]
    sc2 = mod_ref[0, 4:5, :]
    x1 = _layer_norm(DEEPNORM_ALPHA * x_ref[...] + g1 * y, lg_ref[...], lb_ref[...])
    x1_ref[...] = x1
    hq = (x1 * (1.0 + sc2) + sh2).astype(BF16)
    hq_ref[...] = hq
    qp_ref[...] = jnp.dot(hq, wpq_ref[...], preferred_element_type=F32).astype(BF16)


def _merge_call(x_all, mods, ya, yc, proj, conv_w, w_branch, w_out, ln_g, ln_b, w_pq, ctx_blocks):
    n = x_all.shape[0]
    tb = ROW_BLOCK
    nb = n // tb
    halo = tb // 8
    last8 = n // 8 - 1
    kern = functools.partial(_merge_kernel, n_blocks=nb, ctx_blocks=ctx_blocks)
    qcols = PEER_HEADS * PEER_QDIM

    def full(shape):
        return pl.BlockSpec(shape, lambda i: (0,) * len(shape))

    def cols512(c):
        return pl.BlockSpec((tb, B_WIDTH), lambda i: (i, c))

    def before(c):
        return pl.BlockSpec((8, B_WIDTH), lambda i: (jnp.maximum(i * halo - 1, 0), c))

    def after(c):
        return pl.BlockSpec((8, B_WIDTH), lambda i: (jnp.minimum((i + 1) * halo, last8), c))

    return pl.pallas_call(
        kern,
        grid=(nb,),
        in_specs=[
            pl.BlockSpec((tb, D_MODEL), lambda i: (i, 0)),
            pl.BlockSpec((1, N_MOD, D_MODEL), lambda i: (jnp.where(i < ctx_blocks, 0, 1), 0, 0)),
            pl.BlockSpec((tb, N_HEADS * HEAD_DIM), lambda i: (i, 0)),
            pl.BlockSpec((tb, N_HEADS * HEAD_DIM), lambda i: (i, 0)),
            cols512(3), cols512(4), cols512(5),
            before(3), before(5), after(3), after(5),
            pl.BlockSpec((tb, 3 * D_MODEL), lambda i: (i, 1)),
            full((CONV_W, B_WIDTH)),
            full((3, B_WIDTH, D_MODEL)),
            full((D_MODEL, D_MODEL)),
            full((1, D_MODEL)), full((1, D_MODEL)),
            full((D_MODEL, qcols)),
        ],
        out_specs=[
            pl.BlockSpec((tb, D_MODEL), lambda i: (i, 0)),
            pl.BlockSpec((tb, D_MODEL), lambda i: (i, 0)),
            pl.BlockSpec((tb, qcols), lambda i: (i, 0)),
        ],
        out_shape=[
            jax.ShapeDtypeStruct((n, D_MODEL), F32),
            jax.ShapeDtypeStruct((n, D_MODEL), BF16),
            jax.ShapeDtypeStruct((n, qcols), BF16),
        ],
        compiler_params=_cparams(("arbitrary",)),
        name="merge",
    )(x_all, mods, ya, yc, proj, proj, proj, proj, proj, proj, proj, proj,
      conv_w, w_branch, w_out, ln_g, ln_b, w_pq)


def _extract_rounds(chains, rounds, on_round, tie_break):
    rows = chains[0].shape[0]
    ridx = lax.broadcasted_iota(jnp.int32, chains[0].shape, 0)
    chains = list(chains)
    ranks = [jnp.full(c.shape, rounds, jnp.int32) for c in chains]
    for r in range(rounds):
        for i, s in enumerate(chains):
            m = jnp.max(s, axis=0, keepdims=True)
            if tie_break:
                first = jnp.min(jnp.where(s == m, ridx, rows), axis=0, keepdims=True)
                hit = ridx == first
            else:
                hit = s == m
            ranks[i] = jnp.where(hit, r, ranks[i])
            chains[i] = jnp.where(hit, -jnp.inf, s)
            on_round(i, r, m)
    return ranks


def _extract_topk(chains, rounds, on_round, rank_ref):
    ranks = _extract_rounds(chains, rounds, on_round, tie_break=False)
    most = jnp.float32(0.0)
    for i, rank in enumerate(ranks):
        rank_ref[i] = rank
        most = jnp.maximum(most, jnp.max(jnp.sum((rank < rounds).astype(F32), axis=0, keepdims=True)))

    @pl.when(most > rounds)
    def _():
        for i, rank in enumerate(_extract_rounds(chains, rounds, on_round, tie_break=True)):
            rank_ref[i] = rank


def _topk_kernel(qp_ref, sk_ref, r2_o, w2_o, k1_o, w1_o, tv_scr, cand_scr, kj_scr, rank_scr, crank_scr):
    tt = qp_ref.shape[0]
    half = PEER_QDIM // 2
    q = qp_ref[...]
    heads = range(TOPK_HEADS)

    scores = [_nt_dot(sk_ref[h, p], q[:, (2 * h + p) * half:(2 * h + p + 1) * half])
              for h in heads for p in range(2)]

    def keep(i, r, m):
        tv_scr[i, r:r + 1, :] = m

    _extract_topk(scores, PEER_TOPK, keep, rank_scr)

    for h in heads:
        tv1 = tv_scr[2 * h]
        tv2 = tv_scr[2 * h + 1]
        off = 0
        for j, cnt in enumerate(CAND_COUNTS):
            cand_scr[h, off:off + cnt, :] = tv1[j:j + 1, :] + tv2[0:cnt, :]
            off += cnt
        cand_scr[h, N_CAND:CAND_ROWS, :] = jnp.full((CAND_ROWS - N_CAND, tt), -jnp.inf, F32)
    cands = [cand_scr[h] for h in heads]
    _extract_topk(cands, PEER_TOPK, lambda i, r, m: None, crank_scr)

    def twice_bf16(x):
        hi = lax.bitcast_convert_type(x.astype(BF16).astype(F32), jnp.uint32)
        return lax.bitcast_convert_type(hi | (hi >> 16), F32)

    for h in heads:
        cand = cands[h]
        chosen = crank_scr[h] < PEER_TOPK
        cmax = cand[0:1, :]
        z = jnp.sum(jnp.where(chosen, jnp.exp(cand - cmax), 0.0), axis=0, keepdims=True)
        cnt_f = chosen.astype(F32)
        off = 0
        for j, cnt in enumerate(CAND_COUNTS):
            kj_scr[h, j:j + 1, :] = jnp.sum(cnt_f[off:off + cnt, :], axis=0, keepdims=True)
            off += cnt

        rank1 = rank_scr[2 * h]
        rank2 = rank_scr[2 * h + 1]
        k1 = jnp.zeros((N_KEYS, tt), F32)
        for j in range(PEER_TOPK):
            k1 = jnp.where(rank1 == j, kj_scr[h, j:j + 1, :], k1)
        k1_o[h] = twice_bf16(k1)
        w1_o[h] = twice_bf16(
            jnp.where(rank1 < PEER_TOPK, jnp.exp(scores[2 * h] - tv_scr[2 * h, 0:1, :]), 0.0) / z)
        r2_o[h] = rank2.astype(F32).astype(BF16)
        w2_o[h] = jnp.where(rank2 < PEER_TOPK, jnp.exp(scores[2 * h + 1] - tv_scr[2 * h + 1, 0:1, :]),
                            0.0).astype(BF16)


def _topk_call(qp, sub_keys):
    n = qp.shape[0]
    tt = ROW_BLOCK
    hp = TOPK_HEADS
    out = pl.BlockSpec((hp, N_KEYS, tt), lambda t, h: (h, 0, t))
    shape = jax.ShapeDtypeStruct((PEER_HEADS, N_KEYS, n), F32)
    shape_bf16 = jax.ShapeDtypeStruct((PEER_HEADS, N_KEYS, n), BF16)
    return pl.pallas_call(
        _topk_kernel,
        grid=(n // tt, PEER_HEADS // hp),
        in_specs=[
            pl.BlockSpec((tt, hp * PEER_QDIM), lambda t, h: (t, h)),
            pl.BlockSpec((hp, 2, N_KEYS, PEER_QDIM // 2), lambda t, h: (h, 0, 0, 0)),
        ],
        out_specs=[out, out, out, out],
        out_shape=[shape_bf16, shape_bf16, shape, shape],
        scratch_shapes=[
            pltpu.VMEM((2 * hp, PEER_TOPK, tt), F32),
            pltpu.VMEM((hp, CAND_ROWS, tt), F32),
            pltpu.VMEM((hp, PEER_TOPK, tt), F32),
            pltpu.VMEM((2 * hp, N_KEYS, tt), jnp.int32),
            pltpu.VMEM((hp, CAND_ROWS, tt), jnp.int32),
        ],
        compiler_params=_cparams(("arbitrary", "arbitrary")),
        name="peer_topk",
    )(qp, sub_keys)


_ERF_A = (-2.72614225801306e-10, 2.77068142495902e-08, -2.10102402082508e-06, -5.69250639462346e-05,
          -7.34990630326855e-04, -2.95459980854025e-03, -1.60960333262415e-02)
_ERF_B = (-1.45660718464996e-05, -2.13374055278905e-04, -1.68282697438203e-03, -7.37332916720468e-03,
          -1.42647390514189e-02)


def _gelu(x):
    dt = x.dtype
    lim = 4.0 * math.sqrt(2.0)
    xc = jnp.clip(x, -lim, lim)
    t = xc * xc
    deg_a = len(_ERF_A) - 1
    deg_b = len(_ERF_B) - 1
    a2 = [c * (0.5 / math.sqrt(2.0)) / 2.0 ** (deg_a - i) for i, c in enumerate(_ERF_A)]
    b2 = [c / 2.0 ** (deg_b - i) for i, c in enumerate(_ERF_B)]
    a2 = [c / b2[0] for c in a2]
    b2 = [c / b2[0] for c in b2]
    a = jnp.asarray(a2[0], dt)
    for c in a2[1:]:
        a = a * t + jnp.asarray(c, dt)
    b = t + jnp.asarray(b2[1], dt)
    for c in b2[2:]:
        b = b * t + jnp.asarray(c, dt)
    return x * (0.5 + xc * (a / b))


def _dense_kernel(hq_ref, u_ref, vt_ref, r2_ref, w2_ref, k1_ref, w1_ref, o_ref, h_scr, a_scr, *, n_eb):
    s = pl.program_id(0)
    eb = u_ref.shape[0]
    tt = hq_ref.shape[0]
    n_i1 = eb // N_KEYS

    @pl.when(s == 0)
    def _():
        h_scr[...] = jnp.zeros(h_scr.shape, BF16)
        a_scr[...] = jnp.zeros(a_scr.shape, BF16)

    @pl.when(jnp.maximum(s - 2, 0) % n_eb == 0)
    def _():
        o_ref[...] = jnp.zeros(o_ref.shape, F32)

    def rows16(ref, h, ii, cs):
        row = ref[h, ii:ii + 1, cs]
        return pltpu.bitcast(jnp.tile(row, (8, 1)), BF16)

    def chunk(c, carry):
        cs = pl.ds(pl.multiple_of(c * PEER_LANES, PEER_LANES), PEER_LANES)

        o_ref[:, cs] += jnp.dot(vt_ref[...], a_scr[:, cs], preferred_element_type=F32)

        for ii in range(n_i1):
            rows = slice(ii * N_KEYS, (ii + 1) * N_KEYS)
            g = None
            for h in range(PEER_HEADS):
                k1b = jnp.concatenate([rows16(k1_ref, h, ii, cs)] * (N_KEYS // 16), axis=0)
                w1b = jnp.concatenate([rows16(w1_ref, h, ii, cs)] * (N_KEYS // 16), axis=0)
                keep = r2_ref[h, :, cs] < k1b
                gate = jnp.where(keep, w2_ref[h, :, cs] * w1b, 0.0)
                g = gate if g is None else g + gate
            a_scr[rows, cs] = _gelu(h_scr[rows, cs]) * g

        h_scr[:, cs] = _nt_dot(u_ref[...], hq_ref[cs, :]).astype(BF16)
        return carry

    lax.fori_loop(0, tt // PEER_LANES, chunk, 0)


def _dense_call(hq, u_tab, vt_tab, layer, r2, w2, k1, w1):
    n = hq.shape[0]
    n_exp = u_tab.shape[1]
    tt = PEER_TOKENS
    eb = PEER_EXPERTS
    n_eb = n_exp // eb
    steps = (n // tt) * n_eb

    def blk(s, lag):
        return jnp.clip(s - lag, 0, steps - 1)

    tab = pl.BlockSpec((PEER_HEADS, N_KEYS, tt), lambda s: (0, 0, blk(s, 1) // n_eb))
    row = pl.BlockSpec((PEER_HEADS, eb // N_KEYS, tt), lambda s: (0, blk(s, 1) % n_eb, blk(s, 1) // n_eb))
    return pl.pallas_call(
        functools.partial(_dense_kernel, n_eb=n_eb),
        grid=(steps + 2,),
        in_specs=[
            pl.BlockSpec((tt, D_MODEL), lambda s: (blk(s, 0) // n_eb, 0)),
            pl.BlockSpec((None, eb, D_MODEL), lambda s: (layer, blk(s, 0) % n_eb, 0)),
            pl.BlockSpec((None, D_MODEL, eb), lambda s: (layer, 0, blk(s, 2) % n_eb)),
            tab, tab, row, row,
        ],
        out_specs=pl.BlockSpec((D_MODEL, tt), lambda s: (0, blk(s, 2) // n_eb)),
        out_shape=jax.ShapeDtypeStruct((D_MODEL, n), F32),
        scratch_shapes=[pltpu.VMEM((eb, tt), BF16), pltpu.VMEM((eb, tt), BF16)],
        compiler_params=_cparams(("arbitrary",)),
        name="peer_dense",
    )(hq, u_tab, vt_tab, r2, w2, k1, w1)


def _ln2_kernel(x1_ref, ft_ref, mod_ref, lg_ref, lb_ref, o_ref):
    g2 = mod_ref[0, 5:6, :]
    f = ft_ref[...].T
    o_ref[...] = _layer_norm(DEEPNORM_ALPHA * x1_ref[...] + g2 * f, lg_ref[...], lb_ref[...])


def _ln2_call(x1, ft, mods, ln_g, ln_b, ctx_blocks, latent_only):
    n = x1.shape[0]
    tb = ROW_BLOCK
    skip = ctx_blocks if latent_only else 0
    return pl.pallas_call(
        _ln2_kernel,
        grid=(n // tb,),
        in_specs=[
            pl.BlockSpec((tb, D_MODEL), lambda i: (i, 0)),
            pl.BlockSpec((D_MODEL, tb), lambda i: (0, i)),
            pl.BlockSpec((1, N_MOD, D_MODEL), lambda i: (jnp.where(i < ctx_blocks, 0, 1), 0, 0)),
            pl.BlockSpec((1, D_MODEL), lambda i: (0, 0)),
            pl.BlockSpec((1, D_MODEL), lambda i: (0, 0)),
        ],
        out_specs=pl.BlockSpec((tb, D_MODEL), lambda i: (jnp.maximum(i - skip, 0), 0)),
        out_shape=jax.ShapeDtypeStruct((n - skip * tb, D_MODEL), F32),
        compiler_params=_cparams(("arbitrary",)),
        name="ln2",
    )(x1, ft, mods, ln_g, ln_b)


def _rope_tables(ctx_len, s_len):
    t = np.arange(s_len)
    pos = np.stack([t // GRID_W, t % GRID_W], axis=-1).astype(np.float64)
    quarter = HEAD_DIM // 4
    inv_freq = ROPE_THETA ** (-np.arange(quarter, dtype=np.float64) / quarter)
    ang = pos[:, :, None] * inv_freq
    cos = np.repeat(np.cos(ang), 2, axis=1).reshape(s_len, HEAD_DIM)
    sin = np.repeat(np.sin(ang), 2, axis=1).reshape(s_len, HEAD_DIM)
    first_half = (np.arange(HEAD_DIM) % 32) < 16
    sa = np.where(first_half, -sin, 0.0)
    sb = np.where(first_half, 0.0, sin)

    def full(tab, fill):
        tab = np.concatenate([np.full((ctx_len, HEAD_DIM), fill), tab], axis=0)
        return jnp.asarray(np.tile(tab, (1, 2)), F32)

    return full(cos, 1.0), full(sa, 0.0), full(sb, 0.0)


def kernel(x, c, ctx, c_ctx, w_mod, b_mod, w_in, q_norm, k_norm, conv_w, sink, w_branch, w_out,
           ln1_g, ln1_b, w_pq, sub_keys, u_tab, v_tab, ln2_g, ln2_b):
    bsz, s_len, d = x.shape
    ctx_len = ctx.shape[1]
    depth = w_mod.shape[0]
    assert bsz == 1 and d == D_MODEL and depth == DEPTH
    assert ctx_len % ROW_BLOCK == 0
    assert (ctx_len + s_len) % PEER_TOKENS == 0 and (ctx_len + s_len) % INPROJ_ROWS == 0
    ctx_blocks = ctx_len // ROW_BLOCK

    x_all = jnp.concatenate([ctx[0], x[0]], axis=0)
    cos_t, sa_t, sb_t = _rope_tables(ctx_len, s_len)

    cc = jnp.zeros((8, D_MODEL), F32).at[0].set(c_ctx).at[1].set(c[0])
    mod_all = _mod_call(cc, w_mod.astype(BF16), b_mod[:, None, :])
    u_all = u_tab.astype(BF16)
    vt_all = jnp.swapaxes(v_tab, 1, 2).astype(BF16)

    for layer in range(depth):
        mods = mod_all[layer, 0:2].reshape(2, N_MOD, D_MODEL)
        proj = _inproj_call(x_all, mods, w_in[layer].astype(BF16), ctx_len)
        qg = jnp.tile(q_norm[layer], 2)[None, :]
        kg = jnp.tile(k_norm[layer], 2)[None, :]
        qa, ka, va, qc, kc, vc = _prep_call(proj, cos_t, sa_t, sb_t, qg, kg)
        ya = _attn_a_call(qa, ka, va, ctx_len)
        yc = _attn_c_call(sink[layer], qc, kc, vc, ctx_len)
        x1, hq, qp = _merge_call(x_all, mods, ya, yc, proj, conv_w[layer],
                                 w_branch[layer].astype(BF16), w_out[layer].astype(BF16),
                                 ln1_g[layer][None, :], ln1_b[layer][None, :],
                                 w_pq[layer].astype(BF16), ctx_blocks)
        r2, w2, k1, w1 = _topk_call(qp, sub_keys[layer].astype(BF16))
        ft = _dense_call(hq, u_all, vt_all, layer, r2, w2, k1, w1)
        x_all = _ln2_call(x1, ft, mods, ln2_g[layer][None, :], ln2_b[layer][None, :], ctx_blocks,
                          latent_only=(layer == depth - 1))

    return x_all[None]
```

```python
import functools
import math

import jax
import jax.numpy as jnp
import numpy as np
from jax import lax
from jax.experimental import pallas as pl
from jax.experimental.pallas import tpu as pltpu

F32 = jnp.float32
BF16 = jnp.bfloat16

D_MODEL = 1024
HEAD_DIM = 64
N_HEADS = 8
N_KV = 2
GROUP = N_HEADS // N_KV
GRID_W = 64
WINDOW = 128
CONV_W = 3
B_WIDTH = 512
N_MOD = 6
ROPE_THETA = 10000.0
PEER_HEADS = 8
N_KEYS = 128
PEER_TOPK = 16
PEER_QDIM = 256
DEPTH = 2
IN_COLS = 6144
DEEPNORM_ALPHA = (2 * DEPTH) ** 0.25
LN_EPS = 1e-6
RMS_EPS = 1e-6
NEG_INF = -1e30
ATTN_SCALE = HEAD_DIM ** -0.5

ROW_BLOCK = 256
INPROJ_ROWS = 1280
WINDOW_ROWS = 128
KEY_CHUNK = 1024
BOUND_SLACK = 1.0 + 2.0 ** -6
SAFE_ROW_SUM = 1e-30
PEER_TOKENS = 1280
PEER_LANES = 256
TOPK_HEADS = 8
PEER_EXPERTS = 1024
VMEM_LIMIT = 56 * 1024 * 1024

CAND_COUNTS = tuple(PEER_TOPK // (j + 1) for j in range(PEER_TOPK))
N_CAND = sum(CAND_COUNTS)
CAND_ROWS = 56


def _cparams(sem):
    return pltpu.CompilerParams(dimension_semantics=sem, vmem_limit_bytes=VMEM_LIMIT)


def _nt_dot(a, b):
    return lax.dot_general(a, b, (((1,), (1,)), ((), ())), preferred_element_type=F32)


def _layer_norm(r, g, b):
    mu = jnp.mean(r, axis=-1, keepdims=True)
    rc = r - mu
    var = jnp.mean(rc * rc, axis=-1, keepdims=True)
    return rc * lax.rsqrt(var + LN_EPS) * g + b


def _mod_kernel(c_ref, w_ref, b_ref, o_ref):
    c = c_ref[...]
    h = (c * jax.nn.sigmoid(c)).astype(BF16)
    o_ref[0] = jnp.dot(h, w_ref[0], preferred_element_type=F32) + b_ref[0]


def _mod_call(cc, w_mod, b_mod):
    depth = w_mod.shape[0]
    cols = w_mod.shape[2]
    tn = 1024
    return pl.pallas_call(
        _mod_kernel,
        grid=(depth, cols // tn),
        in_specs=[
            pl.BlockSpec((8, D_MODEL), lambda l, j: (0, 0)),
            pl.BlockSpec((1, D_MODEL, tn), lambda l, j: (l, 0, j)),
            pl.BlockSpec((1, 1, tn), lambda l, j: (l, 0, j)),
        ],
        out_specs=pl.BlockSpec((1, 8, tn), lambda l, j: (l, 0, j)),
        out_shape=jax.ShapeDtypeStruct((depth, 8, cols), F32),
        compiler_params=_cparams(("arbitrary", "arbitrary")),
        name="mod",
    )(cc, w_mod, b_mod)


def _inproj_kernel(x_ref, mod_ref, w_ref, o_ref, hm_scr, *, ctx_len):
    tb = x_ref.shape[0]

    @pl.when(pl.program_id(1) == 0)
    def _():
        row = pl.program_id(0) * tb + lax.broadcasted_iota(jnp.int32, (tb, 1), 0)
        is_ctx = row < ctx_len
        sh = jnp.where(is_ctx, mod_ref[0, 0:1, :], mod_ref[1, 0:1, :])
        sc = jnp.where(is_ctx, mod_ref[0, 1:2, :], mod_ref[1, 1:2, :])
        hm_scr[...] = (x_ref[...] * (1.0 + sc) + sh).astype(BF16)

    o_ref[...] = jnp.dot(hm_scr[...], w_ref[...], preferred_element_type=F32)


def _inproj_call(x_all, mods, w_in, ctx_len):
    n = x_all.shape[0]
    tn = 1024
    tb = INPROJ_ROWS
    return pl.pallas_call(
        functools.partial(_inproj_kernel, ctx_len=ctx_len),
        grid=(n // tb, IN_COLS // tn),
        in_specs=[
            pl.BlockSpec((tb, D_MODEL), lambda i, j: (i, 0)),
            pl.BlockSpec((2, N_MOD, D_MODEL), lambda i, j: (0, 0, 0)),
            pl.BlockSpec((D_MODEL, tn), lambda i, j: (0, j)),
        ],
        out_specs=pl.BlockSpec((tb, tn), lambda i, j: (i, j)),
        out_shape=jax.ShapeDtypeStruct((n, IN_COLS), F32),
        scratch_shapes=[pltpu.VMEM((tb, D_MODEL), BF16)],
        compiler_params=_cparams(("arbitrary", "arbitrary")),
        name="inproj",
    )(x_all, mods, w_in)


def _prep_kernel(p_ref, cos_ref, sa_ref, sb_ref, qg_ref, kg_ref,
                 qa_o, ka_o, va_o, qc_o, kc_o, vc_o):
    x = p_ref[...]
    cos = cos_ref[...]
    sa = sa_ref[...]
    sb = sb_ref[...]

    def tile(t, width):
        reps = width // 128
        return t if reps == 1 else jnp.concatenate([t] * reps, axis=1)

    def rope(y):
        w = y.shape[1]
        return (y * tile(cos, w)
                + pltpu.roll(y, w - 16, 1) * tile(sa, w)
                + pltpu.roll(y, 16, 1) * tile(sb, w))

    def inv_rms(xh):
        return lax.rsqrt(jnp.mean(xh * xh, axis=-1, keepdims=True) + RMS_EPS)

    qa = x[:, 0:512]
    ka = x[:, 512:640]
    va = x[:, 640:768]
    qc = x[:, 768:1280]
    kc = x[:, 1280:1408]
    vc = x[:, 1408:1536]

    qa_r = rope(qa * tile(qg_ref[...], 512))
    ka_r = rope(ka * kg_ref[...])
    qc_r = rope(qc)
    kc_r = rope(kc)

    for h in range(N_HEADS):
        sl = slice(h * HEAD_DIM, (h + 1) * HEAD_DIM)
        qa_o[h, :, 0:HEAD_DIM] = (qa_r[:, sl] * (inv_rms(qa[:, sl]) * ATTN_SCALE)).astype(BF16)
        qa_o[h, :, HEAD_DIM:2 * HEAD_DIM] = jnp.zeros((x.shape[0], HEAD_DIM), BF16)
        qc_o[h] = (qc_r[:, sl] * ATTN_SCALE).astype(BF16)
    lane = lax.broadcasted_iota(jnp.int32, (x.shape[0], HEAD_DIM), 1)
    ones_col = jnp.where(lane == 0, 1.0, 0.0).astype(BF16)
    for h in range(N_KV):
        sl = slice(h * HEAD_DIM, (h + 1) * HEAD_DIM)
        ka_o[h, :, 0:HEAD_DIM] = (ka_r[:, sl] * inv_rms(ka[:, sl])).astype(BF16)
        ka_o[h, :, HEAD_DIM:2 * HEAD_DIM] = ones_col
        va_o[h, :, 0:HEAD_DIM] = va[:, sl].astype(BF16)
        va_o[h, :, HEAD_DIM:2 * HEAD_DIM] = ones_col
        kc_o[h] = kc_r[:, sl].astype(BF16)
        vc_o[h, :, 0:HEAD_DIM] = vc[:, sl].astype(BF16)
        vc_o[h, :, HEAD_DIM:2 * HEAD_DIM] = ones_col


def _prep_call(proj, cos_t, sa_t, sb_t, qg, kg):
    n = proj.shape[0]
    tb = ROW_BLOCK
    tab = pl.BlockSpec((tb, 128), lambda i: (i, 0))
    gain = pl.BlockSpec((1, 128), lambda i: (0, 0))

    def hm(nh, width=HEAD_DIM):
        return pl.BlockSpec((nh, tb, width), lambda i: (0, i, 0))

    def hshape(nh, width=HEAD_DIM):
        return jax.ShapeDtypeStruct((nh, n, width), BF16)

    return pl.pallas_call(
        _prep_kernel,
        grid=(n // tb,),
        in_specs=[pl.BlockSpec((tb, 1536), lambda i: (i, 0)), tab, tab, tab, gain, gain],
        out_specs=[hm(N_HEADS, 2 * HEAD_DIM), hm(N_KV, 2 * HEAD_DIM), hm(N_KV, 2 * HEAD_DIM),
                   hm(N_HEADS), hm(N_KV), hm(N_KV, 2 * HEAD_DIM)],
        out_shape=[hshape(N_HEADS, 2 * HEAD_DIM), hshape(N_KV, 2 * HEAD_DIM), hshape(N_KV, 2 * HEAD_DIM),
                   hshape(N_HEADS), hshape(N_KV), hshape(N_KV, 2 * HEAD_DIM)],
        compiler_params=_cparams(("arbitrary",)),
        name="prep",
    )(proj, cos_t, sa_t, sb_t, qg, kg)


def _store_heads(o_ref, o, tq):
    for g in range(GROUP):
        o_ref[:, g * HEAD_DIM:(g + 1) * HEAD_DIM] = o[g * tq:(g + 1) * tq, :].astype(BF16)


def _attn_a_kernel(q_ref, k_ref, v_ref, o_ref, s_scr, mx_scr, m_scr, acc_scr, kmax_scr, *,
                   ctx_len, ctx_blocks, lat_chunks, unroll):
    i = pl.program_id(1)
    tq = q_ref.shape[1]
    rows = GROUP * tq
    key_chunk = s_scr.shape[2]
    q0 = q_ref[...].reshape(rows, 2 * HEAD_DIM)

    @pl.when(i == 0)
    def _():
        kk = k_ref[0, :, 0:HEAD_DIM].astype(F32)
        k_norm2 = jnp.max(jnp.sum(kk * kk, axis=-1, keepdims=True), axis=0, keepdims=True)
        kmax_scr[...] = jnp.broadcast_to(jnp.sqrt(k_norm2), kmax_scr.shape)

    def chunk_off(c):
        return pl.multiple_of(ctx_len + c * key_chunk, 128)

    def run(q, online):
        def scores(slot, off, width):
            s = _nt_dot(q, k_ref[0, pl.ds(off, width), :])
            s_scr[slot, :, 0:width] = s
            if online:
                mx_scr[slot] = jnp.broadcast_to(jnp.max(s, axis=-1, keepdims=True), (rows, 128))

        def update(slot, off, width):
            s = s_scr[slot, :, 0:width]
            v = v_ref[0, pl.ds(off, width), :]
            if online:
                m_old = m_scr[...]
                m_new = jnp.maximum(m_old, mx_scr[slot])
                p = jnp.exp(s - jnp.concatenate([m_new] * (width // 128), axis=1))
                acc_scr[...] = jnp.exp(m_old - m_new) * acc_scr[...] + jnp.dot(
                    p.astype(BF16), v, preferred_element_type=F32)
                m_scr[...] = m_new
            else:
                acc_scr[...] += jnp.dot(jnp.exp(s).astype(BF16), v, preferred_element_type=F32)

        if online:
            m_scr[...] = jnp.full(m_scr.shape, -jnp.inf, F32)
        acc_scr[...] = jnp.zeros(acc_scr.shape, F32)
        scores(0, 0, ctx_len)
        update(0, 0, ctx_len)

        @pl.when(i >= ctx_blocks)
        def _():
            scores(0, chunk_off(0), key_chunk)

            def body(it, carry):
                for u in range(0, unroll, 2):
                    c = unroll * it + u
                    scores(1, chunk_off(c + 1), key_chunk)
                    update(0, chunk_off(c), key_chunk)
                    scores(0, chunk_off(jnp.minimum(c + 2, lat_chunks - 1)), key_chunk)
                    update(1, chunk_off(c + 1), key_chunk)
                return carry

            lax.fori_loop(0, lat_chunks // unroll, body, 0)

        acc = acc_scr[...]
        _store_heads(o_ref, acc[:, 0:HEAD_DIM] / acc[:, HEAD_DIM:HEAD_DIM + 1], tq)

    qf = q0.astype(F32)
    q_norm = jnp.sqrt(jnp.sum(qf * qf, axis=-1, keepdims=True))
    bound = (q_norm * kmax_scr[0:1, 0:1] * BOUND_SLACK).astype(BF16).astype(F32)
    shift_lane = lax.broadcasted_iota(jnp.int32, qf.shape, 1) == HEAD_DIM
    run(jnp.where(shift_lane, -bound, qf).astype(BF16), online=False)

    @pl.when(jnp.logical_not(jnp.min(acc_scr[:, HEAD_DIM:HEAD_DIM + 1]) > SAFE_ROW_SUM))
    def _():
        run(q0, online=True)


def _attn_a_call(q, k, v, ctx_len):
    n = q.shape[1]
    tq = ROW_BLOCK
    s_len = n - ctx_len
    key_chunk = KEY_CHUNK if s_len % (2 * KEY_CHUNK) == 0 else KEY_CHUNK // 2
    lat_chunks = s_len // key_chunk
    assert s_len % key_chunk == 0 and lat_chunks % 2 == 0
    unroll = max(u for u in (8, 4, 2) if lat_chunks % u == 0)
    kern = functools.partial(_attn_a_kernel, ctx_len=ctx_len, ctx_blocks=ctx_len // tq, lat_chunks=lat_chunks,
                             unroll=unroll)
    return pl.pallas_call(
        kern,
        grid=(N_KV, n // tq),
        in_specs=[
            pl.BlockSpec((GROUP, tq, 2 * HEAD_DIM), lambda j, i: (j, i, 0)),
            pl.BlockSpec((1, n, 2 * HEAD_DIM), lambda j, i: (j, 0, 0)),
            pl.BlockSpec((1, n, 2 * HEAD_DIM), lambda j, i: (j, 0, 0)),
        ],
        out_specs=pl.BlockSpec((tq, GROUP * HEAD_DIM), lambda j, i: (i, j)),
        out_shape=jax.ShapeDtypeStruct((n, N_HEADS * HEAD_DIM), BF16),
        scratch_shapes=[
            pltpu.VMEM((2, GROUP * tq, key_chunk), F32),
            pltpu.VMEM((2, GROUP * tq, 128), F32),
            pltpu.VMEM((GROUP * tq, 128), F32),
            pltpu.VMEM((GROUP * tq, 2 * HEAD_DIM), F32),
            pltpu.VMEM((8, 128), F32),
        ],
        compiler_params=_cparams(("arbitrary", "arbitrary")),
        name="attn_global",
    )(q, k, v)


def _attn_c_kernel(sink_ref, q_ref, k_ref, v_ref, o_ref, *, ctx_len, ctx_blocks, s_len):
    j = pl.program_id(0)
    i = pl.program_id(1)
    tq = q_ref.shape[1]
    rows = GROUP * tq
    n_loc = tq + 2 * WINDOW
    q = q_ref[...].reshape(rows, HEAD_DIM)

    p0 = i * tq - ctx_len
    ws = pl.multiple_of(jnp.clip(p0 - WINDOW, 0, s_len - n_loc), 128)
    k_loc = k_ref[0, pl.ds(ctx_len + ws, n_loc), :]
    v_loc = v_ref[0, pl.ds(ctx_len + ws, n_loc), :]
    k_ctx = k_ref[0, 0:ctx_len, :]
    v_ctx = v_ref[0, 0:ctx_len, :]

    dist = (lax.broadcasted_iota(jnp.int32, (tq, n_loc), 0) - lax.broadcasted_iota(jnp.int32, (tq, n_loc), 1)
            + (p0 - ws))
    valid = (jnp.abs(dist) <= WINDOW) & (i >= ctx_blocks)
    s_loc = jnp.where(valid[None], _nt_dot(q, k_loc).reshape(GROUP, tq, n_loc), NEG_INF).reshape(rows, n_loc)
    s_ctx = _nt_dot(q, k_ctx)

    grp = lax.broadcasted_iota(jnp.int32, (rows, 1), 0) // tq
    sink = jnp.full((rows, 1), sink_ref[j * GROUP + GROUP - 1], F32)
    for g in range(GROUP - 1):
        sink = jnp.where(grp == g, sink_ref[j * GROUP + g], sink)

    m = jnp.maximum(jnp.maximum(jnp.max(s_loc, axis=-1, keepdims=True),
                                jnp.max(s_ctx, axis=-1, keepdims=True)), sink)
    acc = (jnp.dot(jnp.exp(s_loc - m).astype(BF16), v_loc, preferred_element_type=F32)
           + jnp.dot(jnp.exp(s_ctx - m).astype(BF16), v_ctx, preferred_element_type=F32))
    o = acc[:, 0:HEAD_DIM] / (acc[:, HEAD_DIM:HEAD_DIM + 1] + jnp.exp(sink - m))
    _store_heads(o_ref, o, tq)


def _attn_c_call(sink, q, k, v, ctx_len):
    n = q.shape[1]
    tq = WINDOW_ROWS
    kern = functools.partial(_attn_c_kernel, ctx_len=ctx_len, ctx_blocks=ctx_len // tq, s_len=n - ctx_len)
    return pl.pallas_call(
        kern,
        grid=(N_KV, n // tq),
        in_specs=[
            pl.BlockSpec(memory_space=pltpu.SMEM),
            pl.BlockSpec((GROUP, tq, HEAD_DIM), lambda j, i: (j, i, 0)),
            pl.BlockSpec((1, n, HEAD_DIM), lambda j, i: (j, 0, 0)),
            pl.BlockSpec((1, n, 2 * HEAD_DIM), lambda j, i: (j, 0, 0)),
        ],
        out_specs=pl.BlockSpec((tq, GROUP * HEAD_DIM), lambda j, i: (i, j)),
        out_shape=jax.ShapeDtypeStruct((n, N_HEADS * HEAD_DIM), BF16),
        compiler_params=_cparams(("arbitrary", "arbitrary")),
        name="attn_window",
    )(sink, q, k, v)


def _merge_kernel(x_ref, mod_ref, ya_ref, yc_ref, xb_ref, gb_ref, gc_ref,
                  xbp_ref, gcp_ref, xbn_ref, gcn_ref, gl_ref, cw_ref, wb_ref, wo_ref,
                  lg_ref, lb_ref, wpq_ref, x1_ref, hq_ref, qp_ref, *, n_blocks, ctx_blocks):
    i = pl.program_id(0)
    tb = x_ref.shape[0]

    z = gc_ref[...] * xb_ref[...]
    z_before = gcp_ref[7:8, :] * xbp_ref[7:8, :]
    z_after = gcn_ref[0:1, :] * xbn_ref[0:1, :]
    seq_start = (i == 0) | (i == ctx_blocks)
    seq_end = (i == ctx_blocks - 1) | (i == n_blocks - 1)
    z_before = jnp.where(seq_start, 0.0, z_before)
    z_after = jnp.where(seq_end, 0.0, z_after)
    rid = lax.broadcasted_iota(jnp.int32, z.shape, 0)
    z_prev = jnp.where(rid == 0, z_before, pltpu.roll(z, 1, 0))
    z_next = jnp.where(rid == tb - 1, z_after, pltpu.roll(z, tb - 1, 0))
    cw = cw_ref[...]
    yb = gb_ref[...] * (cw[0:1, :] * z_prev + cw[1:2, :] * z + cw[2:3, :] * z_next)

    pa = jnp.dot(ya_ref[...], wb_ref[0], preferred_element_type=F32)
    pb = jnp.dot(yb.astype(BF16), wb_ref[1], preferred_element_type=F32)
    pc = jnp.dot(yc_ref[...], wb_ref[2], preferred_element_type=F32)
    gates = jax.nn.sigmoid(gl_ref[...])
    m = (gates[:, 0:D_MODEL] * pa + gates[:, D_MODEL:2 * D_MODEL] * pb
         + gates[:, 2 * D_MODEL:3 * D_MODEL] * pc)
    y = jnp.dot(m.astype(BF16), wo_ref[...], preferred_element_type=F32)

    g1 = mod_ref[0, 2:3, :]
    sh2 = mod_ref[0, 3:4, :]
    sc2 = mod_ref[0, 4:5, :]
    x1 = _layer_norm(DEEPNORM_ALPHA * x_ref[...] + g1 * y, lg_ref[...], lb_ref[...])
    x1_ref[...] = x1
    hq = (x1 * (1.0 + sc2) + sh2).astype(BF16)
    hq_ref[...] = hq
    qp_ref[...] = jnp.dot(hq, wpq_ref[...], preferred_element_type=F32).astype(BF16)


def _merge_call(x_all, mods, ya, yc, proj, conv_w, w_branch, w_out, ln_g, ln_b, w_pq, ctx_blocks):
    n = x_all.shape[0]
    tb = ROW_BLOCK
    nb = n // tb
    halo = tb // 8
    last8 = n // 8 - 1
    kern = functools.partial(_merge_kernel, n_blocks=nb, ctx_blocks=ctx_blocks)
    qcols = PEER_HEADS * PEER_QDIM

    def full(shape):
        return pl.BlockSpec(shape, lambda i: (0,) * len(shape))

    def cols512(c):
        return pl.BlockSpec((tb, B_WIDTH), lambda i: (i, c))

    def before(c):
        return pl.BlockSpec((8, B_WIDTH), lambda i: (jnp.maximum(i * halo - 1, 0), c))

    def after(c):
        return pl.BlockSpec((8, B_WIDTH), lambda i: (jnp.minimum((i + 1) * halo, last8), c))

    return pl.pallas_call(
        kern,
        grid=(nb,),
        in_specs=[
            pl.BlockSpec((tb, D_MODEL), lambda i: (i, 0)),
            pl.BlockSpec((1, N_MOD, D_MODEL), lambda i: (jnp.where(i < ctx_blocks, 0, 1), 0, 0)),
            pl.BlockSpec((tb, N_HEADS * HEAD_DIM), lambda i: (i, 0)),
            pl.BlockSpec((tb, N_HEADS * HEAD_DIM), lambda i: (i, 0)),
            cols512(3), cols512(4), cols512(5),
            before(3), before(5), after(3), after(5),
            pl.BlockSpec((tb, 3 * D_MODEL), lambda i: (i, 1)),
            full((CONV_W, B_WIDTH)),
            full((3, B_WIDTH, D_MODEL)),
            full((D_MODEL, D_MODEL)),
            full((1, D_MODEL)), full((1, D_MODEL)),
            full((D_MODEL, qcols)),
        ],
        out_specs=[
            pl.BlockSpec((tb, D_MODEL), lambda i: (i, 0)),
            pl.BlockSpec((tb, D_MODEL), lambda i: (i, 0)),
            pl.BlockSpec((tb, qcols), lambda i: (i, 0)),
        ],
        out_shape=[
            jax.ShapeDtypeStruct((n, D_MODEL), F32),
            jax.ShapeDtypeStruct((n, D_MODEL), BF16),
            jax.ShapeDtypeStruct((n, qcols), BF16),
        ],
        compiler_params=_cparams(("arbitrary",)),
        name="merge",
    )(x_all, mods, ya, yc, proj, proj, proj, proj, proj, proj, proj, proj,
      conv_w, w_branch, w_out, ln_g, ln_b, w_pq)


def _extract_rounds(chains, rounds, on_round, tie_break):
    rows = chains[0].shape[0]
    ridx = lax.broadcasted_iota(jnp.int32, chains[0].shape, 0)
    chains = list(chains)
    ranks = [jnp.full(c.shape, rounds, jnp.int32) for c in chains]
    for r in range(rounds):
        for i, s in enumerate(chains):
            m = jnp.max(s, axis=0, keepdims=True)
            if tie_break:
                first = jnp.min(jnp.where(s == m, ridx, rows), axis=0, keepdims=True)
                hit = ridx == first
            else:
                hit = s == m
            ranks[i] = jnp.where(hit, r, ranks[i])
            chains[i] = jnp.where(hit, -jnp.inf, s)
            on_round(i, r, m)
    return ranks


def _extract_topk(chains, rounds, on_round, rank_ref):
    ranks = _extract_rounds(chains, rounds, on_round, tie_break=False)
    most = jnp.float32(0.0)
    for i, rank in enumerate(ranks):
        rank_ref[i] = rank
        most = jnp.maximum(most, jnp.max(jnp.sum((rank < rounds).astype(F32), axis=0, keepdims=True)))

    @pl.when(most > rounds)
    def _():
        for i, rank in enumerate(_extract_rounds(chains, rounds, on_round, tie_break=True)):
            rank_ref[i] = rank


def _topk_kernel(qp_ref, sk_ref, r2_o, w2_o, k1_o, w1_o, tv_scr, cand_scr, kj_scr, rank_scr, crank_scr):
    tt = qp_ref.shape[0]
    half = PEER_QDIM // 2
    q = qp_ref[...]
    heads = range(TOPK_HEADS)

    scores = [_nt_dot(sk_ref[h, p], q[:, (2 * h + p) * half:(2 * h + p + 1) * half])
              for h in heads for p in range(2)]

    def keep(i, r, m):
        tv_scr[i, r:r + 1, :] = m

    _extract_topk(scores, PEER_TOPK, keep, rank_scr)

    for h in heads:
        tv1 = tv_scr[2 * h]
        tv2 = tv_scr[2 * h + 1]
        off = 0
        for j, cnt in enumerate(CAND_COUNTS):
            cand_scr[h, off:off + cnt, :] = tv1[j:j + 1, :] + tv2[0:cnt, :]
            off += cnt
        cand_scr[h, N_CAND:CAND_ROWS, :] = jnp.full((CAND_ROWS - N_CAND, tt), -jnp.inf, F32)
    cands = [cand_scr[h] for h in heads]
    _extract_topk(cands, PEER_TOPK, lambda i, r, m: None, crank_scr)

    def twice_bf16(x):
        hi = lax.bitcast_convert_type(x.astype(BF16).astype(F32), jnp.uint32)
        return lax.bitcast_convert_type(hi | (hi >> 16), F32)

    for h in heads:
        cand = cands[h]
        chosen = crank_scr[h] < PEER_TOPK
        cmax = cand[0:1, :]
        z = jnp.sum(jnp.where(chosen, jnp.exp(cand - cmax), 0.0), axis=0, keepdims=True)
        cnt_f = chosen.astype(F32)
        off = 0
        for j, cnt in enumerate(CAND_COUNTS):
            kj_scr[h, j:j + 1, :] = jnp.sum(cnt_f[off:off + cnt, :], axis=0, keepdims=True)
            off += cnt

        rank1 = rank_scr[2 * h]
        rank2 = rank_scr[2 * h + 1]
        k1 = jnp.zeros((N_KEYS, tt), F32)
        for j in range(PEER_TOPK):
            k1 = jnp.where(rank1 == j, kj_scr[h, j:j + 1, :], k1)
        k1_o[h] = twice_bf16(k1)
        w1_o[h] = twice_bf16(
            jnp.where(rank1 < PEER_TOPK, jnp.exp(scores[2 * h] - tv_scr[2 * h, 0:1, :]), 0.0) / z)
        r2_o[h] = rank2.astype(F32).astype(BF16)
        w2_o[h] = jnp.where(rank2 < PEER_TOPK, jnp.exp(scores[2 * h + 1] - tv_scr[2 * h + 1, 0:1, :]),
                            0.0).astype(BF16)


def _topk_call(qp, sub_keys):
    n = qp.shape[0]
    tt = ROW_BLOCK
    hp = TOPK_HEADS
    out = pl.BlockSpec((hp, N_KEYS, tt), lambda t, h: (h, 0, t))
    shape = jax.ShapeDtypeStruct((PEER_HEADS, N_KEYS, n), F32)
    shape_bf16 = jax.ShapeDtypeStruct((PEER_HEADS, N_KEYS, n), BF16)
    return pl.pallas_call(
        _topk_kernel,
        grid=(n // tt, PEER_HEADS // hp),
        in_specs=[
            pl.BlockSpec((tt, hp * PEER_QDIM), lambda t, h: (t, h)),
            pl.BlockSpec((hp, 2, N_KEYS, PEER_QDIM // 2), lambda t, h: (h, 0, 0, 0)),
        ],
        out_specs=[out, out, out, out],
        out_shape=[shape_bf16, shape_bf16, shape, shape],
        scratch_shapes=[
            pltpu.VMEM((2 * hp, PEER_TOPK, tt), F32),
            pltpu.VMEM((hp, CAND_ROWS, tt), F32),
            pltpu.VMEM((hp, PEER_TOPK, tt), F32),
            pltpu.VMEM((2 * hp, N_KEYS, tt), jnp.int32),
            pltpu.VMEM((hp, CAND_ROWS, tt), jnp.int32),
        ],
        compiler_params=_cparams(("arbitrary", "arbitrary")),
        name="peer_topk",
    )(qp, sub_keys)


_ERF_A = (-2.72614225801306e-10, 2.77068142495902e-08, -2.10102402082508e-06, -5.69250639462346e-05,
          -7.34990630326855e-04, -2.95459980854025e-03, -1.60960333262415e-02)
_ERF_B = (-1.45660718464996e-05, -2.13374055278905e-04, -1.68282697438203e-03, -7.37332916720468e-03,
          -1.42647390514189e-02)


def _gelu(x):
    dt = x.dtype
    lim = 4.0 * math.sqrt(2.0)
    xc = jnp.clip(x, -lim, lim)
    t = xc * xc
    deg_a = len(_ERF_A) - 1
    deg_b = len(_ERF_B) - 1
    a2 = [c * (0.5 / math.sqrt(2.0)) / 2.0 ** (deg_a - i) for i, c in enumerate(_ERF_A)]
    b2 = [c / 2.0 ** (deg_b - i) for i, c in enumerate(_ERF_B)]
    a2 = [c / b2[0] for c in a2]
    b2 = [c / b2[0] for c in b2]
    a = jnp.asarray(a2[0], dt)
    for c in a2[1:]:
        a = a * t + jnp.asarray(c, dt)
    b = t + jnp.asarray(b2[1], dt)
    for c in b2[2:]:
        b = b * t + jnp.asarray(c, dt)
    return x * (0.5 + xc * (a / b))


def _dense_kernel(hq_ref, u_ref, vt_ref, r2_ref, w2_ref, k1_ref, w1_ref, o_ref, h_scr, a_scr, *, n_eb):
    s = pl.program_id(0)
    eb = u_ref.shape[0]
    tt = hq_ref.shape[0]
    n_i1 = eb // N_KEYS

    @pl.when(s == 0)
    def _():
        h_scr[...] = jnp.zeros(h_scr.shape, BF16)
        a_scr[...] = jnp.zeros(a_scr.shape, BF16)

    @pl.when(jnp.maximum(s - 2, 0) % n_eb == 0)
    def _():
        o_ref[...] = jnp.zeros(o_ref.shape, F32)

    def rows16(ref, h, ii, cs):
        row = ref[h, ii:ii + 1, cs]
        return pltpu.bitcast(jnp.tile(row, (8, 1)), BF16)

    for c in range(tt // PEER_LANES):
        cs = slice(c * PEER_LANES, (c + 1) * PEER_LANES)

        o_ref[:, cs] += jnp.dot(vt_ref[...], a_scr[:, cs], preferred_element_type=F32)

        for ii in range(n_i1):
            rows = slice(ii * N_KEYS, (ii + 1) * N_KEYS)
            g = None
            for h in range(PEER_HEADS):
                k1b = jnp.concatenate([rows16(k1_ref, h, ii, cs)] * (N_KEYS // 16), axis=0)
                w1b = jnp.concatenate([rows16(w1_ref, h, ii, cs)] * (N_KEYS // 16), axis=0)
                keep = r2_ref[h, :, cs] < k1b
                gate = jnp.where(keep, w2_ref[h, :, cs] * w1b, 0.0)
                g = gate if g is None else g + gate
            a_scr[rows, cs] = _gelu(h_scr[rows, cs]) * g

        h_scr[:, cs] = _nt_dot(u_ref[...], hq_ref[cs, :]).astype(BF16)


def _dense_call(hq, u_tab, vt_tab, layer, r2, w2, k1, w1):
    n = hq.shape[0]
    n_exp = u_tab.shape[1]
    tt = PEER_TOKENS
    eb = PEER_EXPERTS
    n_eb = n_exp // eb
    steps = (n // tt) * n_eb

    def blk(s, lag):
        return jnp.clip(s - lag, 0, steps - 1)

    tab = pl.BlockSpec((PEER_HEADS, N_KEYS, tt), lambda s: (0, 0, blk(s, 1) // n_eb))
    row = pl.BlockSpec((PEER_HEADS, eb // N_KEYS, tt), lambda s: (0, blk(s, 1) % n_eb, blk(s, 1) // n_eb))
    return pl.pallas_call(
        functools.partial(_dense_kernel, n_eb=n_eb),
        grid=(steps + 2,),
        in_specs=[
            pl.BlockSpec((tt, D_MODEL), lambda s: (blk(s, 0) // n_eb, 0)),
            pl.BlockSpec((None, eb, D_MODEL), lambda s: (layer, blk(s, 0) % n_eb, 0)),
            pl.BlockSpec((None, D_MODEL, eb), lambda s: (layer, 0, blk(s, 2) % n_eb)),
            tab, tab, row, row,
        ],
        out_specs=pl.BlockSpec((D_MODEL, tt), lambda s: (0, blk(s, 2) // n_eb)),
        out_shape=jax.ShapeDtypeStruct((D_MODEL, n), F32),
        scratch_shapes=[pltpu.VMEM((eb, tt), BF16), pltpu.VMEM((eb, tt), BF16)],
        compiler_params=_cparams(("arbitrary",)),
        name="peer_dense",
    )(hq, u_tab, vt_tab, r2, w2, k1, w1)


def _ln2_kernel(x1_ref, ft_ref, mod_ref, lg_ref, lb_ref, o_ref):
    g2 = mod_ref[0, 5:6, :]
    f = ft_ref[...].T
    o_ref[...] = _layer_norm(DEEPNORM_ALPHA * x1_ref[...] + g2 * f, lg_ref[...], lb_ref[...])


def _ln2_call(x1, ft, mods, ln_g, ln_b, ctx_blocks, latent_only):
    n = x1.shape[0]
    tb = ROW_BLOCK
    skip = ctx_blocks if latent_only else 0
    return pl.pallas_call(
        _ln2_kernel,
        grid=(n // tb,),
        in_specs=[
            pl.BlockSpec((tb, D_MODEL), lambda i: (i, 0)),
            pl.BlockSpec((D_MODEL, tb), lambda i: (0, i)),
            pl.BlockSpec((1, N_MOD, D_MODEL), lambda i: (jnp.where(i < ctx_blocks, 0, 1), 0, 0)),
            pl.BlockSpec((1, D_MODEL), lambda i: (0, 0)),
            pl.BlockSpec((1, D_MODEL), lambda i: (0, 0)),
        ],
        out_specs=pl.BlockSpec((tb, D_MODEL), lambda i: (jnp.maximum(i - skip, 0), 0)),
        out_shape=jax.ShapeDtypeStruct((n - skip * tb, D_MODEL), F32),
        compiler_params=_cparams(("arbitrary",)),
        name="ln2",
    )(x1, ft, mods, ln_g, ln_b)


def _rope_tables(ctx_len, s_len):
    t = np.arange(s_len)
    pos = np.stack([t // GRID_W, t % GRID_W], axis=-1).astype(np.float64)
    quarter = HEAD_DIM // 4
    inv_freq = ROPE_THETA ** (-np.arange(quarter, dtype=np.float64) / quarter)
    ang = pos[:, :, None] * inv_freq
    cos = np.repeat(np.cos(ang), 2, axis=1).reshape(s_len, HEAD_DIM)
    sin = np.repeat(np.sin(ang), 2, axis=1).reshape(s_len, HEAD_DIM)
    first_half = (np.arange(HEAD_DIM) % 32) < 16
    sa = np.where(first_half, -sin, 0.0)
    sb = np.where(first_half, 0.0, sin)

    def full(tab, fill):
        tab = np.concatenate([np.full((ctx_len, HEAD_DIM), fill), tab], axis=0)
        return jnp.asarray(np.tile(tab, (1, 2)), F32)

    return full(cos, 1.0), full(sa, 0.0), full(sb, 0.0)


def kernel(x, c, ctx, c_ctx, w_mod, b_mod, w_in, q_norm, k_norm, conv_w, sink, w_branch, w_out,
           ln1_g, ln1_b, w_pq, sub_keys, u_tab, v_tab, ln2_g, ln2_b):
    bsz, s_len, d = x.shape
    ctx_len = ctx.shape[1]
    depth = w_mod.shape[0]
    assert bsz == 1 and d == D_MODEL and depth == DEPTH
    assert ctx_len % ROW_BLOCK == 0
    assert (ctx_len + s_len) % PEER_TOKENS == 0 and (ctx_len + s_len) % INPROJ_ROWS == 0
    ctx_blocks = ctx_len // ROW_BLOCK

    x_all = jnp.concatenate([ctx[0], x[0]], axis=0)
    cos_t, sa_t, sb_t = _rope_tables(ctx_len, s_len)

    cc = jnp.zeros((8, D_MODEL), F32).at[0].set(c_ctx).at[1].set(c[0])
    mod_all = _mod_call(cc, w_mod.astype(BF16), b_mod[:, None, :])
    u_all = u_tab.astype(BF16)
    vt_all = jnp.swapaxes(v_tab, 1, 2).astype(BF16)

    for layer in range(depth):
        mods = mod_all[layer, 0:2].reshape(2, N_MOD, D_MODEL)
        proj = _inproj_call(x_all, mods, w_in[layer].astype(BF16), ctx_len)
        qg = jnp.tile(q_norm[layer], 2)[None, :]
        kg = jnp.tile(k_norm[layer], 2)[None, :]
        qa, ka, va, qc, kc, vc = _prep_call(proj, cos_t, sa_t, sb_t, qg, kg)
        ya = _attn_a_call(qa, ka, va, ctx_len)
        yc = _attn_c_call(sink[layer], qc, kc, vc, ctx_len)
        x1, hq, qp = _merge_call(x_all, mods, ya, yc, proj, conv_w[layer],
                                 w_branch[layer].astype(BF16), w_out[layer].astype(BF16),
                                 ln1_g[layer][None, :], ln1_b[layer][None, :],
                                 w_pq[layer].astype(BF16), ctx_blocks)
        r2, w2, k1, w1 = _topk_call(qp, sub_keys[layer].astype(BF16))
        ft = _dense_call(hq, u_all, vt_all, layer, r2, w2, k1, w1)
        x_all = _ln2_call(x1, ft, mods, ln2_g[layer][None, :], ln2_b[layer][None, :], ctx_blocks,
                          latent_only=(layer == depth - 1))

    return x_all[None]
```
